```python
import math, functools
import jax, jax.numpy as jnp
from jax import lax
import numpy as np

D_MODEL = 1024
BATCH = 32
SEQ = 256
DEPTH = 2
DEC_BATCH = 8
DEC_SEQ = 1024
PAST_LEN = 256

GRID_W = 64
N_HEADS = 16
N_KV_HEADS = 4
HEAD_DIM = D_MODEL // N_HEADS
KV_GROUP = N_HEADS // N_KV_HEADS
ROPE_THETA = 10000.0
ROPE_PAIRS_PER_AXIS = HEAD_DIM // 4
Q_BLOCK = 128
WIN_R = 8
WIN_C = 16
N_EXPERTS = 16
EC_CAPACITY_FACTOR = 2
EXPERT_FF = 2 * D_MODEL
N_MIXERS = 2
N_ATTN_LAYERS = (DEPTH + 1) // 2
N_NA_LAYERS = DEPTH // 2
RMS_EPS = 1e-6
NEG_INF = -1e30

kernel_name = 'hybrid_diffusion_gqa_natten_ec_moe_step'


def rms_norm(x, gain):
    xf = x.astype(jnp.float32)
    y = xf * lax.rsqrt(jnp.mean(xf * xf, axis=-1, keepdims=True) + RMS_EPS)
    return (y * gain.astype(jnp.float32)).astype(x.dtype)


def ada_modulation(cond, w_ada, b_ada):
    m = jnp.einsum('bd,de->be', jax.nn.silu(cond), w_ada) + b_ada
    return jnp.split(m[:, None, :], 6, axis=-1)


def modulate(h, shift, scale):
    return h * (1.0 + scale) + shift


def axial_rope_tables(n_tokens):
    t = jnp.arange(n_tokens)
    row = (t // GRID_W).astype(jnp.float32)
    col = (t % GRID_W).astype(jnp.float32)
    inv_freq = ROPE_THETA ** (-jnp.arange(ROPE_PAIRS_PER_AXIS, dtype=jnp.float32) / ROPE_PAIRS_PER_AXIS)
    ang = jnp.concatenate([row[:, None] * inv_freq, col[:, None] * inv_freq], axis=-1)
    return jnp.cos(ang)[:, None, :], jnp.sin(ang)[:, None, :]


def apply_rope(x, cos, sin):
    x1, x2 = jnp.split(x, 2, axis=-1)
    cos = cos.astype(x.dtype)
    sin = sin.astype(x.dtype)
    return jnp.concatenate([x1 * cos - x2 * sin, x2 * cos + x1 * sin], axis=-1)


def blocked_attention(q, k, v):
    b, t, hkv, g, hd = q.shape
    nb = t // Q_BLOCK
    scale = hd ** -0.5
    qb = jnp.moveaxis(q.reshape(b, nb, Q_BLOCK, hkv, g, hd), 1, 0)

    def one_block(qi):
        s = jnp.einsum('bqhgd,bkhd->bhgqk', qi, k).astype(jnp.float32) * scale
        p = jax.nn.softmax(s, axis=-1).astype(v.dtype)
        return jnp.einsum('bhgqk,bkhd->bqhgd', p, v)

    out = lax.map(one_block, qb)
    return jnp.moveaxis(out, 0, 1).reshape(b, t, hkv, g, hd)


def gqa_project(h, w_qkv, q_gain, k_gain):
    b, t, _ = h.shape
    qkv = jnp.einsum('btd,de->bte', h, w_qkv)
    q, k, v = jnp.split(qkv, [N_HEADS * HEAD_DIM, (N_HEADS + N_KV_HEADS) * HEAD_DIM], axis=-1)
    q = rms_norm(q.reshape(b, t, N_HEADS, HEAD_DIM), q_gain)
    k = rms_norm(k.reshape(b, t, N_KV_HEADS, HEAD_DIM), k_gain)
    v = v.reshape(b, t, N_KV_HEADS, HEAD_DIM)
    return q, k, v


def gqa_context(h, w_qkv, q_gain, k_gain, w_o):
    b, t, _ = h.shape
    q, k, v = gqa_project(h, w_qkv, q_gain, k_gain)
    o = blocked_attention(q.reshape(b, t, N_KV_HEADS, KV_GROUP, HEAD_DIM), k, v)
    o = jnp.einsum('bte,ed->btd', o.reshape(b, t, N_HEADS * HEAD_DIM), w_o)
    return o, k, v


def gqa_latent(h, ctx_k, ctx_v, cos, sin, w_qkv, q_gain, k_gain, w_o):
    b, t, _ = h.shape
    q, k, v = gqa_project(h, w_qkv, q_gain, k_gain)
    q = apply_rope(q, cos, sin)
    k = apply_rope(k, cos, sin)
    keys = jnp.concatenate([k, ctx_k.astype(k.dtype)], axis=1)
    vals = jnp.concatenate([v, ctx_v.astype(v.dtype)], axis=1)
    o = blocked_attention(q.reshape(b, t, N_KV_HEADS, KV_GROUP, HEAD_DIM), keys, vals)
    return jnp.einsum('bte,ed->btd', o.reshape(b, t, N_HEADS * HEAD_DIM), w_o)


def na_project(h, w_qkv, q_gain, k_gain):
    b, t, _ = h.shape
    q, k, v = jnp.split(jnp.einsum('btd,de->bte', h, w_qkv), 3, axis=-1)
    q = rms_norm(q.reshape(b, t, N_HEADS, HEAD_DIM), q_gain)
    k = rms_norm(k.reshape(b, t, N_HEADS, HEAD_DIM), k_gain)
    v = v.reshape(b, t, N_HEADS, HEAD_DIM)
    return q, k, v


def na_context(h, w_qkv, q_gain, k_gain, w_o):
    b, t, _ = h.shape
    q, k, v = na_project(h, w_qkv, q_gain, k_gain)
    o = blocked_attention(q[:, :, :, None, :], k, v)
    o = jnp.einsum('bte,ed->btd', o.reshape(b, t, N_HEADS * HEAD_DIM), w_o)
    return o, k, v


def na_latent(h, ctx_k, ctx_v, w_qkv, q_gain, k_gain, rpb, w_o):
    b, t, _ = h.shape
    q, k, v = na_project(h, w_qkv, q_gain, k_gain)
    ctx_k = ctx_k.astype(k.dtype)
    ctx_v = ctx_v.astype(v.dtype)
    rows = t // GRID_W
    kr = min(WIN_R, rows)
    kc = WIN_C
    rb = Q_BLOCK // GRID_W
    band = min(kr + rb - 1, rows)
    n_band = band * GRID_W
    nb = t // Q_BLOCK
    scale = HEAD_DIM ** -0.5
    k_grid = k.reshape(b, rows, GRID_W, N_HEADS, HEAD_DIM)
    v_grid = v.reshape(b, rows, GRID_W, N_HEADS, HEAD_DIM)
    cols = jnp.arange(GRID_W)
    col_start = jnp.clip(cols - kc // 2, 0, GRID_W - kc)
    q_c = jnp.tile(cols, rb)
    q_cs = jnp.tile(col_start, rb)
    k_c = jnp.tile(cols, band)
    qb = jnp.moveaxis(q.reshape(b, nb, Q_BLOCK, N_HEADS, HEAD_DIM), 1, 0)

    def one_block(args):
        qi, blk = args
        q_rows = blk * rb + jnp.arange(rb)
        row_start = jnp.clip(q_rows - kr // 2, 0, rows - kr)
        band_start = jnp.minimum(row_start[0], rows - band)
        kb = lax.dynamic_slice_in_dim(k_grid, band_start, band, axis=1).reshape(b, n_band, N_HEADS, HEAD_DIM)
        vb = lax.dynamic_slice_in_dim(v_grid, band_start, band, axis=1).reshape(b, n_band, N_HEADS, HEAD_DIM)
        q_r = jnp.repeat(q_rows, GRID_W)
        q_rs = jnp.repeat(row_start, GRID_W)
        k_r = band_start + jnp.repeat(jnp.arange(band), GRID_W)
        off_r = k_r[None, :] - q_rs[:, None]
        off_c = k_c[None, :] - q_cs[:, None]
        in_win = (off_r >= 0) & (off_r < kr) & (off_c >= 0) & (off_c < kc)
        rel_r = jnp.clip(k_r[None, :] - q_r[:, None] + (WIN_R - 1), 0, 2 * WIN_R - 2)
        rel_c = jnp.clip(k_c[None, :] - q_c[:, None] + (WIN_C - 1), 0, 2 * WIN_C - 2)
        bias = rpb[:, rel_r, rel_c].astype(jnp.float32)
        s_loc = jnp.einsum('bqhd,bkhd->bhqk', qi, kb).astype(jnp.float32) * scale + bias
        s_loc = jnp.where(in_win, s_loc, NEG_INF)
        s_ctx = jnp.einsum('bqhd,bkhd->bhqk', qi, ctx_k).astype(jnp.float32) * scale
        p = jax.nn.softmax(jnp.concatenate([s_loc, s_ctx], axis=-1), axis=-1).astype(v.dtype)
        return (jnp.einsum('bhqk,bkhd->bqhd', p[..., :n_band], vb)
                + jnp.einsum('bhqk,bkhd->bqhd', p[..., n_band:], ctx_v))

    o = lax.map(one_block, (qb, jnp.arange(nb)))
    o = jnp.moveaxis(o, 0, 1).reshape(b, t, N_HEADS * HEAD_DIM)
    return jnp.einsum('bte,ed->btd', o, w_o)


def expert_choice_moe(x, w_router, w_gate, w_up, w_down):
    b, t, d = x.shape
    n = b * t
    cap = EC_CAPACITY_FACTOR * n // N_EXPERTS
    xf = x.reshape(n, d)
    aff = jax.nn.softmax(jnp.einsum('nd,de->ne', xf, w_router).astype(jnp.float32), axis=-1)
    g, idx = lax.top_k(aff.T, cap)
    xe = xf[idx]
    hid = jax.nn.silu(jnp.einsum('ecd,edf->ecf', xe, w_gate)) * jnp.einsum('ecd,edf->ecf', xe, w_up)
    ye = jnp.einsum('ecf,efd->ecd', hid, w_down) * g[..., None].astype(x.dtype)
    out = jnp.zeros_like(xf).at[idx.reshape(-1)].add(ye.reshape(-1, d))
    return out.reshape(b, t, d)


def setup_inputs(seed: int = 0) -> dict:
    key = jax.random.key(seed)
    ks = jax.random.split(key, 25)
    f32 = jnp.float32

    def nrm(k, shape, scale):
        return jax.random.normal(k, shape, f32) * scale

    qkv_a = (N_HEADS + 2 * N_KV_HEADS) * HEAD_DIM
    qkv_n = 3 * N_HEADS * HEAD_DIM
    hd_all = N_HEADS * HEAD_DIM
    return {
        'x_prompt': nrm(ks[0], (BATCH, SEQ, D_MODEL), 1.0),
        'x_sample': nrm(ks[1], (DEC_BATCH, DEC_SEQ, D_MODEL), 1.0),
        'cache_attn_k': nrm(ks[2], (DEC_BATCH, N_ATTN_LAYERS, PAST_LEN, N_KV_HEADS, HEAD_DIM), 1.0),
        'cache_attn_v': nrm(ks[3], (DEC_BATCH, N_ATTN_LAYERS, PAST_LEN, N_KV_HEADS, HEAD_DIM), 1.0),
        'cache_na_k': nrm(ks[4], (DEC_BATCH, N_NA_LAYERS, PAST_LEN, N_HEADS, HEAD_DIM), 1.0),
        'cache_na_v': nrm(ks[5], (DEC_BATCH, N_NA_LAYERS, PAST_LEN, N_HEADS, HEAD_DIM), 1.0),
        'c': nrm(ks[6], (DEC_BATCH, D_MODEL), 1.0),
        'c_ctx': nrm(ks[7], (D_MODEL,), 1.0),
        'norm1_g': 1.0 + nrm(ks[8], (DEPTH, D_MODEL), 0.02),
        'norm2_g': 1.0 + nrm(ks[9], (DEPTH, D_MODEL), 0.02),
        'w_ada': nrm(ks[10], (DEPTH, D_MODEL, 6 * D_MODEL), D_MODEL ** -0.5),
        'b_ada': nrm(ks[11], (DEPTH, 6 * D_MODEL), 0.02),
        'attn_w_qkv': nrm(ks[12], (N_ATTN_LAYERS, D_MODEL, qkv_a), D_MODEL ** -0.5),
        'attn_q_gain': 1.0 + nrm(ks[13], (N_ATTN_LAYERS, HEAD_DIM), 0.02),
        'attn_k_gain': 1.0 + nrm(ks[14], (N_ATTN_LAYERS, HEAD_DIM), 0.02),
        'attn_w_o': nrm(ks[15], (N_ATTN_LAYERS, hd_all, D_MODEL), hd_all ** -0.5),
        'na_w_qkv': nrm(ks[16], (N_NA_LAYERS, D_MODEL, qkv_n), D_MODEL ** -0.5),
        'na_q_gain': 1.0 + nrm(ks[17], (N_NA_LAYERS, HEAD_DIM), 0.02),
        'na_k_gain': 1.0 + nrm(ks[18], (N_NA_LAYERS, HEAD_DIM), 0.02),
        'na_rpb': nrm(ks[19], (N_NA_LAYERS, N_HEADS, 2 * WIN_R - 1, 2 * WIN_C - 1), 0.1),
        'na_w_o': nrm(ks[20], (N_NA_LAYERS, hd_all, D_MODEL), hd_all ** -0.5),
        'moe_w_router': nrm(ks[21], (DEPTH, D_MODEL, N_EXPERTS), D_MODEL ** -0.5),
        'moe_w_gate': nrm(ks[22], (DEPTH, N_EXPERTS, D_MODEL, EXPERT_FF), D_MODEL ** -0.5),
        'moe_w_up': nrm(ks[23], (DEPTH, N_EXPERTS, D_MODEL, EXPERT_FF), D_MODEL ** -0.5),
        'moe_w_down': nrm(ks[24], (DEPTH, N_EXPERTS, EXPERT_FF, D_MODEL), EXPERT_FF ** -0.5),
    }


def reference(x_prompt, x_sample, cache_attn_k, cache_attn_v, cache_na_k, cache_na_v, c, c_ctx,
              norm1_g, norm2_g, w_ada, b_ada,
              attn_w_qkv, attn_q_gain, attn_k_gain, attn_w_o,
              na_w_qkv, na_q_gain, na_k_gain, na_rpb, na_w_o,
              moe_w_router, moe_w_gate, moe_w_up, moe_w_down):
    y_ctx = x_prompt
    y_lat = x_sample
    cos, sin = axial_rope_tables(x_sample.shape[1])
    attn_k_list, attn_v_list, na_k_list, na_v_list = [], [], [], []
    for i in range(DEPTH):
        j = i // N_MIXERS
        mod_ctx = ada_modulation(c_ctx[None, :], w_ada[i], b_ada[i])
        mod_lat = ada_modulation(c, w_ada[i], b_ada[i])
        h_ctx = modulate(rms_norm(y_ctx, norm1_g[i]), mod_ctx[0], mod_ctx[1])
        h_lat = modulate(rms_norm(y_lat, norm1_g[i]), mod_lat[0], mod_lat[1])
        if i % N_MIXERS == 0:
            o_ctx, k_ctx, v_ctx = gqa_context(h_ctx, attn_w_qkv[j], attn_q_gain[j], attn_k_gain[j], attn_w_o[j])
            o_lat = gqa_latent(h_lat, cache_attn_k[:, j], cache_attn_v[:, j], cos, sin,
                               attn_w_qkv[j], attn_q_gain[j], attn_k_gain[j], attn_w_o[j])
            attn_k_list.append(k_ctx)
            attn_v_list.append(v_ctx)
        else:
            o_ctx, k_ctx, v_ctx = na_context(h_ctx, na_w_qkv[j], na_q_gain[j], na_k_gain[j], na_w_o[j])
            o_lat = na_latent(h_lat, cache_na_k[:, j], cache_na_v[:, j],
                              na_w_qkv[j], na_q_gain[j], na_k_gain[j], na_rpb[j], na_w_o[j])
            na_k_list.append(k_ctx)
            na_v_list.append(v_ctx)
        y_ctx = y_ctx + mod_ctx[2] * o_ctx
        y_lat = y_lat + mod_lat[2] * o_lat
        h_ctx = modulate(rms_norm(y_ctx, norm2_g[i]), mod_ctx[3], mod_ctx[4])
        h_lat = modulate(rms_norm(y_lat, norm2_g[i]), mod_lat[3], mod_lat[4])
        y_ctx = y_ctx + mod_ctx[5] * expert_choice_moe(h_ctx, moe_w_router[i], moe_w_gate[i], moe_w_up[i], moe_w_down[i])
        y_lat = y_lat + mod_lat[5] * expert_choice_moe(h_lat, moe_w_router[i], moe_w_gate[i], moe_w_up[i], moe_w_down[i])
    new_attn_k = jnp.stack(attn_k_list, axis=1)
    new_attn_v = jnp.stack(attn_v_list, axis=1)
    new_na_k = jnp.stack(na_k_list, axis=1)
    new_na_v = jnp.stack(na_v_list, axis=1)
    return (y_ctx, y_lat, new_attn_k, new_attn_v, new_na_k, new_na_v)
```

```python
import functools

import jax
import jax.numpy as jnp
import numpy as np
from jax import lax
from jax.experimental import pallas as pl
from jax.experimental.pallas import tpu as pltpu

F32 = jnp.float32
BF16 = jnp.bfloat16
I32 = jnp.int32

D_MODEL = 1024
N_HEADS = 16
N_KV_HEADS = 4
HEAD_DIM = 64
GRID_W = 64
WIN_R = 8
WIN_C = 16
N_EXPERTS = 16
EXPERT_FF = 2048
ROPE_THETA = 10000.0
RMS_EPS = 1e-6
NEG_INF = -1e30

LANES = 128
MXU_DIM = 256
VMEM_LIMIT = 56 * 1024 * 1024

N_TOK = 8192
CAP = 2 * N_TOK // N_EXPERTS
TM = 512
N_MOD_ROWS = 16
TOK_CHUNK = 512
ROW_BLK = 256
N_CHUNKS = N_TOK // TOK_CHUNK
FF_TILE = 512
NA_BAND = 10
NA_QBLK = 128


def _params(*sem):
    return pltpu.CompilerParams(dimension_semantics=sem, vmem_limit_bytes=VMEM_LIMIT)


def _dot(a, b):
    return jnp.dot(a, b, preferred_element_type=F32)


def _dot_nt(a, b):
    return lax.dot_general(a, b, (((1,), (1,)), ((), ())), preferred_element_type=F32)


def _split_bf16(x):
    hi = x.astype(BF16)
    lo = (x - hi.astype(F32)).astype(BF16)
    return hi, lo


def _silu(x):
    return x * (1.0 / (1.0 + jnp.exp(-x)))


def _ada_kernel(cond_ref, w_ref, b_ref, out_ref):
    sx = _silu(cond_ref[...])
    xh, xl = _split_bf16(sx)
    wh, wl = _split_bf16(w_ref[0])
    out_ref[0] = _dot(xh, wh) + _dot(xl, wh) + _dot(xh, wl) + b_ref[0]


def _ada(cond, w_ada, b_ada):
    depth = w_ada.shape[0]
    tn = 1024
    n_out = w_ada.shape[2]
    return pl.pallas_call(
        _ada_kernel,
        grid=(depth, n_out // tn),
        in_specs=[
            pl.BlockSpec((N_MOD_ROWS, D_MODEL), lambda l, n: (0, 0)),
            pl.BlockSpec((1, D_MODEL, tn), lambda l, n: (l, 0, n)),
            pl.BlockSpec((1, 1, tn), lambda l, n: (l, 0, n)),
        ],
        out_specs=pl.BlockSpec((1, N_MOD_ROWS, tn), lambda l, n: (l, 0, n)),
        out_shape=jax.ShapeDtypeStruct((depth, N_MOD_ROWS, n_out), F32),
        compiler_params=_params("arbitrary", "arbitrary"),
        name="ada",
    )(cond, w_ada, b_ada.reshape(depth, 1, n_out))


def _mod_spec(layer, which, tiles_per_group, base_row):
    def index(i):
        return ((layer * N_MOD_ROWS + base_row + i // tiles_per_group) * 6 + which, 0, 0)
    return pl.BlockSpec((1, 1, D_MODEL), index)


def _head_norm(z, seg, gain):
    ss = _dot((z * z).astype(BF16), seg)
    return z * lax.rsqrt(ss * (1.0 / HEAD_DIM) + RMS_EPS) * gain


def _rope(z, cos_t, sin_t):
    lane = lax.broadcasted_iota(I32, z.shape, 1)
    first = (lane & 32) == 0
    n = z.shape[1]
    partner = jnp.where(first, pltpu.roll(z, n - 32, axis=1), pltpu.roll(z, 32, axis=1))
    return z * cos_t + partner * sin_t


def _dup_heads(z):
    outs = []
    for b in range(z.shape[1] // LANES):
        x = z[:, b * LANES:(b + 1) * LANES]
        xr = pltpu.roll(x, HEAD_DIM, axis=1)
        lo = lax.broadcasted_iota(I32, x.shape, 1) < HEAD_DIM
        outs.append(jnp.where(lo, x, xr))
        outs.append(jnp.where(lo, xr, x))
    return jnp.concatenate(outs, axis=1)


def _qkv_kernel(*refs, kv_width, rope, emit_f32):
    it = iter(refs)
    x_ref, g_ref, shift_ref, scale_ref, w_ref, qg_ref, kg_ref, seg_ref = (next(it) for _ in range(8))
    cos_ref = sin_ref = None
    if rope:
        cos_ref, sin_ref = next(it), next(it)
    q_ref, kb_ref, vb_ref = next(it), next(it), next(it)
    kf_ref = vf_ref = None
    if emit_f32:
        kf_ref, vf_ref = next(it), next(it)
    dup = kv_width == N_KV_HEADS * HEAD_DIM

    x = x_ref[...]
    ms = jnp.mean(x * x, axis=-1, keepdims=True)
    h = x * lax.rsqrt(ms + RMS_EPS) * g_ref[...]
    h = h * (1.0 + scale_ref[0]) + shift_ref[0]
    hb = h.astype(BF16)
    seg = seg_ref[...]
    if rope:
        cos_t, sin_t = cos_ref[...], sin_ref[...]

    q_width = N_HEADS * HEAD_DIM
    for cidx in range(q_width // MXU_DIM):
        c0 = cidx * MXU_DIM
        z = _head_norm(_dot(hb, w_ref[:, c0:c0 + MXU_DIM]), seg, qg_ref[...])
        if rope:
            z = _rope(z, cos_t, sin_t)
        q_ref[:, c0:c0 + MXU_DIM] = (z * (HEAD_DIM ** -0.5)).astype(BF16)
    for cidx in range(kv_width // MXU_DIM):
        c0 = cidx * MXU_DIM
        z = _head_norm(_dot(hb, w_ref[:, q_width + c0:q_width + c0 + MXU_DIM]), seg, kg_ref[...])
        if emit_f32:
            kf_ref[:, c0:c0 + MXU_DIM] = z
        if rope:
            z = _rope(z, cos_t, sin_t)
        v = _dot(hb, w_ref[:, q_width + kv_width + c0:q_width + kv_width + c0 + MXU_DIM])
        if emit_f32:
            vf_ref[:, c0:c0 + MXU_DIM] = v
        if dup:
            kb_ref[:, 2 * c0:2 * c0 + 2 * MXU_DIM] = _dup_heads(z).astype(BF16)
            vb_ref[:, 2 * c0:2 * c0 + 2 * MXU_DIM] = _dup_heads(v).astype(BF16)
        else:
            kb_ref[:, c0:c0 + MXU_DIM] = z.astype(BF16)
            vb_ref[:, c0:c0 + MXU_DIM] = v.astype(BF16)


def _qkv(x, norm_g, mod3, layer, tiles_per_group, base_row, w_bf16, q_gain, k_gain, seg,
         kv_width, rope_tables, emit_f32):
    n_tok = x.shape[0]
    n_w = w_bf16.shape[1]
    kvb_width = 2 * kv_width if kv_width == N_KV_HEADS * HEAD_DIM else kv_width
    tile = lambda w: pl.BlockSpec((TM, w), lambda i: (i, 0))
    const = lambda shape: pl.BlockSpec(shape, lambda i: (0,) * len(shape))
    in_specs = [
        tile(D_MODEL),
        const((1, D_MODEL)),
        _mod_spec(layer, 0, tiles_per_group, base_row),
        _mod_spec(layer, 1, tiles_per_group, base_row),
        const((D_MODEL, n_w)),
        const((1, MXU_DIM)),
        const((1, MXU_DIM)),
        const((MXU_DIM, MXU_DIM)),
    ]
    args = [x, norm_g.reshape(1, D_MODEL), mod3, mod3, w_bf16,
            jnp.tile(q_gain, MXU_DIM // HEAD_DIM).reshape(1, MXU_DIM),
            jnp.tile(k_gain, MXU_DIM // HEAD_DIM).reshape(1, MXU_DIM), seg]
    if rope_tables is not None:
        seq_tiles = rope_tables[0].shape[0] // TM
        in_specs += [pl.BlockSpec((TM, MXU_DIM), lambda i: (i % seq_tiles, 0))] * 2
        args += list(rope_tables)
    out_specs = [tile(D_MODEL), tile(kvb_width), tile(kvb_width)]
    out_shape = [jax.ShapeDtypeStruct((n_tok, D_MODEL), BF16),
                 jax.ShapeDtypeStruct((n_tok, kvb_width), BF16),
                 jax.ShapeDtypeStruct((n_tok, kvb_width), BF16)]
    if emit_f32:
        out_specs += [tile(kv_width), tile(kv_width)]
        out_shape += [jax.ShapeDtypeStruct((n_tok, kv_width), F32)] * 2
    return pl.pallas_call(
        functools.partial(_qkv_kernel, kv_width=kv_width, rope=rope_tables is not None,
                          emit_f32=emit_f32),
        grid=(n_tok // TM,),
        in_specs=in_specs,
        out_specs=out_specs,
        out_shape=out_shape,
        compiler_params=_params("arbitrary"),
        name="qkv",
    )(*args)


def _two_head_rows(qj):
    lo = lax.broadcasted_iota(I32, qj.shape, 1) < HEAD_DIM
    zero = jnp.zeros_like(qj)
    return jnp.concatenate([jnp.where(lo, qj, zero), jnp.where(lo, zero, qj)], axis=0)


def _merge_two_heads(r):
    tq = r.shape[0] // 2
    lo = lax.broadcasted_iota(I32, (tq, LANES), 1) < HEAD_DIM
    return jnp.where(lo, r[:tq], r[tq:])


def _attn_kernel(*refs, n_kv_blocks, has_ctx):
    if has_ctx:
        q_ref, k_ref, v_ref, ck_ref, cv_ref, o_ref = refs
    else:
        q_ref, k_ref, v_ref, o_ref = refs
    n_q_blocks = D_MODEL // LANES
    for j in range(n_q_blocks):
        kb = (j * n_kv_blocks) // n_q_blocks
        ksl = slice(kb * LANES, (kb + 1) * LANES)
        q2 = _two_head_rows(q_ref[0, :, j * LANES:(j + 1) * LANES])
        s = _dot_nt(q2, k_ref[0, :, ksl])
        m = jnp.max(s, axis=-1, keepdims=True)
        if has_ctx:
            sc = _dot_nt(q2, ck_ref[0, :, ksl])
            m = jnp.maximum(m, jnp.max(sc, axis=-1, keepdims=True))
        p = jnp.exp(s - m)
        l = jnp.sum(p, axis=-1, keepdims=True)
        r = _dot(p.astype(BF16), v_ref[0, :, ksl])
        if has_ctx:
            pc = jnp.exp(sc - m)
            l = l + jnp.sum(pc, axis=-1, keepdims=True)
            r = r + _dot(pc.astype(BF16), cv_ref[0, :, ksl])
        r = r / l
        o_ref[0, :, j * LANES:(j + 1) * LANES] = _merge_two_heads(r).astype(BF16)


def _attention(q, k, v, ctx_k=None, ctx_v=None, tq=256):
    b, t, _ = q.shape
    s, w = k.shape[1], k.shape[2]
    has_ctx = ctx_k is not None
    in_specs = [
        pl.BlockSpec((1, tq, D_MODEL), lambda bi, qi: (bi, qi, 0)),
        pl.BlockSpec((1, s, w), lambda bi, qi: (bi, 0, 0)),
        pl.BlockSpec((1, s, w), lambda bi, qi: (bi, 0, 0)),
    ]
    args = [q, k, v]
    if has_ctx:
        sc = ctx_k.shape[1]
        in_specs += [pl.BlockSpec((1, sc, w), lambda bi, qi: (bi, 0, 0))] * 2
        args += [ctx_k, ctx_v]
    return pl.pallas_call(
        functools.partial(_attn_kernel, n_kv_blocks=w // LANES, has_ctx=has_ctx),
        grid=(b, t // tq),
        in_specs=in_specs,
        out_specs=pl.BlockSpec((1, tq, D_MODEL), lambda bi, qi: (bi, qi, 0)),
        out_shape=jax.ShapeDtypeStruct((b, t, D_MODEL), BF16),
        compiler_params=_params("arbitrary", "arbitrary"),
        name="attn",
    )(*args)


def _na_band_start(blk, rows):
    row_start = jnp.clip(2 * blk - WIN_R // 2, 0, rows - WIN_R)
    return jnp.minimum(row_start, rows - NA_BAND)


def _na_kernel(q_ref, k_ref, v_ref, ck_ref, cv_ref, bias_ref, o_ref, *, rows):
    blk = pl.program_id(1)
    band_start = _na_band_start(blk, rows)
    n_band = NA_BAND * GRID_W
    band = pl.ds(pl.multiple_of(band_start * GRID_W, LANES), n_band)
    shape = (NA_QBLK, n_band)
    q_r = 2 * blk + lax.broadcasted_iota(I32, shape, 0) // GRID_W
    k_r = band_start + lax.broadcasted_iota(I32, shape, 1) // GRID_W
    q_rs = jnp.clip(q_r - WIN_R // 2, 0, rows - WIN_R)
    row_ok1 = (k_r >= q_rs) & (k_r < q_rs + WIN_R)
    row_ok = jnp.concatenate([row_ok1, row_ok1], axis=0)
    pair_shift = (band_start - 2 * blk + WIN_R) // 2
    for j in range(D_MODEL // LANES):
        sl = slice(j * LANES, (j + 1) * LANES)
        q2 = _two_head_rows(q_ref[0, :, sl])
        s = _dot_nt(q2, k_ref[0, band, sl])
        bias = jnp.concatenate(
            [jnp.concatenate([bias_ref[j, hh, pair_shift + m] for m in range(NA_BAND // 2)], axis=1)
             for hh in range(2)], axis=0)
        s = jnp.where(row_ok, s + bias, NEG_INF)
        sc = _dot_nt(q2, ck_ref[0, :, sl])
        m = jnp.maximum(jnp.max(s, axis=-1, keepdims=True), jnp.max(sc, axis=-1, keepdims=True))
        p = jnp.exp(s - m)
        pc = jnp.exp(sc - m)
        l = jnp.sum(p, axis=-1, keepdims=True) + jnp.sum(pc, axis=-1, keepdims=True)
        r = _dot(p.astype(BF16), v_ref[0, band, sl]) + _dot(pc.astype(BF16), cv_ref[0, :, sl])
        r = r / l
        o_ref[0, :, sl] = _merge_two_heads(r).astype(BF16)


def _na_bias_table(rpb):
    qi = np.arange(LANES)[:, None]
    ki = np.arange(LANES)[None, :]
    q_row, q_col = qi // GRID_W, qi % GRID_W
    k_row, k_col = ki // GRID_W, ki % GRID_W
    col_start = np.clip(q_col - WIN_C // 2, 0, GRID_W - WIN_C)
    col_ok = (k_col >= col_start) & (k_col < col_start + WIN_C)
    rel_c = np.clip(k_col - q_col + WIN_C - 1, 0, 2 * WIN_C - 2)
    dd = np.arange(WIN_R + 1)[:, None, None]
    rel_r = np.clip(2 * dd - WIN_R + k_row - q_row + WIN_R - 1, 0, 2 * WIN_R - 2)
    rel_c = np.broadcast_to(rel_c, rel_r.shape)
    tiles = rpb[:, rel_r, rel_c]
    tiles = jnp.where(col_ok[None, None], tiles, NEG_INF)
    return tiles.reshape(N_HEADS // 2, 2, WIN_R + 1, LANES, LANES)


def _na_attention(q, k, v, ctx_k, ctx_v, bias_tbl):
    b, t, _ = q.shape
    sc = ctx_k.shape[1]
    rows = t // GRID_W
    full = lambda n: pl.BlockSpec((1, n, D_MODEL), lambda bi, qi: (bi, 0, 0))
    return pl.pallas_call(
        functools.partial(_na_kernel, rows=rows),
        grid=(b, t // NA_QBLK),
        in_specs=[
            pl.BlockSpec((1, NA_QBLK, D_MODEL), lambda bi, qi: (bi, qi, 0)),
            full(t), full(t), full(sc), full(sc),
            pl.BlockSpec(bias_tbl.shape, lambda bi, qi: (0, 0, 0, 0, 0)),
        ],
        out_specs=pl.BlockSpec((1, NA_QBLK, D_MODEL), lambda bi, qi: (bi, qi, 0)),
        out_shape=jax.ShapeDtypeStruct((b, t, D_MODEL), BF16),
        compiler_params=_params("arbitrary", "arbitrary"),
        name="na_attn",
    )(q, k, v, ctx_k, ctx_v, bias_tbl)


def _post_kernel(o_ref, x_ref, wo_ref, gate_ref, g_ref, shift_ref, scale_ref, wrh_ref, wrl_ref,
                 y_ref, h_ref, lg_ref):
    y = x_ref[...] + gate_ref[0] * _dot(o_ref[...], wo_ref[...])
    y_ref[...] = y
    ms = jnp.mean(y * y, axis=-1, keepdims=True)
    h = y * lax.rsqrt(ms + RMS_EPS) * g_ref[...]
    h = h * (1.0 + scale_ref[0]) + shift_ref[0]
    hh, hl = _split_bf16(h)
    h_ref[...] = hh
    lg_ref[...] = _dot_nt(wrh_ref[...], hh) + _dot_nt(wrh_ref[...], hl) + _dot_nt(wrl_ref[...], hh)


def _post_attention(o, x, wo_bf16, norm_g, mod3, layer, tiles_per_group, base_row, wr_hi, wr_lo):
    n_tok = x.shape[0]
    tile = lambda w: pl.BlockSpec((TM, w), lambda i: (i, 0))
    const = lambda shape: pl.BlockSpec(shape, lambda i: (0,) * len(shape))
    return pl.pallas_call(
        _post_kernel,
        grid=(n_tok // TM,),
        in_specs=[
            tile(D_MODEL), tile(D_MODEL), const((D_MODEL, D_MODEL)),
            _mod_spec(layer, 2, tiles_per_group, base_row),
            const((1, D_MODEL)),
            _mod_spec(layer, 3, tiles_per_group, base_row),
            _mod_spec(layer, 4, tiles_per_group, base_row),
            const((N_EXPERTS, D_MODEL)), const((N_EXPERTS, D_MODEL)),
        ],
        out_specs=[tile(D_MODEL), tile(D_MODEL), pl.BlockSpec((N_EXPERTS, TM), lambda i: (0, i))],
        out_shape=[jax.ShapeDtypeStruct((n_tok, D_MODEL), F32),
                   jax.ShapeDtypeStruct((n_tok, D_MODEL), BF16),
                   jax.ShapeDtypeStruct((N_EXPERTS, n_tok), F32)],
        compiler_params=_params("arbitrary"),
        name="post_attn",
    )(o, x, wo_bf16, mod3, norm_g.reshape(1, D_MODEL), mod3, mod3, wr_hi, wr_lo)


def _excl_prefix(mask, tri):
    outs, offs = [], []
    off = jnp.zeros((mask.shape[0], 1), F32)
    for c in range(mask.shape[1] // MXU_DIM):
        xc = mask[:, c * MXU_DIM:(c + 1) * MXU_DIM]
        offs.append(off)
        outs.append(_dot(xc.astype(BF16), tri) + off)
        off = off + jnp.sum(xc, axis=1, keepdims=True)
    offs.append(off)
    return jnp.concatenate(outs, axis=1), offs


def _plan_kernel(lg_ref, tri_ref, pos_t_ref, posm_ref, aff_ref, cum_ref):
    lg = lg_ref[0]
    ex = jnp.exp(lg - jnp.max(lg, axis=0, keepdims=True))
    aff = ex / jnp.sum(ex, axis=0, keepdims=True)
    bits = lax.bitcast_convert_type(aff, I32)

    def search(i, thr):
        cand = thr | jnp.left_shift(jnp.int32(1), 30 - i)
        cnt = jnp.sum(jnp.where(bits >= cand, 1.0, 0.0), axis=1, keepdims=True)
        return jnp.where(cnt >= CAP, cand, thr)

    thr = lax.fori_loop(0, 31, search, jnp.zeros((N_EXPERTS, 1), I32))
    tri = tri_ref[...]
    gt = bits > thr
    eq = jnp.where(bits == thr, 1.0, 0.0)
    need = CAP - jnp.sum(jnp.where(gt, 1.0, 0.0), axis=1, keepdims=True)
    eq_rank, _ = _excl_prefix(eq, tri)
    sel = jnp.where(gt | ((eq > 0.0) & (eq_rank < need)), 1.0, 0.0)
    pos, offs = _excl_prefix(sel, tri)
    posm = jnp.where(sel > 0.0, pos, -1.0)

    for c in range(N_CHUNKS):
        sl = slice(c * TOK_CHUNK, (c + 1) * TOK_CHUNK)
        posm_ref[0, c] = posm[:, sl].astype(I32)
        aff_ref[0, c] = aff[:, sl]
    lane = lax.broadcasted_iota(I32, (N_EXPERTS, LANES), 1)
    per_chunk = TOK_CHUNK // MXU_DIM
    cum = jnp.zeros((N_EXPERTS, LANES), F32)
    for c in range(N_CHUNKS + 1):
        cum = jnp.where(lane == c, offs[c * per_chunk], cum)
    cum_ref[0] = cum.astype(I32)
    padded = jnp.concatenate([posm, jnp.full((LANES - N_EXPERTS, N_TOK), -1.0, F32)], axis=0)
    for t in range(N_TOK // LANES):
        pos_t_ref[0, t * LANES:(t + 1) * LANES, :] = padded[:, t * LANES:(t + 1) * LANES].T


def _plan(logits_t, tri):
    ns = logits_t.shape[0]
    chunked = lambda dt: jax.ShapeDtypeStruct((ns, N_CHUNKS, N_EXPERTS, TOK_CHUNK), dt)
    return pl.pallas_call(
        _plan_kernel,
        grid=(ns,),
        in_specs=[pl.BlockSpec((1, N_EXPERTS, N_TOK), lambda s: (s, 0, 0)),
                  pl.BlockSpec((MXU_DIM, MXU_DIM), lambda s: (0, 0))],
        out_specs=[pl.BlockSpec((1, N_TOK, LANES), lambda s: (s, 0, 0)),
                   pl.BlockSpec((1, N_CHUNKS, N_EXPERTS, TOK_CHUNK), lambda s: (s, 0, 0, 0)),
                   pl.BlockSpec((1, N_CHUNKS, N_EXPERTS, TOK_CHUNK), lambda s: (s, 0, 0, 0)),
                   pl.BlockSpec((1, N_EXPERTS, LANES), lambda s: (s, 0, 0))],
        out_shape=[jax.ShapeDtypeStruct((ns, N_TOK, LANES), F32), chunked(I32), chunked(F32),
                   jax.ShapeDtypeStruct((ns, N_EXPERTS, LANES), I32)],
        compiler_params=_params("arbitrary"),
        name="plan",
    )(logits_t, tri)


def _gather_kernel(cum_ref, h_ref, posm_ref, aff_ref, xe_ref, gc_ref, acc_ref, gacc_ref):
    s, e = pl.program_id(0), pl.program_id(1)
    base = (s * N_EXPERTS + e) * LANES
    for rb in range(CAP // ROW_BLK):
        r0 = rb * ROW_BLK
        c_lo = jnp.int32(0)
        c_hi = jnp.int32(0)
        for c in range(N_CHUNKS):
            c_lo = c_lo + (cum_ref[base + c + 1] <= r0).astype(I32)
            c_hi = c_hi + (cum_ref[base + c] < r0 + ROW_BLK).astype(I32)
        acc_ref[...] = jnp.zeros_like(acc_ref)
        gacc_ref[...] = jnp.zeros_like(gacc_ref)
        row = lax.broadcasted_iota(I32, (ROW_BLK, TOK_CHUNK), 0) + r0

        def chunk(c, carry):
            hit = row == posm_ref[0, c, pl.ds(e, 1), :]
            onehot = jnp.where(hit, 1.0, 0.0).astype(BF16)
            tok = pl.ds(pl.multiple_of(c * TOK_CHUNK, TOK_CHUNK), TOK_CHUNK)
            acc_ref[...] += _dot(onehot, h_ref[0, tok, :])
            gacc_ref[...] += jnp.sum(jnp.where(hit, aff_ref[0, c, pl.ds(e, 1), :], 0.0),
                                     axis=1, keepdims=True)
            return carry

        lax.fori_loop(c_lo, c_hi, chunk, 0)
        xe_ref[0, 0, r0:r0 + ROW_BLK, :] = acc_ref[...].astype(BF16)
        gc_ref[0, 0, r0:r0 + ROW_BLK, :] = jnp.broadcast_to(gacc_ref[...], (ROW_BLK, LANES))


def _gather(cum_flat, h2, posm, aff):
    ns = h2.shape[0]
    plan_spec = pl.BlockSpec((1, N_CHUNKS, N_EXPERTS, TOK_CHUNK), lambda s, e, cum: (s, 0, 0, 0))
    return pl.pallas_call(
        _gather_kernel,
        grid_spec=pltpu.PrefetchScalarGridSpec(
            num_scalar_prefetch=1,
            grid=(ns, N_EXPERTS),
            in_specs=[pl.BlockSpec((1, N_TOK, D_MODEL), lambda s, e, cum: (s, 0, 0)),
                      plan_spec, plan_spec],
            out_specs=[pl.BlockSpec((1, 1, CAP, D_MODEL), lambda s, e, cum: (s, e, 0, 0)),
                       pl.BlockSpec((1, 1, CAP, LANES), lambda s, e, cum: (s, e, 0, 0))],
            scratch_shapes=[pltpu.VMEM((ROW_BLK, D_MODEL), F32), pltpu.VMEM((ROW_BLK, 1), F32)],
        ),
        out_shape=[jax.ShapeDtypeStruct((ns, N_EXPERTS, CAP, D_MODEL), BF16),
                   jax.ShapeDtypeStruct((ns, N_EXPERTS, CAP, LANES), F32)],
        compiler_params=_params("arbitrary", "arbitrary"),
        name="moe_gather",
    )(cum_flat, h2, posm, aff)


def _ffn_kernel(x_ref, wg_ref, wu_ref, wd_ref, gc_ref, ye_ref, acc_ref, *, row_tile):
    f = pl.program_id(1)
    ns = x_ref.shape[0]

    @pl.when(f == 0)
    def _():
        acc_ref[...] = jnp.zeros_like(acc_ref)

    wg = wg_ref[0].astype(BF16)
    wu = wu_ref[0].astype(BF16)
    wd = wd_ref[0].astype(BF16)
    for s in range(ns):
        for r0 in range(0, CAP, row_tile):
            x = x_ref[s, 0, r0:r0 + row_tile, :]
            hid = _silu(_dot(x, wg)) * _dot(x, wu)
            acc_ref[s, r0:r0 + row_tile, :] += _dot(hid.astype(BF16), wd)

    @pl.when(f == pl.num_programs(1) - 1)
    def _():
        for s in range(ns):
            ye_ref[s, 0] = (acc_ref[s] * gc_ref[s, 0, :, 0:1]).astype(BF16)


def _ffn(xe, gc, w_gate, w_up, w_down):
    ns = xe.shape[0]
    return pl.pallas_call(
        functools.partial(_ffn_kernel, row_tile=512),
        grid=(N_EXPERTS, EXPERT_FF // FF_TILE),
        in_specs=[
            pl.BlockSpec((ns, 1, CAP, D_MODEL), lambda e, f: (0, e, 0, 0)),
            pl.BlockSpec((1, D_MODEL, FF_TILE), lambda e, f: (e, 0, f)),
            pl.BlockSpec((1, D_MODEL, FF_TILE), lambda e, f: (e, 0, f)),
            pl.BlockSpec((1, FF_TILE, D_MODEL), lambda e, f: (e, f, 0)),
            pl.BlockSpec((ns, 1, CAP, LANES), lambda e, f: (0, e, 0, 0)),
        ],
        out_specs=pl.BlockSpec((ns, 1, CAP, D_MODEL), lambda e, f: (0, e, 0, 0)),
        out_shape=jax.ShapeDtypeStruct((ns, N_EXPERTS, CAP, D_MODEL), BF16),
        scratch_shapes=[pltpu.VMEM((ns, CAP, D_MODEL), F32)],
        compiler_params=_params("arbitrary", "arbitrary"),
        name="moe_ffn",
    )(xe, w_gate, w_up, w_down, gc)


def _scatter_kernel(cum_ref, ye_ref, pos_t_ref, y_ref, gate_ref, out_ref, acc_ref):
    s, c = pl.program_id(0), pl.program_id(1)
    acc_ref[...] = jnp.zeros_like(acc_ref)
    lane = lax.broadcasted_iota(I32, (TOK_CHUNK, ROW_BLK), 1).astype(F32)
    for e in range(N_EXPERTS):
        base = (s * N_EXPERTS + e) * LANES
        lo = cum_ref[base + c]
        hi = cum_ref[base + c + 1]
        rb_lo = lo // ROW_BLK
        rb_hi = jnp.where(hi > lo, (hi - 1) // ROW_BLK + 1, rb_lo)
        col = pos_t_ref[0, :, e:e + 1]

        def block(rb, carry):
            onehot = jnp.where(col == lane + (rb * ROW_BLK).astype(F32), 1.0, 0.0).astype(BF16)
            rows = pl.ds(pl.multiple_of(rb * ROW_BLK, ROW_BLK), ROW_BLK)
            acc_ref[...] += _dot(onehot, ye_ref[0, e, rows, :])
            return carry

        lax.fori_loop(rb_lo, rb_hi, block, 0)
    out_ref[...] = y_ref[...] + gate_ref[0] * acc_ref[...]


def _scatter(cum_flat, ye, pos_t, y, mod3, layer, stream_rows):
    ns = ye.shape[0]

    def gate_index(s, c, cum):
        row = 0
        for si, (tpg, base_row) in enumerate(stream_rows):
            row = jnp.where(s == si, base_row + c // tpg, row)
        return ((layer * N_MOD_ROWS + row) * 6 + 5, 0, 0)

    return pl.pallas_call(
        _scatter_kernel,
        grid_spec=pltpu.PrefetchScalarGridSpec(
            num_scalar_prefetch=1,
            grid=(ns, N_CHUNKS),
            in_specs=[
                pl.BlockSpec((1, N_EXPERTS, CAP, D_MODEL), lambda s, c, cum: (s, 0, 0, 0),
                             pipeline_mode=pl.Buffered(1)),
                pl.BlockSpec((1, TOK_CHUNK, LANES), lambda s, c, cum: (s, c, 0)),
                pl.BlockSpec((TOK_CHUNK, D_MODEL), lambda s, c, cum: (s * N_CHUNKS + c, 0)),
                pl.BlockSpec((1, 1, D_MODEL), gate_index),
            ],
            out_specs=pl.BlockSpec((TOK_CHUNK, D_MODEL), lambda s, c, cum: (s * N_CHUNKS + c, 0)),
            scratch_shapes=[pltpu.VMEM((TOK_CHUNK, D_MODEL), F32)],
        ),
        out_shape=jax.ShapeDtypeStruct(y.shape, F32),
        compiler_params=_params("arbitrary", "arbitrary"),
        name="moe_scatter",
    )(cum_flat, ye, pos_t, y, mod3)


def _rope_tables(n_tokens):
    t = np.arange(n_tokens)
    row = (t // GRID_W).astype(np.float32)
    col = (t % GRID_W).astype(np.float32)
    pairs = HEAD_DIM // 4
    inv_freq = ROPE_THETA ** (-jnp.arange(pairs, dtype=F32) / pairs)
    ang = jnp.concatenate([row[:, None] * inv_freq, col[:, None] * inv_freq], axis=-1)
    cos, sin = jnp.cos(ang), jnp.sin(ang)
    reps = MXU_DIM // HEAD_DIM
    return (jnp.tile(jnp.concatenate([cos, cos], axis=-1), (1, reps)),
            jnp.tile(jnp.concatenate([-sin, sin], axis=-1), (1, reps)))


def _dup_cache(cache):
    b, s, hk, hd = cache.shape
    return jnp.broadcast_to(cache[:, :, :, None, :], (b, s, hk, 2, hd)).reshape(b, s, 2 * hk * hd).astype(BF16)


def kernel(x_prompt, x_sample, cache_attn_k, cache_attn_v, cache_na_k, cache_na_v, c, c_ctx,
           norm1_g, norm2_g, w_ada, b_ada, attn_w_qkv, attn_q_gain, attn_k_gain, attn_w_o,
           na_w_qkv, na_q_gain, na_k_gain, na_rpb, na_w_o,
           moe_w_router, moe_w_gate, moe_w_up, moe_w_down):
    bc, tc, _ = x_prompt.shape
    bl, tl, _ = x_sample.shape
    depth = w_ada.shape[0]
    assert bc * tc == N_TOK and bl * tl == N_TOK and 1 + bl <= N_MOD_ROWS

    cond = jnp.zeros((N_MOD_ROWS, D_MODEL), F32).at[0].set(c_ctx).at[1:1 + bl].set(c)
    mod3 = _ada(cond, w_ada, b_ada).reshape(depth * N_MOD_ROWS * 6, 1, D_MODEL)
    ctx_rows = (N_TOK // TM, 0)
    lat_rows = (tl // TM, 1)

    seg = jnp.asarray(np.kron(np.eye(MXU_DIM // HEAD_DIM), np.ones((HEAD_DIM, HEAD_DIM))), BF16)
    tri = jnp.asarray(np.triu(np.ones((MXU_DIM, MXU_DIM)), k=1), BF16)
    rope_tables = _rope_tables(tl)

    y_ctx = x_prompt.reshape(N_TOK, D_MODEL)
    y_lat = x_sample.reshape(N_TOK, D_MODEL)
    new_k, new_v = [], []
    for i in range(depth):
        j = i // 2
        if i % 2 == 0:
            w_qkv, q_gain, k_gain, w_o = attn_w_qkv[j], attn_q_gain[j], attn_k_gain[j], attn_w_o[j]
            kv_width = N_KV_HEADS * HEAD_DIM
        else:
            w_qkv, q_gain, k_gain, w_o = na_w_qkv[j], na_q_gain[j], na_k_gain[j], na_w_o[j]
            kv_width = N_HEADS * HEAD_DIM
        w_qkv_b = w_qkv.astype(BF16)
        w_o_b = w_o.astype(BF16)
        wr_hi, wr_lo = _split_bf16(moe_w_router[i].T)

        qc, kcb, vcb, kcf, vcf = _qkv(y_ctx, norm1_g[i], mod3, i, *ctx_rows, w_qkv_b, q_gain, k_gain,
                                      seg, kv_width, None, True)
        ql, klb, vlb = _qkv(y_lat, norm1_g[i], mod3, i, *lat_rows, w_qkv_b, q_gain, k_gain,
                            seg, kv_width, rope_tables if i % 2 == 0 else None, False)
        kvb = kcb.shape[1]
        o_ctx = _attention(qc.reshape(bc, tc, D_MODEL), kcb.reshape(bc, tc, kvb), vcb.reshape(bc, tc, kvb))
        if i % 2 == 0:
            o_lat = _attention(ql.reshape(bl, tl, D_MODEL), klb.reshape(bl, tl, kvb),
                               vlb.reshape(bl, tl, kvb),
                               _dup_cache(cache_attn_k[:, j]), _dup_cache(cache_attn_v[:, j]))
            new_k.append(kcf.reshape(bc, 1, tc, N_KV_HEADS, HEAD_DIM))
            new_v.append(vcf.reshape(bc, 1, tc, N_KV_HEADS, HEAD_DIM))
        else:
            past = cache_na_k.shape[2]
            o_lat = _na_attention(ql.reshape(bl, tl, D_MODEL), klb.reshape(bl, tl, kvb),
                                  vlb.reshape(bl, tl, kvb),
                                  cache_na_k[:, j].reshape(bl, past, D_MODEL).astype(BF16),
                                  cache_na_v[:, j].reshape(bl, past, D_MODEL).astype(BF16),
                                  _na_bias_table(na_rpb[j]))
            new_k.append(kcf.reshape(bc, 1, tc, N_HEADS, HEAD_DIM))
            new_v.append(vcf.reshape(bc, 1, tc, N_HEADS, HEAD_DIM))

        y_ctx, h_ctx, lg_ctx = _post_attention(o_ctx.reshape(N_TOK, D_MODEL), y_ctx, w_o_b, norm2_g[i],
                                               mod3, i, *ctx_rows, wr_hi, wr_lo)
        y_lat, h_lat, lg_lat = _post_attention(o_lat.reshape(N_TOK, D_MODEL), y_lat, w_o_b, norm2_g[i],
                                               mod3, i, *lat_rows, wr_hi, wr_lo)

        pos_t, posm, aff, cum = _plan(jnp.stack([lg_ctx, lg_lat]), tri)
        cum_flat = cum.reshape(-1)
        xe, gc = _gather(cum_flat, jnp.stack([h_ctx, h_lat]), posm, aff)
        ye = _ffn(xe, gc, moe_w_gate[i], moe_w_up[i], moe_w_down[i])
        scatter_rows = ((N_CHUNKS, 0), (tl // TOK_CHUNK, 1))
        y_all = _scatter(cum_flat, ye, pos_t, jnp.concatenate([y_ctx, y_lat], axis=0), mod3, i,
                         scatter_rows)
        y_ctx, y_lat = y_all[:N_TOK], y_all[N_TOK:]

    return (y_ctx.reshape(bc, tc, D_MODEL), y_lat.reshape(bl, tl, D_MODEL),
            jnp.concatenate(new_k[0::2], axis=1), jnp.concatenate(new_v[0::2], axis=1),
            jnp.concatenate(new_k[1::2], axis=1), jnp.concatenate(new_v[1::2], axis=1))
```

```python
import functools

import jax
import jax.numpy as jnp
import numpy as np
from jax import lax
from jax.experimental import pallas as pl
from jax.experimental.pallas import tpu as pltpu

F32 = jnp.float32
BF16 = jnp.bfloat16
I32 = jnp.int32

D_MODEL = 1024
N_HEADS = 16
N_KV_HEADS = 4
HEAD_DIM = 64
GRID_W = 64
WIN_R = 8
WIN_C = 16
N_EXPERTS = 16
EXPERT_FF = 2048
ROPE_THETA = 10000.0
RMS_EPS = 1e-6
NEG_INF = -1e30

LANES = 128
MXU_DIM = 256
VMEM_LIMIT = 56 * 1024 * 1024

N_STREAMS = 2
N_TOK = 8192
CAP = 2 * N_TOK // N_EXPERTS
TM = 512
N_MOD_ROWS = 16
TOK_CHUNK = 512
ROW_BLK = 256
N_CHUNKS = N_TOK // TOK_CHUNK
FF_TILE = 512
NA_BAND = 10
NA_QBLK = 128
N_REL_R = 2 * WIN_R - 1
N_REL_C = 2 * WIN_C - 1


def _params(*sem):
    return pltpu.CompilerParams(dimension_semantics=sem, vmem_limit_bytes=VMEM_LIMIT)


def _dot(a, b):
    return jnp.dot(a, b, preferred_element_type=F32)


def _dot_nt(a, b):
    return lax.dot_general(a, b, (((1,), (1,)), ((), ())), preferred_element_type=F32)


def _split_bf16(x):
    hi = x.astype(BF16)
    lo = (x - hi.astype(F32)).astype(BF16)
    return hi, lo


def _silu(x):
    return x * (1.0 / (1.0 + jnp.exp(-x)))


def _ada_kernel(cond_ref, w_ref, b_ref, out_ref):
    sx = _silu(cond_ref[...])
    xh, xl = _split_bf16(sx)
    wh, wl = _split_bf16(w_ref[0])
    out_ref[0] = _dot(xh, wh) + _dot(xl, wh) + _dot(xh, wl) + b_ref[0]


def _ada(cond, w_ada, b_ada):
    depth = w_ada.shape[0]
    tn = 1024
    n_out = w_ada.shape[2]
    return pl.pallas_call(
        _ada_kernel,
        grid=(depth, n_out // tn),
        in_specs=[
            pl.BlockSpec((N_MOD_ROWS, D_MODEL), lambda l, n: (0, 0)),
            pl.BlockSpec((1, D_MODEL, tn), lambda l, n: (l, 0, n)),
            pl.BlockSpec((1, 1, tn), lambda l, n: (l, 0, n)),
        ],
        out_specs=pl.BlockSpec((1, N_MOD_ROWS, tn), lambda l, n: (l, 0, n)),
        out_shape=jax.ShapeDtypeStruct((depth, N_MOD_ROWS, n_out), F32),
        compiler_params=_params("arbitrary", "arbitrary"),
        name="ada",
    )(cond, w_ada, b_ada.reshape(depth, 1, n_out))


def _mod_row(tile, tile_rows, lat_seq):
    ctx_tiles = N_TOK // tile_rows
    return jnp.where(tile < ctx_tiles, 0, 1 + (tile - ctx_tiles) // (lat_seq // tile_rows))


def _mod_spec(layer, which, lat_seq):
    def index(i):
        return ((layer * N_MOD_ROWS + _mod_row(i, TM, lat_seq)) * 6 + which, 0, 0)
    return pl.BlockSpec((1, 1, D_MODEL), index)


def _head_norm(z, seg, gain):
    ss = _dot((z * z).astype(BF16), seg)
    return z * lax.rsqrt(ss * (1.0 / HEAD_DIM) + RMS_EPS) * gain


def _rope(z, cos_t, sin_t):
    lane = lax.broadcasted_iota(I32, z.shape, 1)
    first = (lane & 32) == 0
    n = z.shape[1]
    partner = jnp.where(first, pltpu.roll(z, n - 32, axis=1), pltpu.roll(z, 32, axis=1))
    return z * cos_t + partner * sin_t


def _dup_heads(z):
    outs = []
    for b in range(z.shape[1] // LANES):
        x = z[:, b * LANES:(b + 1) * LANES]
        xr = pltpu.roll(x, HEAD_DIM, axis=1)
        lo = lax.broadcasted_iota(I32, x.shape, 1) < HEAD_DIM
        outs.append(jnp.where(lo, x, xr))
        outs.append(jnp.where(lo, xr, x))
    return jnp.concatenate(outs, axis=1)


def _qkv_kernel(*refs, kv_width, rope):
    it = iter(refs)
    x_ref, g_ref, shift_ref, scale_ref, w_ref, qg_ref, kg_ref, seg_ref = (next(it) for _ in range(8))
    cos_ref = sin_ref = None
    if rope:
        cos_ref, sin_ref = next(it), next(it)
    q_ref, kb_ref, vb_ref, kf_ref, vf_ref = (next(it) for _ in range(5))
    dup = kv_width == N_KV_HEADS * HEAD_DIM

    x = x_ref[...]
    ms = jnp.mean(x * x, axis=-1, keepdims=True)
    h = x * lax.rsqrt(ms + RMS_EPS) * g_ref[...]
    h = h * (1.0 + scale_ref[0]) + shift_ref[0]
    hb = h.astype(BF16)
    seg = seg_ref[...]
    if rope:
        cos_t, sin_t = cos_ref[...], sin_ref[...]

    q_width = N_HEADS * HEAD_DIM
    for cidx in range(q_width // MXU_DIM):
        c0 = cidx * MXU_DIM
        z = _head_norm(_dot(hb, w_ref[:, c0:c0 + MXU_DIM]), seg, qg_ref[...])
        if rope:
            z = _rope(z, cos_t, sin_t)
        q_ref[:, c0:c0 + MXU_DIM] = (z * (HEAD_DIM ** -0.5)).astype(BF16)
    for cidx in range(kv_width // MXU_DIM):
        c0 = cidx * MXU_DIM
        z = _head_norm(_dot(hb, w_ref[:, q_width + c0:q_width + c0 + MXU_DIM]), seg, kg_ref[...])
        kf_ref[:, c0:c0 + MXU_DIM] = z
        if rope:
            z = _rope(z, cos_t, sin_t)
        v = _dot(hb, w_ref[:, q_width + kv_width + c0:q_width + kv_width + c0 + MXU_DIM])
        vf_ref[:, c0:c0 + MXU_DIM] = v
        if dup:
            kb_ref[:, 2 * c0:2 * c0 + 2 * MXU_DIM] = _dup_heads(z).astype(BF16)
            vb_ref[:, 2 * c0:2 * c0 + 2 * MXU_DIM] = _dup_heads(v).astype(BF16)
        else:
            kb_ref[:, c0:c0 + MXU_DIM] = z.astype(BF16)
            vb_ref[:, c0:c0 + MXU_DIM] = v.astype(BF16)


def _qkv(x, norm_g, mod3, layer, lat_seq, w_bf16, q_gain, k_gain, seg, kv_width, rope_tables):
    n_tok = x.shape[0]
    ctx_tiles = N_TOK // TM
    n_w = w_bf16.shape[1]
    kvb_width = 2 * kv_width if kv_width == N_KV_HEADS * HEAD_DIM else kv_width
    tile = lambda w: pl.BlockSpec((TM, w), lambda i: (i, 0))
    ctx_tile = lambda w: pl.BlockSpec((TM, w), lambda i: (jnp.minimum(i, ctx_tiles), 0))
    const = lambda shape: pl.BlockSpec(shape, lambda i: (0,) * len(shape))
    in_specs = [
        tile(D_MODEL),
        const((1, D_MODEL)),
        _mod_spec(layer, 0, lat_seq),
        _mod_spec(layer, 1, lat_seq),
        const((D_MODEL, n_w)),
        const((1, MXU_DIM)),
        const((1, MXU_DIM)),
        const((MXU_DIM, MXU_DIM)),
    ]
    args = [x, norm_g.reshape(1, D_MODEL), mod3, mod3, w_bf16,
            jnp.tile(q_gain, MXU_DIM // HEAD_DIM).reshape(1, MXU_DIM),
            jnp.tile(k_gain, MXU_DIM // HEAD_DIM).reshape(1, MXU_DIM), seg]
    if rope_tables is not None:
        seq_tiles = lat_seq // TM
        rope_index = lambda i: (jnp.where(i < ctx_tiles, seq_tiles, i % seq_tiles), 0)
        in_specs += [pl.BlockSpec((TM, MXU_DIM), rope_index)] * 2
        args += list(rope_tables)
    out_specs = [tile(D_MODEL), tile(kvb_width), tile(kvb_width), ctx_tile(kv_width), ctx_tile(kv_width)]
    out_shape = [jax.ShapeDtypeStruct((n_tok, D_MODEL), BF16),
                 jax.ShapeDtypeStruct((n_tok, kvb_width), BF16),
                 jax.ShapeDtypeStruct((n_tok, kvb_width), BF16),
                 jax.ShapeDtypeStruct((N_TOK + TM, kv_width), F32),
                 jax.ShapeDtypeStruct((N_TOK + TM, kv_width), F32)]
    return pl.pallas_call(
        functools.partial(_qkv_kernel, kv_width=kv_width, rope=rope_tables is not None),
        grid=(n_tok // TM,),
        in_specs=in_specs,
        out_specs=out_specs,
        out_shape=out_shape,
        compiler_params=_params("arbitrary"),
        name="qkv",
    )(*args)


def _two_head_rows(qj):
    lo = lax.broadcasted_iota(I32, qj.shape, 1) < HEAD_DIM
    zero = jnp.zeros_like(qj)
    return jnp.concatenate([jnp.where(lo, qj, zero), jnp.where(lo, zero, qj)], axis=0)


def _merge_two_heads(r):
    tq = r.shape[0] // 2
    lo = lax.broadcasted_iota(I32, (tq, LANES), 1) < HEAD_DIM
    return jnp.where(lo, r[:tq], r[tq:])


def _attn_kernel(*refs, n_kv_blocks, has_ctx):
    if has_ctx:
        q_ref, k_ref, v_ref, ck_ref, cv_ref, o_ref = refs
    else:
        q_ref, k_ref, v_ref, o_ref = refs
    n_q_blocks = D_MODEL // LANES
    for j in range(n_q_blocks):
        kb = (j * n_kv_blocks) // n_q_blocks
        ksl = slice(kb * LANES, (kb + 1) * LANES)
        q2 = _two_head_rows(q_ref[0, :, j * LANES:(j + 1) * LANES])
        s = _dot_nt(q2, k_ref[0, :, ksl])
        m = jnp.max(s, axis=-1, keepdims=True)
        if has_ctx:
            sc = _dot_nt(q2, ck_ref[0, :, ksl])
            m = jnp.maximum(m, jnp.max(sc, axis=-1, keepdims=True))
        p = jnp.exp(s - m)
        l = jnp.sum(p, axis=-1, keepdims=True)
        r = _dot(p.astype(BF16), v_ref[0, :, ksl])
        if has_ctx:
            pc = jnp.exp(sc - m)
            l = l + jnp.sum(pc, axis=-1, keepdims=True)
            r = r + _dot(pc.astype(BF16), cv_ref[0, :, ksl])
        r = r / l
        o_ref[0, :, j * LANES:(j + 1) * LANES] = _merge_two_heads(r).astype(BF16)


def _attention(q, k, v, n_batch, batch0, ctx_k=None, ctx_v=None, tq=256):
    t = q.shape[1]
    s, w = k.shape[1], k.shape[2]
    has_ctx = ctx_k is not None
    in_specs = [
        pl.BlockSpec((1, tq, D_MODEL), lambda bi, qi: (bi + batch0, qi, 0)),
        pl.BlockSpec((1, s, w), lambda bi, qi: (bi + batch0, 0, 0)),
        pl.BlockSpec((1, s, w), lambda bi, qi: (bi + batch0, 0, 0)),
    ]
    args = [q, k, v]
    if has_ctx:
        sc = ctx_k.shape[1]
        in_specs += [pl.BlockSpec((1, sc, w), lambda bi, qi: (bi, 0, 0))] * 2
        args += [ctx_k, ctx_v]
    return pl.pallas_call(
        functools.partial(_attn_kernel, n_kv_blocks=w // LANES, has_ctx=has_ctx),
        grid=(n_batch, t // tq),
        in_specs=in_specs,
        out_specs=pl.BlockSpec((1, tq, D_MODEL), lambda bi, qi: (bi, qi, 0)),
        out_shape=jax.ShapeDtypeStruct((n_batch, t, D_MODEL), BF16),
        compiler_params=_params("arbitrary", "arbitrary"),
        name="attn",
    )(*args)


def _na_band_start(blk, rows):
    row_start = jnp.clip(2 * blk - WIN_R // 2, 0, rows - WIN_R)
    return jnp.minimum(row_start, rows - NA_BAND)


def _na_kernel(q_ref, k_ref, v_ref, ck_ref, cv_ref, bias_ref, o_ref, *, rows):
    blk = pl.program_id(1)
    band_start = _na_band_start(blk, rows)
    n_band = NA_BAND * GRID_W
    band = pl.ds(pl.multiple_of(band_start * GRID_W, LANES), n_band)
    shape = (NA_QBLK, n_band)
    q_r = 2 * blk + lax.broadcasted_iota(I32, shape, 0) // GRID_W
    k_r = band_start + lax.broadcasted_iota(I32, shape, 1) // GRID_W
    q_rs = jnp.clip(q_r - WIN_R // 2, 0, rows - WIN_R)
    row_ok1 = (k_r >= q_rs) & (k_r < q_rs + WIN_R)
    row_ok = jnp.concatenate([row_ok1, row_ok1], axis=0)
    pair_shift = (band_start - 2 * blk + WIN_R) // 2
    for j in range(D_MODEL // LANES):
        sl = slice(j * LANES, (j + 1) * LANES)
        q2 = _two_head_rows(q_ref[0, :, sl])
        s = _dot_nt(q2, k_ref[0, band, sl])
        bias = jnp.concatenate(
            [jnp.concatenate([bias_ref[2 * j + hh, pair_shift + m] for m in range(NA_BAND // 2)], axis=1)
             for hh in range(2)], axis=0)
        s = jnp.where(row_ok, s + bias, NEG_INF)
        sc = _dot_nt(q2, ck_ref[0, :, sl])
        m = jnp.maximum(jnp.max(s, axis=-1, keepdims=True), jnp.max(sc, axis=-1, keepdims=True))
        p = jnp.exp(s - m)
        pc = jnp.exp(sc - m)
        l = jnp.sum(p, axis=-1, keepdims=True) + jnp.sum(pc, axis=-1, keepdims=True)
        r = _dot(p.astype(BF16), v_ref[0, band, sl]) + _dot(pc.astype(BF16), cv_ref[0, :, sl])
        r = r / l
        o_ref[0, :, sl] = _merge_two_heads(r).astype(BF16)


def _na_bias_kernel(rpb_ref, out_ref, tz_ref):
    h = pl.program_id(0)
    shape = (GRID_W, LANES)
    q_col = lax.broadcasted_iota(I32, shape, 0)
    lane = lax.broadcasted_iota(I32, shape, 1)
    k_col = lane & (GRID_W - 1)
    col_start = jnp.clip(q_col - WIN_C // 2, 0, GRID_W - WIN_C)
    col_ok = (k_col >= col_start) & (k_col < col_start + WIN_C)
    rel_c = jnp.clip(k_col - q_col + WIN_C - 1, 0, N_REL_C - 1)
    for a in range(N_REL_R):
        acc = jnp.zeros(shape, F32)
        for b in range(N_REL_C):
            acc = jnp.where(rel_c == b, rpb_ref[(h * N_REL_R + a) * N_REL_C + b], acc)
        tz_ref[a] = jnp.where(col_ok, acc, NEG_INF)
    left = lane < GRID_W
    for dd in range(WIN_R + 1):
        for qr in range(2):
            rel = [min(max(2 * dd - WIN_R + kr - qr + WIN_R - 1, 0), N_REL_R - 1) for kr in range(2)]
            out_ref[0, dd, qr * GRID_W:(qr + 1) * GRID_W, :] = jnp.where(left, tz_ref[rel[0]], tz_ref[rel[1]])


def _na_bias_table(rpb):
    return pl.pallas_call(
        _na_bias_kernel,
        grid=(N_HEADS,),
        in_specs=[pl.BlockSpec(memory_space=pltpu.SMEM)],
        out_specs=pl.BlockSpec((1, WIN_R + 1, LANES, LANES), lambda h: (h, 0, 0, 0)),
        out_shape=jax.ShapeDtypeStruct((N_HEADS, WIN_R + 1, LANES, LANES), F32),
        scratch_shapes=[pltpu.VMEM((N_REL_R, GRID_W, LANES), F32)],
        compiler_params=_params("arbitrary"),
        name="na_bias",
    )(rpb.reshape(-1))


def _na_attention(q, k, v, n_batch, batch0, ctx_k, ctx_v, bias_tbl):
    t = q.shape[1]
    sc = ctx_k.shape[1]
    rows = t // GRID_W
    full = lambda n, off: pl.BlockSpec((1, n, D_MODEL), lambda bi, qi: (bi + off, 0, 0))
    return pl.pallas_call(
        functools.partial(_na_kernel, rows=rows),
        grid=(n_batch, t // NA_QBLK),
        in_specs=[
            pl.BlockSpec((1, NA_QBLK, D_MODEL), lambda bi, qi: (bi + batch0, qi, 0)),
            full(t, batch0), full(t, batch0), full(sc, 0), full(sc, 0),
            pl.BlockSpec(bias_tbl.shape, lambda bi, qi: (0, 0, 0, 0)),
        ],
        out_specs=pl.BlockSpec((1, NA_QBLK, D_MODEL), lambda bi, qi: (bi, qi, 0)),
        out_shape=jax.ShapeDtypeStruct((n_batch, t, D_MODEL), BF16),
        compiler_params=_params("arbitrary", "arbitrary"),
        name="na_attn",
    )(q, k, v, ctx_k, ctx_v, bias_tbl)


def _post_kernel(oc_ref, ol_ref, x_ref, wo_ref, gate_ref, g_ref, shift_ref, scale_ref, wrh_ref, wrl_ref,
                 y_ref, h_ref, lg_ref):
    is_ctx = pl.program_id(0) < N_TOK // TM
    o = jnp.where(is_ctx, oc_ref[...], ol_ref[...])
    y = x_ref[...] + gate_ref[0] * _dot(o, wo_ref[...])
    y_ref[...] = y
    ms = jnp.mean(y * y, axis=-1, keepdims=True)
    h = y * lax.rsqrt(ms + RMS_EPS) * g_ref[...]
    h = h * (1.0 + scale_ref[0]) + shift_ref[0]
    hh, hl = _split_bf16(h)
    h_ref[...] = hh
    lg_ref[...] = _dot_nt(wrh_ref[...], hh) + _dot_nt(wrh_ref[...], hl) + _dot_nt(wrl_ref[...], hh)


def _post_attention(o_ctx, o_lat, x, wo_bf16, norm_g, mod3, layer, lat_seq, wr_hi, wr_lo):
    n_tok = x.shape[0]
    ctx_tiles = N_TOK // TM
    tile = lambda w: pl.BlockSpec((TM, w), lambda i: (i, 0))
    const = lambda shape: pl.BlockSpec(shape, lambda i: (0,) * len(shape))
    return pl.pallas_call(
        _post_kernel,
        grid=(n_tok // TM,),
        in_specs=[
            pl.BlockSpec((TM, D_MODEL), lambda i: (jnp.minimum(i, ctx_tiles - 1), 0)),
            pl.BlockSpec((TM, D_MODEL), lambda i: (jnp.maximum(i - ctx_tiles, 0), 0)),
            tile(D_MODEL), const((D_MODEL, D_MODEL)),
            _mod_spec(layer, 2, lat_seq),
            const((1, D_MODEL)),
            _mod_spec(layer, 3, lat_seq),
            _mod_spec(layer, 4, lat_seq),
            const((N_EXPERTS, D_MODEL)), const((N_EXPERTS, D_MODEL)),
        ],
        out_specs=[tile(D_MODEL), tile(D_MODEL), pl.BlockSpec((N_EXPERTS, TM), lambda i: (0, i))],
        out_shape=[jax.ShapeDtypeStruct((n_tok, D_MODEL), F32),
                   jax.ShapeDtypeStruct((n_tok, D_MODEL), BF16),
                   jax.ShapeDtypeStruct((N_EXPERTS, n_tok), F32)],
        compiler_params=_params("arbitrary"),
        name="post_attn",
    )(o_ctx, o_lat, x, wo_bf16, mod3, norm_g.reshape(1, D_MODEL), mod3, mod3, wr_hi, wr_lo)


def _excl_prefix(mask, tri):
    outs, offs = [], []
    off = jnp.zeros((mask.shape[0], 1), F32)
    for c in range(mask.shape[1] // MXU_DIM):
        xc = mask[:, c * MXU_DIM:(c + 1) * MXU_DIM]
        offs.append(off)
        outs.append(_dot(xc.astype(BF16), tri) + off)
        off = off + jnp.sum(xc, axis=1, keepdims=True)
    offs.append(off)
    return jnp.concatenate(outs, axis=1), offs


def _plan_kernel(lg_ref, tri_ref, pos_t_ref, posm_ref, aff_ref, cum_ref):
    lg = lg_ref[...]
    ex = jnp.exp(lg - jnp.max(lg, axis=0, keepdims=True))
    aff = ex / jnp.sum(ex, axis=0, keepdims=True)
    bits = lax.bitcast_convert_type(aff, I32)

    def search(i, thr):
        cand = thr | jnp.left_shift(jnp.int32(1), 30 - i)
        cnt = jnp.sum(jnp.where(bits >= cand, 1.0, 0.0), axis=1, keepdims=True)
        return jnp.where(cnt >= CAP, cand, thr)

    thr = lax.fori_loop(0, 31, search, jnp.zeros((N_EXPERTS, 1), I32))
    tri = tri_ref[...]
    gt = bits > thr
    eq = jnp.where(bits == thr, 1.0, 0.0)
    need = CAP - jnp.sum(jnp.where(gt, 1.0, 0.0), axis=1, keepdims=True)
    eq_rank, _ = _excl_prefix(eq, tri)
    sel = jnp.where(gt | ((eq > 0.0) & (eq_rank < need)), 1.0, 0.0)
    pos, offs = _excl_prefix(sel, tri)
    posm = jnp.where(sel > 0.0, pos, -1.0)

    for c in range(N_CHUNKS):
        sl = slice(c * TOK_CHUNK, (c + 1) * TOK_CHUNK)
        posm_ref[0, c] = posm[:, sl].astype(I32)
        aff_ref[0, c] = aff[:, sl]
    lane = lax.broadcasted_iota(I32, (N_EXPERTS, LANES), 1)
    per_chunk = TOK_CHUNK // MXU_DIM
    cum = jnp.zeros((N_EXPERTS, LANES), F32)
    for c in range(N_CHUNKS + 1):
        cum = jnp.where(lane == c, offs[c * per_chunk], cum)
    cum_ref[0] = cum.astype(I32)
    padded = jnp.concatenate([posm, jnp.full((LANES - N_EXPERTS, N_TOK), -1.0, F32)], axis=0)
    for t in range(N_TOK // LANES):
        pos_t_ref[0, t * LANES:(t + 1) * LANES, :] = padded[:, t * LANES:(t + 1) * LANES].T


def _plan(logits_t, tri):
    ns = N_STREAMS
    chunked = lambda dt: jax.ShapeDtypeStruct((ns, N_CHUNKS, N_EXPERTS, TOK_CHUNK), dt)
    return pl.pallas_call(
        _plan_kernel,
        grid=(ns,),
        in_specs=[pl.BlockSpec((N_EXPERTS, N_TOK), lambda s: (0, s)),
                  pl.BlockSpec((MXU_DIM, MXU_DIM), lambda s: (0, 0))],
        out_specs=[pl.BlockSpec((1, N_TOK, LANES), lambda s: (s, 0, 0)),
                   pl.BlockSpec((1, N_CHUNKS, N_EXPERTS, TOK_CHUNK), lambda s: (s, 0, 0, 0)),
                   pl.BlockSpec((1, N_CHUNKS, N_EXPERTS, TOK_CHUNK), lambda s: (s, 0, 0, 0)),
                   pl.BlockSpec((1, N_EXPERTS, LANES), lambda s: (s, 0, 0))],
        out_shape=[jax.ShapeDtypeStruct((ns, N_TOK, LANES), F32), chunked(I32), chunked(F32),
                   jax.ShapeDtypeStruct((ns, N_EXPERTS, LANES), I32)],
        compiler_params=_params("arbitrary"),
        name="plan",
    )(logits_t, tri)


def _gather_kernel(cum_ref, h_ref, posm_ref, aff_ref, xe_ref, gc_ref, acc_ref, gacc_ref):
    s, e = pl.program_id(0), pl.program_id(1)
    base = (s * N_EXPERTS + e) * LANES
    for rb in range(CAP // ROW_BLK):
        r0 = rb * ROW_BLK
        c_lo = jnp.int32(0)
        c_hi = jnp.int32(0)
        for c in range(N_CHUNKS):
            c_lo = c_lo + (cum_ref[base + c + 1] <= r0).astype(I32)
            c_hi = c_hi + (cum_ref[base + c] < r0 + ROW_BLK).astype(I32)
        acc_ref[...] = jnp.zeros_like(acc_ref)
        gacc_ref[...] = jnp.zeros_like(gacc_ref)
        row = lax.broadcasted_iota(I32, (ROW_BLK, TOK_CHUNK), 0) + r0

        def chunk(c, carry):
            hit = row == posm_ref[0, c, pl.ds(e, 1), :]
            onehot = jnp.where(hit, 1.0, 0.0).astype(BF16)
            tok = pl.ds(pl.multiple_of(c * TOK_CHUNK, TOK_CHUNK), TOK_CHUNK)
            acc_ref[...] += _dot(onehot, h_ref[0, tok, :])
            gacc_ref[...] += jnp.sum(jnp.where(hit, aff_ref[0, c, pl.ds(e, 1), :], 0.0),
                                     axis=1, keepdims=True)
            return carry

        lax.fori_loop(c_lo, c_hi, chunk, 0)
        xe_ref[0, 0, r0:r0 + ROW_BLK, :] = acc_ref[...].astype(BF16)
        gc_ref[0, 0, r0:r0 + ROW_BLK, :] = jnp.broadcast_to(gacc_ref[...], (ROW_BLK, LANES))


def _gather(cum_flat, h2, posm, aff):
    ns = h2.shape[0]
    plan_spec = pl.BlockSpec((1, N_CHUNKS, N_EXPERTS, TOK_CHUNK), lambda s, e, cum: (s, 0, 0, 0))
    return pl.pallas_call(
        _gather_kernel,
        grid_spec=pltpu.PrefetchScalarGridSpec(
            num_scalar_prefetch=1,
            grid=(ns, N_EXPERTS),
            in_specs=[pl.BlockSpec((1, N_TOK, D_MODEL), lambda s, e, cum: (s, 0, 0)),
                      plan_spec, plan_spec],
            out_specs=[pl.BlockSpec((1, 1, CAP, D_MODEL), lambda s, e, cum: (s, e, 0, 0)),
                       pl.BlockSpec((1, 1, CAP, LANES), lambda s, e, cum: (s, e, 0, 0))],
            scratch_shapes=[pltpu.VMEM((ROW_BLK, D_MODEL), F32), pltpu.VMEM((ROW_BLK, 1), F32)],
        ),
        out_shape=[jax.ShapeDtypeStruct((ns, N_EXPERTS, CAP, D_MODEL), BF16),
                   jax.ShapeDtypeStruct((ns, N_EXPERTS, CAP, LANES), F32)],
        compiler_params=_params("arbitrary", "arbitrary"),
        name="moe_gather",
    )(cum_flat, h2, posm, aff)


def _ffn_kernel(x_ref, wg_ref, wu_ref, wd_ref, gc_ref, ye_ref, acc_ref, *, row_tile):
    f = pl.program_id(1)
    ns = x_ref.shape[0]

    @pl.when(f == 0)
    def _():
        acc_ref[...] = jnp.zeros_like(acc_ref)

    wg = wg_ref[0, 0].astype(BF16)
    wu = wu_ref[0, 0].astype(BF16)
    wd = wd_ref[0, 0].astype(BF16)
    for s in range(ns):
        for r0 in range(0, CAP, row_tile):
            x = x_ref[s, 0, r0:r0 + row_tile, :]
            hid = _silu(_dot(x, wg)) * _dot(x, wu)
            acc_ref[s, r0:r0 + row_tile, :] += _dot(hid.astype(BF16), wd)

    @pl.when(f == pl.num_programs(1) - 1)
    def _():
        for s in range(ns):
            ye_ref[s, 0] = (acc_ref[s] * gc_ref[s, 0, :, 0:1]).astype(BF16)


def _ffn(xe, gc, w_gate, w_up, w_down, layer):
    ns = xe.shape[0]
    return pl.pallas_call(
        functools.partial(_ffn_kernel, row_tile=512),
        grid=(N_EXPERTS, EXPERT_FF // FF_TILE),
        in_specs=[
            pl.BlockSpec((ns, 1, CAP, D_MODEL), lambda e, f: (0, e, 0, 0)),
            pl.BlockSpec((1, 1, D_MODEL, FF_TILE), lambda e, f: (layer, e, 0, f)),
            pl.BlockSpec((1, 1, D_MODEL, FF_TILE), lambda e, f: (layer, e, 0, f)),
            pl.BlockSpec((1, 1, FF_TILE, D_MODEL), lambda e, f: (layer, e, f, 0)),
            pl.BlockSpec((ns, 1, CAP, LANES), lambda e, f: (0, e, 0, 0)),
        ],
        out_specs=pl.BlockSpec((ns, 1, CAP, D_MODEL), lambda e, f: (0, e, 0, 0)),
        out_shape=jax.ShapeDtypeStruct((ns, N_EXPERTS, CAP, D_MODEL), BF16),
        scratch_shapes=[pltpu.VMEM((ns, CAP, D_MODEL), F32)],
        compiler_params=_params("arbitrary", "arbitrary"),
        name="moe_ffn",
    )(xe, w_gate, w_up, w_down, gc)


def _scatter_kernel(cum_ref, ye_ref, pos_t_ref, y_ref, gate_ref, *refs, split):
    out_refs, acc_ref = refs[:-1], refs[-1]
    s, c = pl.program_id(0), pl.program_id(1)
    acc_ref[...] = jnp.zeros_like(acc_ref)
    lane = lax.broadcasted_iota(I32, (TOK_CHUNK, ROW_BLK), 1).astype(F32)
    for e in range(N_EXPERTS):
        base = (s * N_EXPERTS + e) * LANES
        lo = cum_ref[base + c]
        hi = cum_ref[base + c + 1]
        rb_lo = lo // ROW_BLK
        rb_hi = jnp.where(hi > lo, (hi - 1) // ROW_BLK + 1, rb_lo)
        col = pos_t_ref[0, :, e:e + 1]

        def block(rb, carry):
            onehot = jnp.where(col == lane + (rb * ROW_BLK).astype(F32), 1.0, 0.0).astype(BF16)
            rows = pl.ds(pl.multiple_of(rb * ROW_BLK, ROW_BLK), ROW_BLK)
            acc_ref[...] += _dot(onehot, ye_ref[0, e, rows, :])
            return carry

        lax.fori_loop(rb_lo, rb_hi, block, 0)
    if split:
        for si, out_ref in enumerate(out_refs):
            @pl.when(s == si)
            def _(out_ref=out_ref):
                out_ref[...] = y_ref[...] + gate_ref[0] * acc_ref[...]
    else:
        out_refs[0][...] = y_ref[...] + gate_ref[0] * acc_ref[...]


def _scatter(cum_flat, ye, pos_t, y, mod3, layer, lat_seq, split):
    ns = ye.shape[0]

    def gate_index(s, c, cum):
        row = _mod_row(s * N_CHUNKS + c, TOK_CHUNK, lat_seq)
        return ((layer * N_MOD_ROWS + row) * 6 + 5, 0, 0)

    chunk = (TOK_CHUNK, D_MODEL)
    if split:
        out_specs = [
            pl.BlockSpec(chunk, lambda s, c, cum, si=si: (
                jnp.where(s == si, c, jnp.where(s < si, 0, N_CHUNKS - 1)), 0))
            for si in range(ns)]
        out_shape = [jax.ShapeDtypeStruct((N_TOK, D_MODEL), F32)] * ns
    else:
        out_specs = pl.BlockSpec(chunk, lambda s, c, cum: (s * N_CHUNKS + c, 0))
        out_shape = jax.ShapeDtypeStruct(y.shape, F32)
    return pl.pallas_call(
        functools.partial(_scatter_kernel, split=split),
        grid_spec=pltpu.PrefetchScalarGridSpec(
            num_scalar_prefetch=1,
            grid=(ns, N_CHUNKS),
            in_specs=[
                pl.BlockSpec((1, N_EXPERTS, CAP, D_MODEL), lambda s, c, cum: (s, 0, 0, 0),
                             pipeline_mode=pl.Buffered(1)),
                pl.BlockSpec((1, TOK_CHUNK, LANES), lambda s, c, cum: (s, c, 0)),
                pl.BlockSpec(chunk, lambda s, c, cum: (s * N_CHUNKS + c, 0)),
                pl.BlockSpec((1, 1, D_MODEL), gate_index),
            ],
            out_specs=out_specs,
            scratch_shapes=[pltpu.VMEM(chunk, F32)],
        ),
        out_shape=out_shape,
        compiler_params=_params("arbitrary", "arbitrary"),
        name="moe_scatter",
    )(cum_flat, ye, pos_t, y, mod3)


def _rope_tables(n_tokens):
    t = np.arange(n_tokens)
    row = (t // GRID_W).astype(np.float32)
    col = (t % GRID_W).astype(np.float32)
    pairs = HEAD_DIM // 4
    inv_freq = ROPE_THETA ** (-jnp.arange(pairs, dtype=F32) / pairs)
    ang = jnp.concatenate([row[:, None] * inv_freq, col[:, None] * inv_freq], axis=-1)
    cos, sin = jnp.cos(ang), jnp.sin(ang)
    reps = MXU_DIM // HEAD_DIM
    cos_t = jnp.tile(jnp.concatenate([cos, cos], axis=-1), (1, reps))
    sin_t = jnp.tile(jnp.concatenate([-sin, sin], axis=-1), (1, reps))
    return (jnp.concatenate([cos_t, jnp.ones((TM, MXU_DIM), F32)], axis=0),
            jnp.concatenate([sin_t, jnp.zeros((TM, MXU_DIM), F32)], axis=0))


def _dup_cache(cache):
    b, s, hk, hd = cache.shape
    return jnp.broadcast_to(cache[:, :, :, None, :], (b, s, hk, 2, hd)).reshape(b, s, 2 * hk * hd).astype(BF16)


def kernel(x_prompt, x_sample, cache_attn_k, cache_attn_v, cache_na_k, cache_na_v, c, c_ctx,
           norm1_g, norm2_g, w_ada, b_ada, attn_w_qkv, attn_q_gain, attn_k_gain, attn_w_o,
           na_w_qkv, na_q_gain, na_k_gain, na_rpb, na_w_o,
           moe_w_router, moe_w_gate, moe_w_up, moe_w_down):
    bc, tc, _ = x_prompt.shape
    bl, tl, _ = x_sample.shape
    depth = w_ada.shape[0]
    assert bc * tc == N_TOK and bl * tl == N_TOK and 1 + bl <= N_MOD_ROWS
    assert tl % TM == 0 and tl % TOK_CHUNK == 0
    n_all = N_STREAMS * N_TOK

    cond = jnp.zeros((N_MOD_ROWS, D_MODEL), F32).at[0].set(c_ctx).at[1:1 + bl].set(c)
    mod3 = _ada(cond, w_ada, b_ada).reshape(depth * N_MOD_ROWS * 6, 1, D_MODEL)

    seg = jnp.asarray(np.kron(np.eye(MXU_DIM // HEAD_DIM), np.ones((HEAD_DIM, HEAD_DIM))), BF16)
    tri = jnp.asarray(np.triu(np.ones((MXU_DIM, MXU_DIM)), k=1), BF16)
    rope_tables = _rope_tables(tl)

    y = jnp.concatenate([x_prompt.reshape(N_TOK, D_MODEL), x_sample.reshape(N_TOK, D_MODEL)], axis=0)
    new_k, new_v = [], []
    for i in range(depth):
        j = i // 2
        gqa = i % 2 == 0
        if gqa:
            w_qkv, q_gain, k_gain, w_o = attn_w_qkv[j], attn_q_gain[j], attn_k_gain[j], attn_w_o[j]
            kv_heads = N_KV_HEADS
        else:
            w_qkv, q_gain, k_gain, w_o = na_w_qkv[j], na_q_gain[j], na_k_gain[j], na_w_o[j]
            kv_heads = N_HEADS
        wr_hi, wr_lo = _split_bf16(moe_w_router[i].T)

        q, kb, vb, kf, vf = _qkv(y, norm1_g[i], mod3, i, tl, w_qkv.astype(BF16), q_gain, k_gain, seg,
                                 kv_heads * HEAD_DIM, rope_tables if gqa else None)
        kvb = kb.shape[1]
        o_ctx = _attention(q.reshape(n_all // tc, tc, D_MODEL), kb.reshape(n_all // tc, tc, kvb),
                           vb.reshape(n_all // tc, tc, kvb), bc, 0)
        q_l, kb_l, vb_l = (a.reshape(n_all // tl, tl, a.shape[1]) for a in (q, kb, vb))
        if gqa:
            o_lat = _attention(q_l, kb_l, vb_l, bl, N_TOK // tl,
                               _dup_cache(cache_attn_k[:, j]), _dup_cache(cache_attn_v[:, j]))
        else:
            past = cache_na_k.shape[2]
            o_lat = _na_attention(q_l, kb_l, vb_l, bl, N_TOK // tl,
                                  cache_na_k[:, j].reshape(bl, past, D_MODEL).astype(BF16),
                                  cache_na_v[:, j].reshape(bl, past, D_MODEL).astype(BF16),
                                  _na_bias_table(na_rpb[j]))
        new_k.append(kf[:N_TOK].reshape(bc, 1, tc, kv_heads, HEAD_DIM))
        new_v.append(vf[:N_TOK].reshape(bc, 1, tc, kv_heads, HEAD_DIM))

        y, h2, logits_t = _post_attention(o_ctx.reshape(N_TOK, D_MODEL), o_lat.reshape(N_TOK, D_MODEL), y,
                                          w_o.astype(BF16), norm2_g[i], mod3, i, tl, wr_hi, wr_lo)
        pos_t, posm, aff, cum = _plan(logits_t, tri)
        cum_flat = cum.reshape(-1)
        xe, gc = _gather(cum_flat, h2.reshape(N_STREAMS, N_TOK, D_MODEL), posm, aff)
        ye = _ffn(xe, gc, moe_w_gate, moe_w_up, moe_w_down, i)
        y = _scatter(cum_flat, ye, pos_t, y, mod3, i, tl, split=(i == depth - 1))

    y_ctx, y_lat = y
    return (y_ctx.reshape(bc, tc, D_MODEL), y_lat.reshape(bl, tl, D_MODEL),
            jnp.concatenate(new_k[0::2], axis=1), jnp.concatenate(new_v[0::2], axis=1),
            jnp.concatenate(new_k[1::2], axis=1), jnp.concatenate(new_v[1::2], axis=1))
```

```python
import functools

import jax
import jax.numpy as jnp
import numpy as np
from jax import lax
from jax.experimental import pallas as pl
from jax.experimental.pallas import tpu as pltpu

F32 = jnp.float32
BF16 = jnp.bfloat16
I32 = jnp.int32

D_MODEL = 1024
N_HEADS = 16
N_KV_HEADS = 4
HEAD_DIM = 64
GRID_W = 64
WIN_R = 8
WIN_C = 16
N_EXPERTS = 16
EXPERT_FF = 2048
ROPE_THETA = 10000.0
RMS_EPS = 1e-6
NEG_INF = -1e30
F32_MIN_NORMAL_BITS = 0x00800000

LANES = 128
MXU_DIM = 256
VMEM_LIMIT = 56 * 1024 * 1024

N_STREAMS = 2
N_TOK = 8192
CAP = 2 * N_TOK // N_EXPERTS
TM = 512
N_MOD_ROWS = 16
ROW_TILE = D_MODEL // LANES
GROUP = MXU_DIM
N_GROUPS = N_TOK // GROUP
FIN_TOK = 256
N_FIN = N_TOK // FIN_TOK
MOVE_UNROLL = 8
FF_TILE = 512
NA_BAND = 10
NA_QBLK = 128
N_REL_R = 2 * WIN_R - 1
N_REL_C = 2 * WIN_C - 1


def _params(*sem):
    return pltpu.CompilerParams(dimension_semantics=sem, vmem_limit_bytes=VMEM_LIMIT)


def _dot(a, b):
    return jnp.dot(a, b, preferred_element_type=F32)


def _dot_nt(a, b):
    return lax.dot_general(a, b, (((1,), (1,)), ((), ())), preferred_element_type=F32)


def _split_bf16(x):
    hi = x.astype(BF16)
    lo = (x - hi.astype(F32)).astype(BF16)
    return hi, lo


def _silu(x):
    return x * (1.0 / (1.0 + jnp.exp(-x)))


def _ada_kernel(cond_ref, w_ref, b_ref, out_ref):
    sx = _silu(cond_ref[...])
    xh, xl = _split_bf16(sx)
    wh, wl = _split_bf16(w_ref[0])
    out_ref[0] = _dot(xh, wh) + _dot(xl, wh) + _dot(xh, wl) + b_ref[0]


def _ada(cond, w_ada, b_ada):
    depth = w_ada.shape[0]
    tn = 1024
    n_out = w_ada.shape[2]
    return pl.pallas_call(
        _ada_kernel,
        grid=(depth, n_out // tn),
        in_specs=[
            pl.BlockSpec((N_MOD_ROWS, D_MODEL), lambda l, n: (0, 0)),
            pl.BlockSpec((1, D_MODEL, tn), lambda l, n: (l, 0, n)),
            pl.BlockSpec((1, 1, tn), lambda l, n: (l, 0, n)),
        ],
        out_specs=pl.BlockSpec((1, N_MOD_ROWS, tn), lambda l, n: (l, 0, n)),
        out_shape=jax.ShapeDtypeStruct((depth, N_MOD_ROWS, n_out), F32),
        compiler_params=_params("arbitrary", "arbitrary"),
        name="ada",
    )(cond, w_ada, b_ada.reshape(depth, 1, n_out))


def _mod_row(tile, tile_rows, lat_seq):
    ctx_tiles = N_TOK // tile_rows
    return jnp.where(tile < ctx_tiles, 0, 1 + (tile - ctx_tiles) // (lat_seq // tile_rows))


def _mod_spec(layer, which, lat_seq):
    def index(i):
        return ((layer * N_MOD_ROWS + _mod_row(i, TM, lat_seq)) * 6 + which, 0, 0)
    return pl.BlockSpec((1, 1, D_MODEL), index)


def _head_norm(z, seg, gain):
    ss = _dot((z * z).astype(BF16), seg)
    return z * lax.rsqrt(ss * (1.0 / HEAD_DIM) + RMS_EPS) * gain


def _rope(z, cos_t, sin_t):
    lane = lax.broadcasted_iota(I32, z.shape, 1)
    first = (lane & 32) == 0
    n = z.shape[1]
    partner = jnp.where(first, pltpu.roll(z, n - 32, axis=1), pltpu.roll(z, 32, axis=1))
    return z * cos_t + partner * sin_t


def _dup_heads(z):
    outs = []
    for b in range(z.shape[1] // LANES):
        x = z[:, b * LANES:(b + 1) * LANES]
        xr = pltpu.roll(x, HEAD_DIM, axis=1)
        lo = lax.broadcasted_iota(I32, x.shape, 1) < HEAD_DIM
        outs.append(jnp.where(lo, x, xr))
        outs.append(jnp.where(lo, xr, x))
    return jnp.concatenate(outs, axis=1)


def _qkv_kernel(*refs, kv_width, rope):
    it = iter(refs)
    x_ref, g_ref, shift_ref, scale_ref, w_ref, qg_ref, kg_ref, seg_ref = (next(it) for _ in range(8))
    cos_ref = sin_ref = None
    if rope:
        cos_ref, sin_ref = next(it), next(it)
    q_ref, kb_ref, vb_ref, kf_ref, vf_ref = (next(it) for _ in range(5))
    dup = kv_width == N_KV_HEADS * HEAD_DIM

    x = x_ref[...]
    ms = jnp.mean(x * x, axis=-1, keepdims=True)
    h = x * lax.rsqrt(ms + RMS_EPS) * g_ref[...]
    h = h * (1.0 + scale_ref[0]) + shift_ref[0]
    hb = h.astype(BF16)
    seg = seg_ref[...]
    if rope:
        cos_t, sin_t = cos_ref[...], sin_ref[...]

    q_width = N_HEADS * HEAD_DIM
    for cidx in range(q_width // MXU_DIM):
        c0 = cidx * MXU_DIM
        z = _head_norm(_dot(hb, w_ref[:, c0:c0 + MXU_DIM]), seg, qg_ref[...])
        if rope:
            z = _rope(z, cos_t, sin_t)
        q_ref[:, c0:c0 + MXU_DIM] = (z * (HEAD_DIM ** -0.5)).astype(BF16)
    for cidx in range(kv_width // MXU_DIM):
        c0 = cidx * MXU_DIM
        z = _head_norm(_dot(hb, w_ref[:, q_width + c0:q_width + c0 + MXU_DIM]), seg, kg_ref[...])
        kf_ref[:, c0:c0 + MXU_DIM] = z
        if rope:
            z = _rope(z, cos_t, sin_t)
        v = _dot(hb, w_ref[:, q_width + kv_width + c0:q_width + kv_width + c0 + MXU_DIM])
        vf_ref[:, c0:c0 + MXU_DIM] = v
        if dup:
            kb_ref[:, 2 * c0:2 * c0 + 2 * MXU_DIM] = _dup_heads(z).astype(BF16)
            vb_ref[:, 2 * c0:2 * c0 + 2 * MXU_DIM] = _dup_heads(v).astype(BF16)
        else:
            kb_ref[:, c0:c0 + MXU_DIM] = z.astype(BF16)
            vb_ref[:, c0:c0 + MXU_DIM] = v.astype(BF16)


def _qkv(x, norm_g, mod3, layer, lat_seq, w_bf16, q_gain, k_gain, seg, kv_width, rope_tables):
    n_tok = x.shape[0]
    ctx_tiles = N_TOK // TM
    n_w = w_bf16.shape[1]
    kvb_width = 2 * kv_width if kv_width == N_KV_HEADS * HEAD_DIM else kv_width
    tile = lambda w: pl.BlockSpec((TM, w), lambda i: (i, 0))
    ctx_tile = lambda w: pl.BlockSpec((TM, w), lambda i: (jnp.minimum(i, ctx_tiles), 0))
    const = lambda shape: pl.BlockSpec(shape, lambda i: (0,) * len(shape))
    in_specs = [
        tile(D_MODEL),
        const((1, D_MODEL)),
        _mod_spec(layer, 0, lat_seq),
        _mod_spec(layer, 1, lat_seq),
        const((D_MODEL, n_w)),
        const((1, MXU_DIM)),
        const((1, MXU_DIM)),
        const((MXU_DIM, MXU_DIM)),
    ]
    args = [x, norm_g.reshape(1, D_MODEL), mod3, mod3, w_bf16,
            jnp.tile(q_gain, MXU_DIM // HEAD_DIM).reshape(1, MXU_DIM),
            jnp.tile(k_gain, MXU_DIM // HEAD_DIM).reshape(1, MXU_DIM), seg]
    if rope_tables is not None:
        seq_tiles = lat_seq // TM
        rope_index = lambda i: (jnp.where(i < ctx_tiles, seq_tiles, i % seq_tiles), 0)
        in_specs += [pl.BlockSpec((TM, MXU_DIM), rope_index)] * 2
        args += list(rope_tables)
    out_specs = [tile(D_MODEL), tile(kvb_width), tile(kvb_width), ctx_tile(kv_width), ctx_tile(kv_width)]
    out_shape = [jax.ShapeDtypeStruct((n_tok, D_MODEL), BF16),
                 jax.ShapeDtypeStruct((n_tok, kvb_width), BF16),
                 jax.ShapeDtypeStruct((n_tok, kvb_width), BF16),
                 jax.ShapeDtypeStruct((N_TOK + TM, kv_width), F32),
                 jax.ShapeDtypeStruct((N_TOK + TM, kv_width), F32)]
    return pl.pallas_call(
        functools.partial(_qkv_kernel, kv_width=kv_width, rope=rope_tables is not None),
        grid=(n_tok // TM,),
        in_specs=in_specs,
        out_specs=out_specs,
        out_shape=out_shape,
        compiler_params=_params("arbitrary"),
        name="qkv",
    )(*args)


def _two_head_rows(qj):
    lo = lax.broadcasted_iota(I32, qj.shape, 1) < HEAD_DIM
    zero = jnp.zeros_like(qj)
    return jnp.concatenate([jnp.where(lo, qj, zero), jnp.where(lo, zero, qj)], axis=0)


def _merge_two_heads(r):
    tq = r.shape[0] // 2
    lo = lax.broadcasted_iota(I32, (tq, LANES), 1) < HEAD_DIM
    return jnp.where(lo, r[:tq], r[tq:])


def _attn_kernel(*refs, n_kv_blocks, has_ctx):
    if has_ctx:
        q_ref, k_ref, v_ref, ck_ref, cv_ref, o_ref = refs
    else:
        q_ref, k_ref, v_ref, o_ref = refs
    n_q_blocks = D_MODEL // LANES
    for j in range(n_q_blocks):
        kb = (j * n_kv_blocks) // n_q_blocks
        ksl = slice(kb * LANES, (kb + 1) * LANES)
        q2 = _two_head_rows(q_ref[0, :, j * LANES:(j + 1) * LANES])
        s = _dot_nt(q2, k_ref[0, :, ksl])
        m = jnp.max(s, axis=-1, keepdims=True)
        if has_ctx:
            sc = _dot_nt(q2, ck_ref[0, :, ksl])
            m = jnp.maximum(m, jnp.max(sc, axis=-1, keepdims=True))
        p = jnp.exp(s - m)
        l = jnp.sum(p, axis=-1, keepdims=True)
        r = _dot(p.astype(BF16), v_ref[0, :, ksl])
        if has_ctx:
            pc = jnp.exp(sc - m)
            l = l + jnp.sum(pc, axis=-1, keepdims=True)
            r = r + _dot(pc.astype(BF16), cv_ref[0, :, ksl])
        r = r / l
        o_ref[0, :, j * LANES:(j + 1) * LANES] = _merge_two_heads(r).astype(BF16)


def _attention(q, k, v, n_batch, batch0, ctx_k=None, ctx_v=None, tq=256):
    t = q.shape[1]
    s, w = k.shape[1], k.shape[2]
    has_ctx = ctx_k is not None
    in_specs = [
        pl.BlockSpec((1, tq, D_MODEL), lambda bi, qi: (bi + batch0, qi, 0)),
        pl.BlockSpec((1, s, w), lambda bi, qi: (bi + batch0, 0, 0)),
        pl.BlockSpec((1, s, w), lambda bi, qi: (bi + batch0, 0, 0)),
    ]
    args = [q, k, v]
    if has_ctx:
        sc = ctx_k.shape[1]
        in_specs += [pl.BlockSpec((1, sc, w), lambda bi, qi: (bi, 0, 0))] * 2
        args += [ctx_k, ctx_v]
    return pl.pallas_call(
        functools.partial(_attn_kernel, n_kv_blocks=w // LANES, has_ctx=has_ctx),
        grid=(n_batch, t // tq),
        in_specs=in_specs,
        out_specs=pl.BlockSpec((1, tq, D_MODEL), lambda bi, qi: (bi, qi, 0)),
        out_shape=jax.ShapeDtypeStruct((n_batch, t, D_MODEL), BF16),
        compiler_params=_params("arbitrary", "arbitrary"),
        name="attn",
    )(*args)


def _na_band_start(blk, rows):
    row_start = jnp.clip(2 * blk - WIN_R // 2, 0, rows - WIN_R)
    return jnp.minimum(row_start, rows - NA_BAND)


def _na_kernel(q_ref, k_ref, v_ref, ck_ref, cv_ref, bias_ref, o_ref, *, rows):
    blk = pl.program_id(1)
    band_start = _na_band_start(blk, rows)
    n_band = NA_BAND * GRID_W
    band = pl.ds(pl.multiple_of(band_start * GRID_W, LANES), n_band)
    shape = (NA_QBLK, n_band)
    q_r = 2 * blk + lax.broadcasted_iota(I32, shape, 0) // GRID_W
    k_r = band_start + lax.broadcasted_iota(I32, shape, 1) // GRID_W
    q_rs = jnp.clip(q_r - WIN_R // 2, 0, rows - WIN_R)
    row_ok1 = (k_r >= q_rs) & (k_r < q_rs + WIN_R)
    row_ok = jnp.concatenate([row_ok1, row_ok1], axis=0)
    pair_shift = (band_start - 2 * blk + WIN_R) // 2
    for j in range(D_MODEL // LANES):
        sl = slice(j * LANES, (j + 1) * LANES)
        q2 = _two_head_rows(q_ref[0, :, sl])
        s = _dot_nt(q2, k_ref[0, band, sl])
        bias = jnp.concatenate(
            [jnp.concatenate([bias_ref[2 * j + hh, pair_shift + m] for m in range(NA_BAND // 2)], axis=1)
             for hh in range(2)], axis=0)
        s = jnp.where(row_ok, s + bias, NEG_INF)
        sc = _dot_nt(q2, ck_ref[0, :, sl])
        m = jnp.maximum(jnp.max(s, axis=-1, keepdims=True), jnp.max(sc, axis=-1, keepdims=True))
        p = jnp.exp(s - m)
        pc = jnp.exp(sc - m)
        l = jnp.sum(p, axis=-1, keepdims=True) + jnp.sum(pc, axis=-1, keepdims=True)
        r = _dot(p.astype(BF16), v_ref[0, band, sl]) + _dot(pc.astype(BF16), cv_ref[0, :, sl])
        r = r / l
        o_ref[0, :, sl] = _merge_two_heads(r).astype(BF16)


def _na_bias_kernel(rpb_ref, out_ref, tz_ref):
    h = pl.program_id(0)
    shape = (GRID_W, LANES)
    q_col = lax.broadcasted_iota(I32, shape, 0)
    lane = lax.broadcasted_iota(I32, shape, 1)
    k_col = lane & (GRID_W - 1)
    col_start = jnp.clip(q_col - WIN_C // 2, 0, GRID_W - WIN_C)
    col_ok = (k_col >= col_start) & (k_col < col_start + WIN_C)
    rel_c = jnp.clip(k_col - q_col + WIN_C - 1, 0, N_REL_C - 1)
    for a in range(N_REL_R):
        acc = jnp.zeros(shape, F32)
        for b in range(N_REL_C):
            acc = jnp.where(rel_c == b, rpb_ref[(h * N_REL_R + a) * N_REL_C + b], acc)
        tz_ref[a] = jnp.where(col_ok, acc, NEG_INF)
    left = lane < GRID_W
    for dd in range(WIN_R + 1):
        for qr in range(2):
            rel = [min(max(2 * dd - WIN_R + kr - qr + WIN_R - 1, 0), N_REL_R - 1) for kr in range(2)]
            out_ref[0, dd, qr * GRID_W:(qr + 1) * GRID_W, :] = jnp.where(left, tz_ref[rel[0]], tz_ref[rel[1]])


def _na_bias_table(rpb):
    return pl.pallas_call(
        _na_bias_kernel,
        grid=(N_HEADS,),
        in_specs=[pl.BlockSpec(memory_space=pltpu.SMEM)],
        out_specs=pl.BlockSpec((1, WIN_R + 1, LANES, LANES), lambda h: (h, 0, 0, 0)),
        out_shape=jax.ShapeDtypeStruct((N_HEADS, WIN_R + 1, LANES, LANES), F32),
        scratch_shapes=[pltpu.VMEM((N_REL_R, GRID_W, LANES), F32)],
        compiler_params=_params("arbitrary"),
        name="na_bias",
    )(rpb.reshape(-1))


def _na_attention(q, k, v, n_batch, batch0, ctx_k, ctx_v, bias_tbl):
    t = q.shape[1]
    sc = ctx_k.shape[1]
    rows = t // GRID_W
    full = lambda n, off: pl.BlockSpec((1, n, D_MODEL), lambda bi, qi: (bi + off, 0, 0))
    return pl.pallas_call(
        functools.partial(_na_kernel, rows=rows),
        grid=(n_batch, t // NA_QBLK),
        in_specs=[
            pl.BlockSpec((1, NA_QBLK, D_MODEL), lambda bi, qi: (bi + batch0, qi, 0)),
            full(t, batch0), full(t, batch0), full(sc, 0), full(sc, 0),
            pl.BlockSpec(bias_tbl.shape, lambda bi, qi: (0, 0, 0, 0)),
        ],
        out_specs=pl.BlockSpec((1, NA_QBLK, D_MODEL), lambda bi, qi: (bi, qi, 0)),
        out_shape=jax.ShapeDtypeStruct((n_batch, t, D_MODEL), BF16),
        compiler_params=_params("arbitrary", "arbitrary"),
        name="na_attn",
    )(q, k, v, ctx_k, ctx_v, bias_tbl)


def _post_kernel(oc_ref, ol_ref, x_ref, wo_ref, gate_ref, g_ref, shift_ref, scale_ref, wrh_ref, wrl_ref,
                 y_ref, h_ref, lg_ref):
    is_ctx = pl.program_id(0) < N_TOK // TM
    o = jnp.where(is_ctx, oc_ref[...], ol_ref[...])
    y = x_ref[...] + gate_ref[0] * _dot(o, wo_ref[...])
    y_ref[...] = y
    ms = jnp.mean(y * y, axis=-1, keepdims=True)
    h = y * lax.rsqrt(ms + RMS_EPS) * g_ref[...]
    h = h * (1.0 + scale_ref[0]) + shift_ref[0]
    for k in range(ROW_TILE):
        h_ref[pl.ds(k, TM, stride=ROW_TILE), :] = h[:, k * LANES:(k + 1) * LANES]
    hh, hl = _split_bf16(h)
    lg_ref[...] = _dot_nt(wrh_ref[...], hh) + _dot_nt(wrh_ref[...], hl) + _dot_nt(wrl_ref[...], hh)


def _post_attention(o_ctx, o_lat, x, wo_bf16, norm_g, mod3, layer, lat_seq, wr_hi, wr_lo):
    n_tok = x.shape[0]
    ctx_tiles = N_TOK // TM
    tile = lambda w: pl.BlockSpec((TM, w), lambda i: (i, 0))
    const = lambda shape: pl.BlockSpec(shape, lambda i: (0,) * len(shape))
    return pl.pallas_call(
        _post_kernel,
        grid=(n_tok // TM,),
        in_specs=[
            pl.BlockSpec((TM, D_MODEL), lambda i: (jnp.minimum(i, ctx_tiles - 1), 0)),
            pl.BlockSpec((TM, D_MODEL), lambda i: (jnp.maximum(i - ctx_tiles, 0), 0)),
            tile(D_MODEL), const((D_MODEL, D_MODEL)),
            _mod_spec(layer, 2, lat_seq),
            const((1, D_MODEL)),
            _mod_spec(layer, 3, lat_seq),
            _mod_spec(layer, 4, lat_seq),
            const((N_EXPERTS, D_MODEL)), const((N_EXPERTS, D_MODEL)),
        ],
        out_specs=[tile(D_MODEL), pl.BlockSpec((TM * ROW_TILE, LANES), lambda i: (i, 0)),
                   pl.BlockSpec((N_EXPERTS, TM), lambda i: (0, i))],
        out_shape=[jax.ShapeDtypeStruct((n_tok, D_MODEL), F32),
                   jax.ShapeDtypeStruct((n_tok * ROW_TILE, LANES), F32),
                   jax.ShapeDtypeStruct((N_EXPERTS, n_tok), F32)],
        compiler_params=_params("arbitrary"),
        name="post_attn",
    )(o_ctx, o_lat, x, wo_bf16, mod3, norm_g.reshape(1, D_MODEL), mod3, mod3, wr_hi, wr_lo)


def _group_prefix(mask, tri):
    local, offs = [], []
    off = jnp.zeros((mask.shape[0], 1), F32)
    for g in range(mask.shape[1] // GROUP):
        xg = mask[:, g * GROUP:(g + 1) * GROUP]
        offs.append(off)
        local.append(_dot(xg.astype(BF16), tri))
        off = off + jnp.sum(xg, axis=1, keepdims=True)
    offs.append(off)
    return local, offs


def _split3_bf16(x):
    hi = x.astype(BF16)
    r1 = x - hi.astype(F32)
    mid = r1.astype(BF16)
    lo = (r1 - mid.astype(F32)).astype(BF16)
    return hi, mid, lo


def _plan_kernel(lg_ref, tri_ref, idx_ref, gc_ref, cnt_ref, affg_ref):
    lg = lg_ref[...]
    ex = jnp.exp(lg - jnp.max(lg, axis=0, keepdims=True))
    aff = ex / jnp.sum(ex, axis=0, keepdims=True)

    def count_ge(v):
        return jnp.sum(jnp.where(aff >= v, 1.0, 0.0), axis=1, keepdims=True)

    def search(i, thr):
        cand = thr | jnp.left_shift(jnp.int32(1), 30 - i)
        ok = (count_ge(lax.bitcast_convert_type(cand, F32)) >= CAP) & (cand >= F32_MIN_NORMAL_BITS)
        return jnp.where(ok, cand, thr)

    thr = lax.fori_loop(0, 31, search, jnp.zeros((N_EXPERTS, 1), I32))
    lo = lax.bitcast_convert_type(thr, F32)
    hi = lax.bitcast_convert_type(jnp.maximum(thr + 1, F32_MIN_NORMAL_BITS), F32)

    def refine(i, bounds):
        lo, hi = bounds
        mid = lo + (hi - lo) * 0.5
        ok = count_ge(mid) >= CAP
        return jnp.where(ok, mid, lo), jnp.where(ok, hi, mid)

    lo, hi = lax.fori_loop(0, 32, refine, (lo, hi))
    tri = tri_ref[...]
    gt = aff >= hi
    eq = jnp.where((aff >= lo) & (aff < hi), 1.0, 0.0)
    need = CAP - jnp.sum(jnp.where(gt, 1.0, 0.0), axis=1, keepdims=True)
    eq_local, eq_offs = _group_prefix(eq, tri)
    eq_rank = jnp.concatenate([eq_local[g] + eq_offs[g] for g in range(N_GROUPS)], axis=1)
    sel = jnp.where(gt | ((eq > 0.0) & (eq_rank < need)), 1.0, 0.0)
    sel_local, offs = _group_prefix(sel, tri)
    for g in range(N_GROUPS):
        sl = slice(g * GROUP, (g + 1) * GROUP)
        cnt_ref[g] = sel_local[g] + sel[:, sl]
        affg_ref[g] = aff[:, sl]

    lane = lax.broadcasted_iota(I32, (N_EXPERTS, LANES), 1)
    never = jnp.full((N_EXPERTS, LANES), 2.0 * N_TOK, F32)
    grp_lo, grp_hi = never, never
    for g in range(N_GROUPS):
        grp_lo = jnp.where(lane == g, offs[g], grp_lo)
        grp_hi = jnp.where(lane == g, offs[g + 1], grp_hi)
    row = lax.broadcasted_iota(I32, (CAP, LANES), 0).astype(F32)
    in_group_lane = lax.broadcasted_iota(I32, (CAP, GROUP), 1).astype(F32)
    zpad = jnp.zeros((LANES - N_GROUPS, GROUP), BF16)
    for e in range(N_EXPERTS):
        lo_row, hi_row = grp_lo[e:e + 1, :], grp_hi[e:e + 1, :]
        in_grp = (lo_row <= row) & (row < hi_row)
        onehot = jnp.where(in_grp, 1.0, 0.0).astype(BF16)
        counts = _dot(onehot, jnp.concatenate([cnt_ref[:, e, :].astype(BF16), zpad], axis=0))
        rank = row[:, 0:1] - jnp.sum(jnp.where(in_grp, lo_row, 0.0), axis=1, keepdims=True)
        local = jnp.sum(jnp.where(counts <= rank, 1.0, 0.0), axis=1, keepdims=True)
        grp = jnp.sum(jnp.where(hi_row <= row, 1.0, 0.0), axis=1, keepdims=True)
        tok = grp * GROUP + local
        aff_rows = sum(_dot(onehot, jnp.concatenate([part, zpad], axis=0))
                       for part in _split3_bf16(affg_ref[:, e, :]))
        gate = jnp.sum(jnp.where(in_group_lane == local, aff_rows, 0.0), axis=1, keepdims=True)
        gc_ref[0, e] = jnp.broadcast_to(gate, (CAP, LANES))
        tok_b = jnp.broadcast_to(tok, (CAP, LANES))
        tok_row = jnp.concatenate([tok_b[t * LANES:(t + 1) * LANES, :].T[0:1, :]
                                   for t in range(CAP // LANES)], axis=1)
        idx_ref[0, e:e + 1, :] = tok_row.astype(I32)


def _plan(logits_t, tri):
    ns = N_STREAMS
    return pl.pallas_call(
        _plan_kernel,
        grid=(ns,),
        in_specs=[pl.BlockSpec((N_EXPERTS, N_TOK), lambda s: (0, s)),
                  pl.BlockSpec((MXU_DIM, MXU_DIM), lambda s: (0, 0))],
        out_specs=[pl.BlockSpec((1, N_EXPERTS, CAP), lambda s: (s, 0, 0)),
                   pl.BlockSpec((1, N_EXPERTS, CAP, LANES), lambda s: (s, 0, 0, 0))],
        out_shape=[jax.ShapeDtypeStruct((ns, N_EXPERTS, CAP), I32),
                   jax.ShapeDtypeStruct((ns, N_EXPERTS, CAP, LANES), F32)],
        scratch_shapes=[pltpu.VMEM((N_GROUPS, N_EXPERTS, GROUP), F32),
                        pltpu.VMEM((N_GROUPS, N_EXPERTS, GROUP), F32)],
        compiler_params=_params("arbitrary"),
        name="plan",
    )(logits_t, tri)


def _tile_rows(r):
    return pl.ds(pl.multiple_of(r * ROW_TILE, ROW_TILE), ROW_TILE)


def _gather_kernel(idx_ref, h_ref, xe_ref, stage_ref):
    s, e = pl.program_id(0), pl.program_id(1)
    base = (s * N_EXPERTS + e) * CAP

    def move(g, carry):
        for u in range(MOVE_UNROLL):
            r = g * MOVE_UNROLL + u
            stage_ref[_tile_rows(r), :] = h_ref[0, _tile_rows(idx_ref[base + r]), :]
        return carry

    lax.fori_loop(0, CAP // MOVE_UNROLL, move, 0)
    for k in range(ROW_TILE):
        xe_ref[0, 0, :, k * LANES:(k + 1) * LANES] = (
            stage_ref[pl.ds(k, CAP, stride=ROW_TILE), :].astype(BF16))


def _gather(idx_flat, h2_tiles):
    ns = h2_tiles.shape[0]
    return pl.pallas_call(
        _gather_kernel,
        grid_spec=pltpu.PrefetchScalarGridSpec(
            num_scalar_prefetch=1,
            grid=(ns, N_EXPERTS),
            in_specs=[pl.BlockSpec((1, N_TOK * ROW_TILE, LANES), lambda s, e, idx: (s, 0, 0),
                                   pipeline_mode=pl.Buffered(1))],
            out_specs=pl.BlockSpec((1, 1, CAP, D_MODEL), lambda s, e, idx: (s, e, 0, 0)),
            scratch_shapes=[pltpu.VMEM((CAP * ROW_TILE, LANES), F32)],
        ),
        out_shape=jax.ShapeDtypeStruct((ns, N_EXPERTS, CAP, D_MODEL), BF16),
        compiler_params=_params("arbitrary", "arbitrary"),
        name="moe_gather",
    )(idx_flat, h2_tiles)


def _ffn_kernel(x_ref, wg_ref, wu_ref, wd_ref, gc_ref, ye_ref, acc_ref, *, row_tile):
    f = pl.program_id(1)
    ns = x_ref.shape[0]

    @pl.when(f == 0)
    def _():
        acc_ref[...] = jnp.zeros_like(acc_ref)

    wg = wg_ref[0, 0].astype(BF16)
    wu = wu_ref[0, 0].astype(BF16)
    wd = wd_ref[0, 0].astype(BF16)
    for s in range(ns):
        for r0 in range(0, CAP, row_tile):
            x = x_ref[s, 0, r0:r0 + row_tile, :]
            hid = _silu(_dot(x, wg)) * _dot(x, wu)
            acc_ref[s, r0:r0 + row_tile, :] += _dot(hid.astype(BF16), wd)

    @pl.when(f == pl.num_programs(1) - 1)
    def _():
        for s in range(ns):
            ye = acc_ref[s] * gc_ref[s, 0, :, 0:1]
            for k in range(ROW_TILE):
                ye_ref[s, 0, pl.ds(k, CAP, stride=ROW_TILE), :] = ye[:, k * LANES:(k + 1) * LANES]


def _ffn(xe, gc, w_gate, w_up, w_down, layer):
    ns = xe.shape[0]
    return pl.pallas_call(
        functools.partial(_ffn_kernel, row_tile=512),
        grid=(N_EXPERTS, EXPERT_FF // FF_TILE),
        in_specs=[
            pl.BlockSpec((ns, 1, CAP, D_MODEL), lambda e, f: (0, e, 0, 0)),
            pl.BlockSpec((1, 1, D_MODEL, FF_TILE), lambda e, f: (layer, e, 0, f)),
            pl.BlockSpec((1, 1, D_MODEL, FF_TILE), lambda e, f: (layer, e, 0, f)),
            pl.BlockSpec((1, 1, FF_TILE, D_MODEL), lambda e, f: (layer, e, f, 0)),
            pl.BlockSpec((ns, 1, CAP, LANES), lambda e, f: (0, e, 0, 0)),
        ],
        out_specs=pl.BlockSpec((ns, 1, CAP * ROW_TILE, LANES), lambda e, f: (0, e, 0, 0)),
        out_shape=jax.ShapeDtypeStruct((ns, N_EXPERTS, CAP * ROW_TILE, LANES), F32),
        scratch_shapes=[pltpu.VMEM((ns, CAP, D_MODEL), F32)],
        compiler_params=_params("arbitrary", "arbitrary"),
        name="moe_ffn",
    )(xe, w_gate, w_up, w_down, gc)


def _scatter_kernel(idx_ref, ye_ref, y_ref, gate_ref, *refs, split):
    out_refs, acc_ref = refs[:-1], refs[-1]
    s, j = pl.program_id(0), pl.program_id(1)

    @pl.when(j == 0)
    def _():
        acc_ref[...] = jnp.zeros_like(acc_ref)

    @pl.when(j < N_EXPERTS)
    def _():
        base = (s * N_EXPERTS + j) * CAP

        def add_rows(g, carry):
            r0 = g * MOVE_UNROLL
            dst = [_tile_rows(idx_ref[base + r0 + u]) for u in range(MOVE_UNROLL)]
            val = [acc_ref[dst[u], :] + ye_ref[0, 0, _tile_rows(r0 + u), :] for u in range(MOVE_UNROLL)]
            for u in range(MOVE_UNROLL):
                acc_ref[dst[u], :] = val[u]
            return carry

        lax.fori_loop(0, CAP // MOVE_UNROLL, add_rows, 0)

    @pl.when(j >= N_EXPERTS)
    def _():
        c = j - N_EXPERTS
        rows = FIN_TOK * ROW_TILE
        part = acc_ref.at[pl.ds(pl.multiple_of(c * rows, rows), rows), :]
        moe = jnp.concatenate([part[pl.ds(k, FIN_TOK, stride=ROW_TILE), :] for k in range(ROW_TILE)], axis=1)
        res = y_ref[...] + gate_ref[0] * moe
        if split:
            for si, out_ref in enumerate(out_refs):
                @pl.when(s == si)
                def _(out_ref=out_ref):
                    out_ref[...] = res
        else:
            out_refs[0][...] = res


def _scatter(idx_flat, ye_tiles, y, mod3, layer, lat_seq, split):
    ns = ye_tiles.shape[0]
    fin = lambda j: jnp.maximum(j - N_EXPERTS, 0)

    def gate_index(s, j, idx):
        row = _mod_row(s * N_FIN + fin(j), FIN_TOK, lat_seq)
        return ((layer * N_MOD_ROWS + row) * 6 + 5, 0, 0)

    chunk = (FIN_TOK, D_MODEL)
    if split:
        out_specs = [
            pl.BlockSpec(chunk, lambda s, j, idx, si=si: (
                jnp.where(s == si, fin(j), jnp.where(s < si, 0, N_FIN - 1)), 0))
            for si in range(ns)]
        out_shape = [jax.ShapeDtypeStruct((N_TOK, D_MODEL), F32)] * ns
    else:
        out_specs = pl.BlockSpec(chunk, lambda s, j, idx: (s * N_FIN + fin(j), 0))
        out_shape = jax.ShapeDtypeStruct(y.shape, F32)
    return pl.pallas_call(
        functools.partial(_scatter_kernel, split=split),
        grid_spec=pltpu.PrefetchScalarGridSpec(
            num_scalar_prefetch=1,
            grid=(ns, N_EXPERTS + N_FIN),
            in_specs=[
                pl.BlockSpec((1, 1, CAP * ROW_TILE, LANES),
                             lambda s, j, idx: (s, jnp.minimum(j, N_EXPERTS - 1), 0, 0)),
                pl.BlockSpec(chunk, lambda s, j, idx: (s * N_FIN + fin(j), 0)),
                pl.BlockSpec((1, 1, D_MODEL), gate_index),
            ],
            out_specs=out_specs,
            scratch_shapes=[pltpu.VMEM((N_TOK * ROW_TILE, LANES), F32)],
        ),
        out_shape=out_shape,
        compiler_params=_params("arbitrary", "arbitrary"),
        name="moe_scatter",
    )(idx_flat, ye_tiles, y, mod3)


def _rope_tables(n_tokens):
    t = np.arange(n_tokens)
    row = (t // GRID_W).astype(np.float32)
    col = (t % GRID_W).astype(np.float32)
    pairs = HEAD_DIM // 4
    inv_freq = ROPE_THETA ** (-jnp.arange(pairs, dtype=F32) / pairs)
    ang = jnp.concatenate([row[:, None] * inv_freq, col[:, None] * inv_freq], axis=-1)
    cos, sin = jnp.cos(ang), jnp.sin(ang)
    reps = MXU_DIM // HEAD_DIM
    cos_t = jnp.tile(jnp.concatenate([cos, cos], axis=-1), (1, reps))
    sin_t = jnp.tile(jnp.concatenate([-sin, sin], axis=-1), (1, reps))
    return (jnp.concatenate([cos_t, jnp.ones((TM, MXU_DIM), F32)], axis=0),
            jnp.concatenate([sin_t, jnp.zeros((TM, MXU_DIM), F32)], axis=0))


def _dup_cache(cache):
    b, s, hk, hd = cache.shape
    return jnp.broadcast_to(cache[:, :, :, None, :], (b, s, hk, 2, hd)).reshape(b, s, 2 * hk * hd).astype(BF16)


def kernel(x_prompt, x_sample, cache_attn_k, cache_attn_v, cache_na_k, cache_na_v, c, c_ctx,
           norm1_g, norm2_g, w_ada, b_ada, attn_w_qkv, attn_q_gain, attn_k_gain, attn_w_o,
           na_w_qkv, na_q_gain, na_k_gain, na_rpb, na_w_o,
           moe_w_router, moe_w_gate, moe_w_up, moe_w_down):
    bc, tc, _ = x_prompt.shape
    bl, tl, _ = x_sample.shape
    depth = w_ada.shape[0]
    assert bc * tc == N_TOK and bl * tl == N_TOK and 1 + bl <= N_MOD_ROWS
    assert tl % TM == 0 and tl % FIN_TOK == 0
    n_all = N_STREAMS * N_TOK

    cond = jnp.zeros((N_MOD_ROWS, D_MODEL), F32).at[0].set(c_ctx).at[1:1 + bl].set(c)
    mod3 = _ada(cond, w_ada, b_ada).reshape(depth * N_MOD_ROWS * 6, 1, D_MODEL)

    seg = jnp.asarray(np.kron(np.eye(MXU_DIM // HEAD_DIM), np.ones((HEAD_DIM, HEAD_DIM))), BF16)
    tri = jnp.asarray(np.triu(np.ones((MXU_DIM, MXU_DIM)), k=1), BF16)
    rope_tables = _rope_tables(tl)

    y = jnp.concatenate([x_prompt.reshape(N_TOK, D_MODEL), x_sample.reshape(N_TOK, D_MODEL)], axis=0)
    new_k, new_v = [], []
    for i in range(depth):
        j = i // 2
        gqa = i % 2 == 0
        if gqa:
            w_qkv, q_gain, k_gain, w_o = attn_w_qkv[j], attn_q_gain[j], attn_k_gain[j], attn_w_o[j]
            kv_heads = N_KV_HEADS
        else:
            w_qkv, q_gain, k_gain, w_o = na_w_qkv[j], na_q_gain[j], na_k_gain[j], na_w_o[j]
            kv_heads = N_HEADS
        wr_hi, wr_lo = _split_bf16(moe_w_router[i].T)

        q, kb, vb, kf, vf = _qkv(y, norm1_g[i], mod3, i, tl, w_qkv.astype(BF16), q_gain, k_gain, seg,
                                 kv_heads * HEAD_DIM, rope_tables if gqa else None)
        kvb = kb.shape[1]
        o_ctx = _attention(q.reshape(n_all // tc, tc, D_MODEL), kb.reshape(n_all // tc, tc, kvb),
                           vb.reshape(n_all // tc, tc, kvb), bc, 0)
        q_l, kb_l, vb_l = (a.reshape(n_all // tl, tl, a.shape[1]) for a in (q, kb, vb))
        if gqa:
            o_lat = _attention(q_l, kb_l, vb_l, bl, N_TOK // tl,
                               _dup_cache(cache_attn_k[:, j]), _dup_cache(cache_attn_v[:, j]))
        else:
            past = cache_na_k.shape[2]
            o_lat = _na_attention(q_l, kb_l, vb_l, bl, N_TOK // tl,
                                  cache_na_k[:, j].reshape(bl, past, D_MODEL).astype(BF16),
                                  cache_na_v[:, j].reshape(bl, past, D_MODEL).astype(BF16),
                                  _na_bias_table(na_rpb[j]))
        new_k.append(kf[:N_TOK].reshape(bc, 1, tc, kv_heads, HEAD_DIM))
        new_v.append(vf[:N_TOK].reshape(bc, 1, tc, kv_heads, HEAD_DIM))

        y, h2, logits_t = _post_attention(o_ctx.reshape(N_TOK, D_MODEL), o_lat.reshape(N_TOK, D_MODEL), y,
                                          w_o.astype(BF16), norm2_g[i], mod3, i, tl, wr_hi, wr_lo)
        idx, gc = _plan(logits_t, tri)
        idx_flat = idx.reshape(-1)
        xe = _gather(idx_flat, h2.reshape(N_STREAMS, N_TOK * ROW_TILE, LANES))
        ye = _ffn(xe, gc, moe_w_gate, moe_w_up, moe_w_down, i)
        y = _scatter(idx_flat, ye, y, mod3, i, tl, split=(i == depth - 1))

    y_ctx, y_lat = y
    return (y_ctx.reshape(bc, tc, D_MODEL), y_lat.reshape(bl, tl, D_MODEL),
            jnp.concatenate(new_k[0::2], axis=1), jnp.concatenate(new_v[0::2], axis=1),
            jnp.concatenate(new_k[1::2], axis=1), jnp.concatenate(new_v[1::2], axis=1))
```

```python
import functools

import jax
import jax.numpy as jnp
import numpy as np
from jax import lax
from jax.experimental import pallas as pl
from jax.experimental.pallas import tpu as pltpu

F32 = jnp.float32
BF16 = jnp.bfloat16
I32 = jnp.int32

D_MODEL = 1024
N_HEADS = 16
N_KV_HEADS = 4
HEAD_DIM = 64
GRID_W = 64
WIN_R = 8
WIN_C = 16
N_EXPERTS = 16
EXPERT_FF = 2048
ROPE_THETA = 10000.0
RMS_EPS = 1e-6
NEG_INF = -1e30
F32_MIN_NORMAL_BITS = 0x00800000

LANES = 128
MXU_DIM = 256
VMEM_LIMIT = 56 * 1024 * 1024

N_STREAMS = 2
N_TOK = 8192
CAP = 2 * N_TOK // N_EXPERTS
TM = 512
N_MOD_ROWS = 16
ROW_TILE = D_MODEL // LANES
GROUP = MXU_DIM
N_GROUPS = N_TOK // GROUP
FIN_TOK = 512
N_FIN = N_TOK // FIN_TOK
MOVE_UNROLL = 8
FF_TILE = 512
NA_BAND = 10
NA_QBLK = 128
N_REL_R = 2 * WIN_R - 1
N_REL_C = 2 * WIN_C - 1


def _params(*sem):
    return pltpu.CompilerParams(dimension_semantics=sem, vmem_limit_bytes=VMEM_LIMIT)


def _dot(a, b):
    return jnp.dot(a, b, preferred_element_type=F32)


def _dot_nt(a, b):
    return lax.dot_general(a, b, (((1,), (1,)), ((), ())), preferred_element_type=F32)


def _split_bf16(x):
    hi = x.astype(BF16)
    lo = (x - hi.astype(F32)).astype(BF16)
    return hi, lo


def _silu(x):
    return x * (1.0 / (1.0 + jnp.exp(-x)))


def _ada_kernel(cond_ref, w_ref, b_ref, out_ref):
    sx = _silu(cond_ref[...])
    xh, xl = _split_bf16(sx)
    wh, wl = _split_bf16(w_ref[0])
    out_ref[0] = _dot(xh, wh) + _dot(xl, wh) + _dot(xh, wl) + b_ref[0]


def _ada(cond, w_ada, b_ada):
    depth = w_ada.shape[0]
    tn = 1024
    n_out = w_ada.shape[2]
    return pl.pallas_call(
        _ada_kernel,
        grid=(depth, n_out // tn),
        in_specs=[
            pl.BlockSpec((N_MOD_ROWS, D_MODEL), lambda l, n: (0, 0)),
            pl.BlockSpec((1, D_MODEL, tn), lambda l, n: (l, 0, n)),
            pl.BlockSpec((1, 1, tn), lambda l, n: (l, 0, n)),
        ],
        out_specs=pl.BlockSpec((1, N_MOD_ROWS, tn), lambda l, n: (l, 0, n)),
        out_shape=jax.ShapeDtypeStruct((depth, N_MOD_ROWS, n_out), F32),
        compiler_params=_params("arbitrary", "arbitrary"),
        name="ada",
    )(cond, w_ada, b_ada.reshape(depth, 1, n_out))


def _mod_row(tile, tile_rows, lat_seq):
    ctx_tiles = N_TOK // tile_rows
    return jnp.where(tile < ctx_tiles, 0, 1 + (tile - ctx_tiles) // (lat_seq // tile_rows))


def _mod_spec(layer, which, lat_seq, tile_of=lambda i: i):
    def index(i):
        return ((layer * N_MOD_ROWS + _mod_row(tile_of(i), TM, lat_seq)) * 6 + which, 0, 0)
    return pl.BlockSpec((1, 1, D_MODEL), index)


CTX_TILES = N_TOK // TM


def _x_specs(x_ctx_block0, x_lat_block0, tile_of=lambda i: i):
    return [
        pl.BlockSpec((TM, D_MODEL), lambda i: (x_ctx_block0 + jnp.minimum(tile_of(i), CTX_TILES - 1), 0)),
        pl.BlockSpec((TM, D_MODEL), lambda i: (x_lat_block0 + jnp.maximum(tile_of(i) - CTX_TILES, 0), 0)),
    ]


def _head_norm(z, seg, gain):
    ss = _dot((z * z).astype(BF16), seg)
    return z * lax.rsqrt(ss * (1.0 / HEAD_DIM) + RMS_EPS) * gain


def _rope(z, cos_t, sin_t):
    lane = lax.broadcasted_iota(I32, z.shape, 1)
    first = (lane & 32) == 0
    n = z.shape[1]
    partner = jnp.where(first, pltpu.roll(z, n - 32, axis=1), pltpu.roll(z, 32, axis=1))
    return z * cos_t + partner * sin_t


def _dup_heads(z):
    outs = []
    for b in range(z.shape[1] // LANES):
        x = z[:, b * LANES:(b + 1) * LANES]
        xr = pltpu.roll(x, HEAD_DIM, axis=1)
        lo = lax.broadcasted_iota(I32, x.shape, 1) < HEAD_DIM
        outs.append(jnp.where(lo, x, xr))
        outs.append(jnp.where(lo, xr, x))
    return jnp.concatenate(outs, axis=1)


def _qkv_kernel(*refs, kv_width, rope):
    it = iter(refs)
    xc_ref, xl_ref, g_ref, shift_ref, scale_ref, w_ref, qg_ref, kg_ref, seg_ref = (next(it) for _ in range(9))
    cos_ref = sin_ref = None
    if rope:
        cos_ref, sin_ref = next(it), next(it)
    q_ref, kb_ref, vb_ref, kf_ref, vf_ref = (next(it) for _ in range(5))
    dup = kv_width == N_KV_HEADS * HEAD_DIM

    x = jnp.where(_qkv_tile(pl.program_id(0)) < CTX_TILES, xc_ref[...], xl_ref[...])
    ms = jnp.mean(x * x, axis=-1, keepdims=True)
    h = x * lax.rsqrt(ms + RMS_EPS) * g_ref[...]
    h = h * (1.0 + scale_ref[0]) + shift_ref[0]
    hb = h.astype(BF16)
    seg = seg_ref[...]
    if rope:
        cos_t, sin_t = cos_ref[...], sin_ref[...]

    q_width = N_HEADS * HEAD_DIM
    for cidx in range(q_width // MXU_DIM):
        c0 = cidx * MXU_DIM
        z = _head_norm(_dot(hb, w_ref[:, c0:c0 + MXU_DIM]), seg, qg_ref[...])
        if rope:
            z = _rope(z, cos_t, sin_t)
        q_ref[:, c0:c0 + MXU_DIM] = (z * (HEAD_DIM ** -0.5)).astype(BF16)
    for cidx in range(kv_width // MXU_DIM):
        c0 = cidx * MXU_DIM
        z = _head_norm(_dot(hb, w_ref[:, q_width + c0:q_width + c0 + MXU_DIM]), seg, kg_ref[...])
        kf_ref[:, c0:c0 + MXU_DIM] = z
        if rope:
            z = _rope(z, cos_t, sin_t)
        v = _dot(hb, w_ref[:, q_width + kv_width + c0:q_width + kv_width + c0 + MXU_DIM])
        vf_ref[:, c0:c0 + MXU_DIM] = v
        if dup:
            kb_ref[:, 2 * c0:2 * c0 + 2 * MXU_DIM] = _dup_heads(z).astype(BF16)
            vb_ref[:, 2 * c0:2 * c0 + 2 * MXU_DIM] = _dup_heads(v).astype(BF16)
        else:
            kb_ref[:, c0:c0 + MXU_DIM] = z.astype(BF16)
            vb_ref[:, c0:c0 + MXU_DIM] = v.astype(BF16)


def _qkv_tile(i):
    return (i + CTX_TILES) % (N_STREAMS * CTX_TILES)


def _qkv(x_ctx, x_lat, norm_g, mod3, layer, lat_seq, w_bf16, q_gain, k_gain, seg, kv_width, rope_tables):
    n_tok = N_STREAMS * N_TOK
    n_w = w_bf16.shape[1]
    kvb_width = 2 * kv_width if kv_width == N_KV_HEADS * HEAD_DIM else kv_width
    tile = lambda w: pl.BlockSpec((TM, w), lambda i: (_qkv_tile(i), 0))
    ctx_tile = lambda w: pl.BlockSpec((TM, w), lambda i: (jnp.maximum(i - CTX_TILES, 0), 0))
    const = lambda shape: pl.BlockSpec(shape, lambda i: (0,) * len(shape))
    in_specs = _x_specs(x_ctx[1], x_lat[1], _qkv_tile) + [
        const((1, D_MODEL)),
        _mod_spec(layer, 0, lat_seq, _qkv_tile),
        _mod_spec(layer, 1, lat_seq, _qkv_tile),
        const((D_MODEL, n_w)),
        const((1, MXU_DIM)),
        const((1, MXU_DIM)),
        const((MXU_DIM, MXU_DIM)),
    ]
    args = [x_ctx[0], x_lat[0], norm_g.reshape(1, D_MODEL), mod3, mod3, w_bf16,
            jnp.tile(q_gain, MXU_DIM // HEAD_DIM).reshape(1, MXU_DIM),
            jnp.tile(k_gain, MXU_DIM // HEAD_DIM).reshape(1, MXU_DIM), seg]
    if rope_tables is not None:
        seq_tiles = lat_seq // TM
        rope_index = lambda i: (jnp.where(_qkv_tile(i) < CTX_TILES, seq_tiles, _qkv_tile(i) % seq_tiles), 0)
        in_specs += [pl.BlockSpec((TM, MXU_DIM), rope_index)] * 2
        args += list(rope_tables)
    out_specs = [tile(D_MODEL), tile(kvb_width), tile(kvb_width), ctx_tile(kv_width), ctx_tile(kv_width)]
    out_shape = [jax.ShapeDtypeStruct((n_tok, D_MODEL), BF16),
                 jax.ShapeDtypeStruct((n_tok, kvb_width), BF16),
                 jax.ShapeDtypeStruct((n_tok, kvb_width), BF16),
                 jax.ShapeDtypeStruct((N_TOK, kv_width), F32),
                 jax.ShapeDtypeStruct((N_TOK, kv_width), F32)]
    return pl.pallas_call(
        functools.partial(_qkv_kernel, kv_width=kv_width, rope=rope_tables is not None),
        grid=(n_tok // TM,),
        in_specs=in_specs,
        out_specs=out_specs,
        out_shape=out_shape,
        compiler_params=_params("arbitrary"),
        name="qkv",
    )(*args)


def _two_head_rows(qj):
    lo = lax.broadcasted_iota(I32, qj.shape, 1) < HEAD_DIM
    zero = jnp.zeros_like(qj)
    return jnp.concatenate([jnp.where(lo, qj, zero), jnp.where(lo, zero, qj)], axis=0)


def _merge_two_heads(r):
    tq = r.shape[0] // 2
    lo = lax.broadcasted_iota(I32, (tq, LANES), 1) < HEAD_DIM
    return jnp.where(lo, r[:tq], r[tq:])


def _attn_kernel(*refs, n_kv_blocks, has_ctx):
    if has_ctx:
        q_ref, k_ref, v_ref, ck_ref, cv_ref, o_ref = refs
    else:
        q_ref, k_ref, v_ref, o_ref = refs
    n_q_blocks = D_MODEL // LANES
    for j in range(n_q_blocks):
        kb = (j * n_kv_blocks) // n_q_blocks
        ksl = slice(kb * LANES, (kb + 1) * LANES)
        q2 = _two_head_rows(q_ref[0, :, j * LANES:(j + 1) * LANES])
        s = _dot_nt(q2, k_ref[0, :, ksl])
        m = jnp.max(s, axis=-1, keepdims=True)
        if has_ctx:
            sc = _dot_nt(q2, ck_ref[0, :, ksl])
            m = jnp.maximum(m, jnp.max(sc, axis=-1, keepdims=True))
        p = jnp.exp(s - m)
        l = jnp.sum(p, axis=-1, keepdims=True)
        r = _dot(p.astype(BF16), v_ref[0, :, ksl])
        if has_ctx:
            pc = jnp.exp(sc - m)
            l = l + jnp.sum(pc, axis=-1, keepdims=True)
            r = r + _dot(pc.astype(BF16), cv_ref[0, :, ksl])
        r = r / l
        o_ref[0, :, j * LANES:(j + 1) * LANES] = _merge_two_heads(r).astype(BF16)


def _attention(q, k, v, n_batch, batch0, ctx_k=None, ctx_v=None, tq=256):
    t = q.shape[1]
    s, w = k.shape[1], k.shape[2]
    has_ctx = ctx_k is not None
    in_specs = [
        pl.BlockSpec((1, tq, D_MODEL), lambda bi, qi: (bi + batch0, qi, 0)),
        pl.BlockSpec((1, s, w), lambda bi, qi: (bi + batch0, 0, 0)),
        pl.BlockSpec((1, s, w), lambda bi, qi: (bi + batch0, 0, 0)),
    ]
    args = [q, k, v]
    if has_ctx:
        sc = ctx_k.shape[1]
        in_specs += [pl.BlockSpec((1, sc, w), lambda bi, qi: (bi, 0, 0))] * 2
        args += [ctx_k, ctx_v]
    return pl.pallas_call(
        functools.partial(_attn_kernel, n_kv_blocks=w // LANES, has_ctx=has_ctx),
        grid=(n_batch, t // tq),
        in_specs=in_specs,
        out_specs=pl.BlockSpec((1, tq, D_MODEL), lambda bi, qi: (bi, qi, 0)),
        out_shape=jax.ShapeDtypeStruct((n_batch, t, D_MODEL), BF16),
        compiler_params=_params("arbitrary", "arbitrary"),
        name="attn",
    )(*args)


def _na_band_start(blk, rows):
    row_start = jnp.clip(2 * blk - WIN_R // 2, 0, rows - WIN_R)
    return jnp.minimum(row_start, rows - NA_BAND)


def _na_kernel(q_ref, k_ref, v_ref, ck_ref, cv_ref, bias_ref, o_ref, *, rows):
    blk = pl.program_id(1)
    band_start = _na_band_start(blk, rows)
    n_band = NA_BAND * GRID_W
    band = pl.ds(pl.multiple_of(band_start * GRID_W, LANES), n_band)
    shape = (NA_QBLK, n_band)
    q_r = 2 * blk + lax.broadcasted_iota(I32, shape, 0) // GRID_W
    k_r = band_start + lax.broadcasted_iota(I32, shape, 1) // GRID_W
    q_rs = jnp.clip(q_r - WIN_R // 2, 0, rows - WIN_R)
    row_ok1 = (k_r >= q_rs) & (k_r < q_rs + WIN_R)
    row_ok = jnp.concatenate([row_ok1, row_ok1], axis=0)
    pair_shift = (band_start - 2 * blk + WIN_R) // 2
    for j in range(D_MODEL // LANES):
        sl = slice(j * LANES, (j + 1) * LANES)
        q2 = _two_head_rows(q_ref[0, :, sl])
        s = _dot_nt(q2, k_ref[0, band, sl])
        bias = jnp.concatenate(
            [jnp.concatenate([bias_ref[2 * j + hh, pair_shift + m] for m in range(NA_BAND // 2)], axis=1)
             for hh in range(2)], axis=0)
        s = jnp.where(row_ok, s + bias, NEG_INF)
        sc = _dot_nt(q2, ck_ref[0, :, sl])
        m = jnp.maximum(jnp.max(s, axis=-1, keepdims=True), jnp.max(sc, axis=-1, keepdims=True))
        p = jnp.exp(s - m)
        pc = jnp.exp(sc - m)
        l = jnp.sum(p, axis=-1, keepdims=True) + jnp.sum(pc, axis=-1, keepdims=True)
        r = _dot(p.astype(BF16), v_ref[0, band, sl]) + _dot(pc.astype(BF16), cv_ref[0, :, sl])
        r = r / l
        o_ref[0, :, sl] = _merge_two_heads(r).astype(BF16)


def _na_bias_kernel(rpb_ref, out_ref, tz_ref):
    h = pl.program_id(0)
    shape = (GRID_W, LANES)
    q_col = lax.broadcasted_iota(I32, shape, 0)
    lane = lax.broadcasted_iota(I32, shape, 1)
    k_col = lane & (GRID_W - 1)
    col_start = jnp.clip(q_col - WIN_C // 2, 0, GRID_W - WIN_C)
    col_ok = (k_col >= col_start) & (k_col < col_start + WIN_C)
    rel_c = jnp.clip(k_col - q_col + WIN_C - 1, 0, N_REL_C - 1)
    for a in range(N_REL_R):
        acc = jnp.zeros(shape, F32)
        for b in range(N_REL_C):
            acc = jnp.where(rel_c == b, rpb_ref[(h * N_REL_R + a) * N_REL_C + b], acc)
        tz_ref[a] = jnp.where(col_ok, acc, NEG_INF)
    left = lane < GRID_W
    for dd in range(WIN_R + 1):
        for qr in range(2):
            rel = [min(max(2 * dd - WIN_R + kr - qr + WIN_R - 1, 0), N_REL_R - 1) for kr in range(2)]
            out_ref[0, dd, qr * GRID_W:(qr + 1) * GRID_W, :] = jnp.where(left, tz_ref[rel[0]], tz_ref[rel[1]])


def _na_bias_table(rpb):
    return pl.pallas_call(
        _na_bias_kernel,
        grid=(N_HEADS,),
        in_specs=[pl.BlockSpec(memory_space=pltpu.SMEM)],
        out_specs=pl.BlockSpec((1, WIN_R + 1, LANES, LANES), lambda h: (h, 0, 0, 0)),
        out_shape=jax.ShapeDtypeStruct((N_HEADS, WIN_R + 1, LANES, LANES), F32),
        scratch_shapes=[pltpu.VMEM((N_REL_R, GRID_W, LANES), F32)],
        compiler_params=_params("arbitrary"),
        name="na_bias",
    )(rpb.reshape(-1))


def _na_attention(q, k, v, n_batch, batch0, ctx_k, ctx_v, bias_tbl):
    t = q.shape[1]
    sc = ctx_k.shape[1]
    rows = t // GRID_W
    full = lambda n, off: pl.BlockSpec((1, n, D_MODEL), lambda bi, qi: (bi + off, 0, 0))
    return pl.pallas_call(
        functools.partial(_na_kernel, rows=rows),
        grid=(n_batch, t // NA_QBLK),
        in_specs=[
            pl.BlockSpec((1, NA_QBLK, D_MODEL), lambda bi, qi: (bi + batch0, qi, 0)),
            full(t, batch0), full(t, batch0), full(sc, 0), full(sc, 0),
            pl.BlockSpec(bias_tbl.shape, lambda bi, qi: (0, 0, 0, 0)),
        ],
        out_specs=pl.BlockSpec((1, NA_QBLK, D_MODEL), lambda bi, qi: (bi, qi, 0)),
        out_shape=jax.ShapeDtypeStruct((n_batch, t, D_MODEL), BF16),
        compiler_params=_params("arbitrary", "arbitrary"),
        name="na_attn",
    )(q, k, v, ctx_k, ctx_v, bias_tbl)


def _post_kernel(oc_ref, ol_ref, xc_ref, xl_ref, wo_ref, gate_ref, g_ref, shift_ref, scale_ref,
                 wrh_ref, wrl_ref, y_ref, h_ref, lg_ref):
    is_ctx = pl.program_id(0) < CTX_TILES
    o = jnp.where(is_ctx, oc_ref[...], ol_ref[...])
    x = jnp.where(is_ctx, xc_ref[...], xl_ref[...])
    y = x + gate_ref[0] * _dot(o, wo_ref[...])
    y_ref[...] = y
    ms = jnp.mean(y * y, axis=-1, keepdims=True)
    h = y * lax.rsqrt(ms + RMS_EPS) * g_ref[...]
    h = h * (1.0 + scale_ref[0]) + shift_ref[0]
    for k in range(ROW_TILE):
        h_ref[pl.ds(k, TM, stride=ROW_TILE), :] = h[:, k * LANES:(k + 1) * LANES]
    hh, hl = _split_bf16(h)
    lg_ref[...] = _dot_nt(wrh_ref[...], hh) + _dot_nt(wrh_ref[...], hl) + _dot_nt(wrl_ref[...], hh)


def _post_attention(o_ctx, o_lat, x_ctx, x_lat, wo_bf16, norm_g, mod3, layer, lat_seq, wr_hi, wr_lo):
    n_tok = N_STREAMS * N_TOK
    tile = lambda w: pl.BlockSpec((TM, w), lambda i: (i, 0))
    const = lambda shape: pl.BlockSpec(shape, lambda i: (0,) * len(shape))
    return pl.pallas_call(
        _post_kernel,
        grid=(n_tok // TM,),
        in_specs=_x_specs(0, 0) + _x_specs(x_ctx[1], x_lat[1]) + [
            const((D_MODEL, D_MODEL)),
            _mod_spec(layer, 2, lat_seq),
            const((1, D_MODEL)),
            _mod_spec(layer, 3, lat_seq),
            _mod_spec(layer, 4, lat_seq),
            const((N_EXPERTS, D_MODEL)), const((N_EXPERTS, D_MODEL)),
        ],
        out_specs=[tile(D_MODEL), pl.BlockSpec((TM * ROW_TILE, LANES), lambda i: (i, 0)),
                   pl.BlockSpec((N_EXPERTS, TM), lambda i: (0, i))],
        out_shape=[jax.ShapeDtypeStruct((n_tok, D_MODEL), F32),
                   jax.ShapeDtypeStruct((n_tok * ROW_TILE, LANES), F32),
                   jax.ShapeDtypeStruct((N_EXPERTS, n_tok), F32)],
        compiler_params=_params("arbitrary"),
        name="post_attn",
    )(o_ctx, o_lat, x_ctx[0], x_lat[0], wo_bf16, mod3, norm_g.reshape(1, D_MODEL), mod3, mod3, wr_hi, wr_lo)


def _group_prefix(mask, tri):
    local, offs = [], []
    off = jnp.zeros((mask.shape[0], 1), F32)
    for g in range(mask.shape[1] // GROUP):
        xg = mask[:, g * GROUP:(g + 1) * GROUP]
        offs.append(off)
        local.append(_dot(xg.astype(BF16), tri))
        off = off + jnp.sum(xg, axis=1, keepdims=True)
    offs.append(off)
    return local, offs


def _split3_bf16(x):
    hi = x.astype(BF16)
    r1 = x - hi.astype(F32)
    mid = r1.astype(BF16)
    lo = (r1 - mid.astype(F32)).astype(BF16)
    return hi, mid, lo


def _plan_kernel(lg_ref, tri_ref, idx_ref, gc_ref, cnt_ref, affg_ref):
    lg = lg_ref[...]
    ex = jnp.exp(lg - jnp.max(lg, axis=0, keepdims=True))
    aff = ex / jnp.sum(ex, axis=0, keepdims=True)

    def count_ge(v):
        return jnp.sum(jnp.where(aff >= v, 1.0, 0.0), axis=1, keepdims=True)

    def search(i, thr):
        cand = thr | jnp.left_shift(jnp.int32(1), 30 - i)
        ok = (count_ge(lax.bitcast_convert_type(cand, F32)) >= CAP) & (cand >= F32_MIN_NORMAL_BITS)
        return jnp.where(ok, cand, thr)

    thr = lax.fori_loop(0, 31, search, jnp.zeros((N_EXPERTS, 1), I32))
    lo = lax.bitcast_convert_type(thr, F32)
    hi = lax.bitcast_convert_type(jnp.maximum(thr + 1, F32_MIN_NORMAL_BITS), F32)

    def refine(i, bounds):
        lo, hi = bounds
        mid = lo + (hi - lo) * 0.5
        ok = count_ge(mid) >= CAP
        return jnp.where(ok, mid, lo), jnp.where(ok, hi, mid)

    lo, hi = lax.fori_loop(0, 32, refine, (lo, hi))
    tri = tri_ref[...]
    gt = aff >= hi
    eq = jnp.where((aff >= lo) & (aff < hi), 1.0, 0.0)
    need = CAP - jnp.sum(jnp.where(gt, 1.0, 0.0), axis=1, keepdims=True)
    eq_local, eq_offs = _group_prefix(eq, tri)
    eq_rank = jnp.concatenate([eq_local[g] + eq_offs[g] for g in range(N_GROUPS)], axis=1)
    sel = jnp.where(gt | ((eq > 0.0) & (eq_rank < need)), 1.0, 0.0)
    sel_local, offs = _group_prefix(sel, tri)
    for g in range(N_GROUPS):
        sl = slice(g * GROUP, (g + 1) * GROUP)
        cnt_ref[g] = sel_local[g] + sel[:, sl]
        affg_ref[g] = aff[:, sl]

    lane = lax.broadcasted_iota(I32, (N_EXPERTS, LANES), 1)
    never = jnp.full((N_EXPERTS, LANES), 2.0 * N_TOK, F32)
    grp_lo, grp_hi = never, never
    for g in range(N_GROUPS):
        grp_lo = jnp.where(lane == g, offs[g], grp_lo)
        grp_hi = jnp.where(lane == g, offs[g + 1], grp_hi)
    row = lax.broadcasted_iota(I32, (CAP, LANES), 0).astype(F32)
    in_group_lane = lax.broadcasted_iota(I32, (CAP, GROUP), 1).astype(F32)
    zpad = jnp.zeros((LANES - N_GROUPS, GROUP), BF16)
    for e in range(N_EXPERTS):
        lo_row, hi_row = grp_lo[e:e + 1, :], grp_hi[e:e + 1, :]
        in_grp = (lo_row <= row) & (row < hi_row)
        onehot = jnp.where(in_grp, 1.0, 0.0).astype(BF16)
        counts = _dot(onehot, jnp.concatenate([cnt_ref[:, e, :].astype(BF16), zpad], axis=0))
        rank = row[:, 0:1] - jnp.sum(jnp.where(in_grp, lo_row, 0.0), axis=1, keepdims=True)
        local = jnp.sum(jnp.where(counts <= rank, 1.0, 0.0), axis=1, keepdims=True)
        grp = jnp.sum(jnp.where(hi_row <= row, 1.0, 0.0), axis=1, keepdims=True)
        tok = grp * GROUP + local
        aff_rows = sum(_dot(onehot, jnp.concatenate([part, zpad], axis=0))
                       for part in _split3_bf16(affg_ref[:, e, :]))
        gate = jnp.sum(jnp.where(in_group_lane == local, aff_rows, 0.0), axis=1, keepdims=True)
        gc_ref[0, e] = jnp.broadcast_to(gate, (CAP, LANES))
        tok_b = jnp.broadcast_to(tok, (CAP, LANES))
        tok_row = jnp.concatenate([tok_b[t * LANES:(t + 1) * LANES, :].T[0:1, :]
                                   for t in range(CAP // LANES)], axis=1)
        idx_ref[0, e:e + 1, :] = tok_row.astype(I32)


def _plan(logits_t, tri):
    ns = N_STREAMS
    return pl.pallas_call(
        _plan_kernel,
        grid=(ns,),
        in_specs=[pl.BlockSpec((N_EXPERTS, N_TOK), lambda s: (0, s)),
                  pl.BlockSpec((MXU_DIM, MXU_DIM), lambda s: (0, 0))],
        out_specs=[pl.BlockSpec((1, N_EXPERTS, CAP), lambda s: (s, 0, 0)),
                   pl.BlockSpec((1, N_EXPERTS, CAP, LANES), lambda s: (s, 0, 0, 0))],
        out_shape=[jax.ShapeDtypeStruct((ns, N_EXPERTS, CAP), I32),
                   jax.ShapeDtypeStruct((ns, N_EXPERTS, CAP, LANES), F32)],
        scratch_shapes=[pltpu.VMEM((N_GROUPS, N_EXPERTS, GROUP), F32),
                        pltpu.VMEM((N_GROUPS, N_EXPERTS, GROUP), F32)],
        compiler_params=_params("arbitrary"),
        name="plan",
    )(logits_t, tri)


def _tile_rows(r):
    return pl.ds(pl.multiple_of(r * ROW_TILE, ROW_TILE), ROW_TILE)


def _gather_kernel(idx_ref, h_ref, xe_ref, stage_ref):
    s, e = pl.program_id(0), pl.program_id(1)
    base = (s * N_EXPERTS + e) * CAP

    def move(g, carry):
        for u in range(MOVE_UNROLL):
            r = g * MOVE_UNROLL + u
            stage_ref[_tile_rows(r), :] = h_ref[0, _tile_rows(idx_ref[base + r]), :]
        return carry

    lax.fori_loop(0, CAP // MOVE_UNROLL, move, 0)
    for k in range(ROW_TILE):
        xe_ref[0, 0, :, k * LANES:(k + 1) * LANES] = (
            stage_ref[pl.ds(k, CAP, stride=ROW_TILE), :].astype(BF16))


def _gather(idx_flat, h2_tiles):
    ns = h2_tiles.shape[0]
    return pl.pallas_call(
        _gather_kernel,
        grid_spec=pltpu.PrefetchScalarGridSpec(
            num_scalar_prefetch=1,
            grid=(ns, N_EXPERTS),
            in_specs=[pl.BlockSpec((1, N_TOK * ROW_TILE, LANES), lambda s, e, idx: (s, 0, 0),
                                   pipeline_mode=pl.Buffered(1))],
            out_specs=pl.BlockSpec((1, 1, CAP, D_MODEL), lambda s, e, idx: (s, e, 0, 0)),
            scratch_shapes=[pltpu.VMEM((CAP * ROW_TILE, LANES), F32)],
        ),
        out_shape=jax.ShapeDtypeStruct((ns, N_EXPERTS, CAP, D_MODEL), BF16),
        compiler_params=_params("arbitrary", "arbitrary"),
        name="moe_gather",
    )(idx_flat, h2_tiles)


def _ffn_kernel(x_ref, wg_ref, wu_ref, wd_ref, gc_ref, ye_ref, acc_ref, *, row_tile):
    f = pl.program_id(1)
    last = pl.num_programs(1) - 1
    ns = x_ref.shape[0]
    wg = wg_ref[0, 0].astype(BF16)
    wu = wu_ref[0, 0].astype(BF16)
    wd = wd_ref[0, 0].astype(BF16)

    def partial_out(s, r0):
        x = x_ref[s, 0, r0:r0 + row_tile, :]
        hid = _silu(_dot(x, wg)) * _dot(x, wu)
        return _dot(hid.astype(BF16), wd)

    tiles = [(s, r0) for s in range(ns) for r0 in range(0, CAP, row_tile)]

    @pl.when(f == 0)
    def _():
        for s, r0 in tiles:
            acc_ref[s, r0:r0 + row_tile, :] = partial_out(s, r0)

    @pl.when((f > 0) & (f < last))
    def _():
        for s, r0 in tiles:
            acc_ref[s, r0:r0 + row_tile, :] += partial_out(s, r0)

    @pl.when(f == last)
    def _():
        for s, r0 in tiles:
            ye = (acc_ref[s, r0:r0 + row_tile, :] + partial_out(s, r0)) * gc_ref[s, 0, r0:r0 + row_tile, 0:1]
            for k in range(ROW_TILE):
                ye_ref[s, 0, pl.ds(r0 * ROW_TILE + k, row_tile, stride=ROW_TILE), :] = (
                    ye[:, k * LANES:(k + 1) * LANES])


def _ffn(xe, gc, w_gate, w_up, w_down, layer):
    ns = xe.shape[0]
    return pl.pallas_call(
        functools.partial(_ffn_kernel, row_tile=512),
        grid=(N_EXPERTS, EXPERT_FF // FF_TILE),
        in_specs=[
            pl.BlockSpec((ns, 1, CAP, D_MODEL), lambda e, f: (0, e, 0, 0)),
            pl.BlockSpec((1, 1, D_MODEL, FF_TILE), lambda e, f: (layer, e, 0, f)),
            pl.BlockSpec((1, 1, D_MODEL, FF_TILE), lambda e, f: (layer, e, 0, f)),
            pl.BlockSpec((1, 1, FF_TILE, D_MODEL), lambda e, f: (layer, e, f, 0)),
            pl.BlockSpec((ns, 1, CAP, LANES), lambda e, f: (0, e, 0, 0)),
        ],
        out_specs=pl.BlockSpec((ns, 1, CAP * ROW_TILE, LANES), lambda e, f: (0, e, 0, 0)),
        out_shape=jax.ShapeDtypeStruct((ns, N_EXPERTS, CAP * ROW_TILE, LANES), F32),
        scratch_shapes=[pltpu.VMEM((ns, CAP, D_MODEL), F32)],
        compiler_params=_params("arbitrary", "arbitrary"),
        name="moe_ffn",
    )(xe, w_gate, w_up, w_down, gc)


def _scatter_kernel(idx_ref, ye_ref, y_ref, gate_ref, *refs, split):
    out_refs, acc_ref = refs[:-1], refs[-1]
    s, j = pl.program_id(0), pl.program_id(1)

    @pl.when(j == 0)
    def _():
        acc_ref[...] = jnp.zeros_like(acc_ref)

    @pl.when(j < N_EXPERTS)
    def _():
        base = (s * N_EXPERTS + j) * CAP

        def add_rows(g, carry):
            r0 = g * MOVE_UNROLL
            dst = [_tile_rows(idx_ref[base + r0 + u]) for u in range(MOVE_UNROLL)]
            val = [acc_ref[dst[u], :] + ye_ref[0, 0, _tile_rows(r0 + u), :] for u in range(MOVE_UNROLL)]
            for u in range(MOVE_UNROLL):
                acc_ref[dst[u], :] = val[u]
            return carry

        lax.fori_loop(0, CAP // MOVE_UNROLL, add_rows, 0)

    @pl.when(j >= N_EXPERTS)
    def _():
        c = j - N_EXPERTS
        rows = FIN_TOK * ROW_TILE
        part = acc_ref.at[pl.ds(pl.multiple_of(c * rows, rows), rows), :]
        moe = jnp.concatenate([part[pl.ds(k, FIN_TOK, stride=ROW_TILE), :] for k in range(ROW_TILE)], axis=1)
        res = y_ref[...] + gate_ref[0] * moe
        if split:
            for si, out_ref in enumerate(out_refs):
                @pl.when(s == si)
                def _(out_ref=out_ref):
                    out_ref[...] = res
        else:
            out_refs[0][...] = res


def _scatter(idx_flat, ye_tiles, y, mod3, layer, lat_seq, split):
    ns = ye_tiles.shape[0]
    fin = lambda j: jnp.maximum(j - N_EXPERTS, 0)

    def gate_index(s, j, idx):
        row = _mod_row(s * N_FIN + fin(j), FIN_TOK, lat_seq)
        return ((layer * N_MOD_ROWS + row) * 6 + 5, 0, 0)

    chunk = (FIN_TOK, D_MODEL)
    if split:
        out_specs = [
            pl.BlockSpec(chunk, lambda s, j, idx, si=si: (
                jnp.where(s == si, fin(j), jnp.where(s < si, 0, N_FIN - 1)), 0))
            for si in range(ns)]
        out_shape = [jax.ShapeDtypeStruct((N_TOK, D_MODEL), F32)] * ns
    else:
        out_specs = pl.BlockSpec(chunk, lambda s, j, idx: (s * N_FIN + fin(j), 0))
        out_shape = jax.ShapeDtypeStruct(y.shape, F32)
    return pl.pallas_call(
        functools.partial(_scatter_kernel, split=split),
        grid_spec=pltpu.PrefetchScalarGridSpec(
            num_scalar_prefetch=1,
            grid=(ns, N_EXPERTS + N_FIN),
            in_specs=[
                pl.BlockSpec((1, 1, CAP * ROW_TILE, LANES),
                             lambda s, j, idx: (s, jnp.minimum(j, N_EXPERTS - 1), 0, 0)),
                pl.BlockSpec(chunk, lambda s, j, idx: (s * N_FIN + fin(j), 0)),
                pl.BlockSpec((1, 1, D_MODEL), gate_index),
            ],
            out_specs=out_specs,
            scratch_shapes=[pltpu.VMEM((N_TOK * ROW_TILE, LANES), F32)],
        ),
        out_shape=out_shape,
        compiler_params=_params("arbitrary", "arbitrary"),
        name="moe_scatter",
    )(idx_flat, ye_tiles, y, mod3)


def _rope_tables(n_tokens):
    t = np.arange(n_tokens)
    row = (t // GRID_W).astype(np.float32)
    col = (t % GRID_W).astype(np.float32)
    pairs = HEAD_DIM // 4
    inv_freq = ROPE_THETA ** (-jnp.arange(pairs, dtype=F32) / pairs)
    ang = jnp.concatenate([row[:, None] * inv_freq, col[:, None] * inv_freq], axis=-1)
    cos, sin = jnp.cos(ang), jnp.sin(ang)
    reps = MXU_DIM // HEAD_DIM
    cos_t = jnp.tile(jnp.concatenate([cos, cos], axis=-1), (1, reps))
    sin_t = jnp.tile(jnp.concatenate([-sin, sin], axis=-1), (1, reps))
    return (jnp.concatenate([cos_t, jnp.ones((TM, MXU_DIM), F32)], axis=0),
            jnp.concatenate([sin_t, jnp.zeros((TM, MXU_DIM), F32)], axis=0))


def _dup_cache(cache):
    b, s, hk, hd = cache.shape
    return jnp.broadcast_to(cache[:, :, :, None, :], (b, s, hk, 2, hd)).reshape(b, s, 2 * hk * hd).astype(BF16)


def kernel(x_prompt, x_sample, cache_attn_k, cache_attn_v, cache_na_k, cache_na_v, c, c_ctx,
           norm1_g, norm2_g, w_ada, b_ada, attn_w_qkv, attn_q_gain, attn_k_gain, attn_w_o,
           na_w_qkv, na_q_gain, na_k_gain, na_rpb, na_w_o,
           moe_w_router, moe_w_gate, moe_w_up, moe_w_down):
    bc, tc, _ = x_prompt.shape
    bl, tl, _ = x_sample.shape
    depth = w_ada.shape[0]
    assert bc * tc == N_TOK and bl * tl == N_TOK and 1 + bl <= N_MOD_ROWS
    assert tl % TM == 0 and tl % FIN_TOK == 0
    n_all = N_STREAMS * N_TOK

    cond = jnp.zeros((N_MOD_ROWS, D_MODEL), F32).at[0].set(c_ctx).at[1:1 + bl].set(c)
    mod3 = _ada(cond, w_ada, b_ada).reshape(depth * N_MOD_ROWS * 6, 1, D_MODEL)

    seg = jnp.asarray(np.kron(np.eye(MXU_DIM // HEAD_DIM), np.ones((HEAD_DIM, HEAD_DIM))), BF16)
    tri = jnp.asarray(np.triu(np.ones((MXU_DIM, MXU_DIM)), k=1), BF16)
    rope_tables = _rope_tables(tl)

    x_ctx = (x_prompt.reshape(N_TOK, D_MODEL), 0)
    x_lat = (x_sample.reshape(N_TOK, D_MODEL), 0)
    new_k, new_v = [], []
    for i in range(depth):
        j = i // 2
        gqa = i % 2 == 0
        if gqa:
            w_qkv, q_gain, k_gain, w_o = attn_w_qkv[j], attn_q_gain[j], attn_k_gain[j], attn_w_o[j]
            kv_heads = N_KV_HEADS
        else:
            w_qkv, q_gain, k_gain, w_o = na_w_qkv[j], na_q_gain[j], na_k_gain[j], na_w_o[j]
            kv_heads = N_HEADS
        wr_hi, wr_lo = _split_bf16(moe_w_router[i].T)

        q, kb, vb, kf, vf = _qkv(x_ctx, x_lat, norm1_g[i], mod3, i, tl, w_qkv.astype(BF16), q_gain, k_gain,
                                 seg, kv_heads * HEAD_DIM, rope_tables if gqa else None)
        kvb = kb.shape[1]
        o_ctx = _attention(q.reshape(n_all // tc, tc, D_MODEL), kb.reshape(n_all // tc, tc, kvb),
                           vb.reshape(n_all // tc, tc, kvb), bc, 0)
        q_l, kb_l, vb_l = (a.reshape(n_all // tl, tl, a.shape[1]) for a in (q, kb, vb))
        if gqa:
            o_lat = _attention(q_l, kb_l, vb_l, bl, N_TOK // tl,
                               _dup_cache(cache_attn_k[:, j]), _dup_cache(cache_attn_v[:, j]))
        else:
            past = cache_na_k.shape[2]
            o_lat = _na_attention(q_l, kb_l, vb_l, bl, N_TOK // tl,
                                  cache_na_k[:, j].reshape(bl, past, D_MODEL).astype(BF16),
                                  cache_na_v[:, j].reshape(bl, past, D_MODEL).astype(BF16),
                                  _na_bias_table(na_rpb[j]))
        new_k.append(kf.reshape(bc, 1, tc, kv_heads, HEAD_DIM))
        new_v.append(vf.reshape(bc, 1, tc, kv_heads, HEAD_DIM))

        y, h2, logits_t = _post_attention(o_ctx.reshape(N_TOK, D_MODEL), o_lat.reshape(N_TOK, D_MODEL),
                                          x_ctx, x_lat, w_o.astype(BF16), norm2_g[i], mod3, i, tl,
                                          wr_hi, wr_lo)
        idx, gc = _plan(logits_t, tri)
        idx_flat = idx.reshape(-1)
        xe = _gather(idx_flat, h2.reshape(N_STREAMS, N_TOK * ROW_TILE, LANES))
        ye = _ffn(xe, gc, moe_w_gate, moe_w_up, moe_w_down, i)
        y = _scatter(idx_flat, ye, y, mod3, i, tl, split=(i == depth - 1))
        x_ctx, x_lat = (y, 0), (y, CTX_TILES)

    y_ctx, y_lat = y
    return (y_ctx.reshape(bc, tc, D_MODEL), y_lat.reshape(bl, tl, D_MODEL),
            jnp.concatenate(new_k[0::2], axis=1), jnp.concatenate(new_v[0::2], axis=1),
            jnp.concatenate(new_k[1::2], axis=1), jnp.concatenate(new_v[1::2], axis=1))
```

```python
import functools

import jax
import jax.numpy as jnp
import numpy as np
from jax import lax
from jax.experimental import pallas as pl
from jax.experimental.pallas import tpu as pltpu

F32 = jnp.float32
BF16 = jnp.bfloat16
I32 = jnp.int32

D_MODEL = 1024
N_HEADS = 16
N_KV_HEADS = 4
HEAD_DIM = 64
GRID_W = 64
WIN_R = 8
WIN_C = 16
N_EXPERTS = 16
EXPERT_FF = 2048
ROPE_THETA = 10000.0
RMS_EPS = 1e-6
NEG_INF = -1e30
F32_MIN_NORMAL_BITS = 0x00800000
LOG2_E = 1.4426950408889634
Q_SCALE = HEAD_DIM ** -0.5 * LOG2_E

LANES = 128
MXU_DIM = 256
VMEM_LIMIT = 56 * 1024 * 1024

N_STREAMS = 2
N_TOK = 8192
CAP = 2 * N_TOK // N_EXPERTS
TM = 512
N_MOD_ROWS = 16
ROW_TILE = D_MODEL // LANES
GROUP = MXU_DIM
N_GROUPS = N_TOK // GROUP
FIN_TOK = 512
N_FIN = N_TOK // FIN_TOK
MOVE_UNROLL = 8
FF_TILE = 512
NA_BAND = 10
NA_QBLK = 128
N_REL_R = 2 * WIN_R - 1
N_REL_C = 2 * WIN_C - 1


def _params(*sem):
    return pltpu.CompilerParams(dimension_semantics=sem, vmem_limit_bytes=VMEM_LIMIT)


def _dot(a, b):
    return jnp.dot(a, b, preferred_element_type=F32)


def _dot_nt(a, b):
    return lax.dot_general(a, b, (((1,), (1,)), ((), ())), preferred_element_type=F32)


def _split_bf16(x):
    hi = x.astype(BF16)
    lo = (x - hi.astype(F32)).astype(BF16)
    return hi, lo


def _silu(x):
    return x * (1.0 / (1.0 + jnp.exp(-x)))


def _ada_kernel(cond_ref, w_ref, b_ref, out_ref):
    sx = _silu(cond_ref[...])
    xh, xl = _split_bf16(sx)
    wh, wl = _split_bf16(w_ref[0])
    out_ref[0] = _dot(xh, wh) + _dot(xl, wh) + _dot(xh, wl) + b_ref[0]


def _ada(cond, w_ada, b_ada):
    depth = w_ada.shape[0]
    tn = 1024
    n_out = w_ada.shape[2]
    return pl.pallas_call(
        _ada_kernel,
        grid=(depth, n_out // tn),
        in_specs=[
            pl.BlockSpec((N_MOD_ROWS, D_MODEL), lambda l, n: (0, 0)),
            pl.BlockSpec((1, D_MODEL, tn), lambda l, n: (l, 0, n)),
            pl.BlockSpec((1, 1, tn), lambda l, n: (l, 0, n)),
        ],
        out_specs=pl.BlockSpec((1, N_MOD_ROWS, tn), lambda l, n: (l, 0, n)),
        out_shape=jax.ShapeDtypeStruct((depth, N_MOD_ROWS, n_out), F32),
        compiler_params=_params("arbitrary", "arbitrary"),
        name="ada",
    )(cond, w_ada, b_ada.reshape(depth, 1, n_out))


def _mod_row(tile, tile_rows, lat_seq):
    ctx_tiles = N_TOK // tile_rows
    return jnp.where(tile < ctx_tiles, 0, 1 + (tile - ctx_tiles) // (lat_seq // tile_rows))


def _mod_spec(layer, which, lat_seq, tile_of=lambda i: i):
    def index(i):
        return ((layer * N_MOD_ROWS + _mod_row(tile_of(i), TM, lat_seq)) * 6 + which, 0, 0)
    return pl.BlockSpec((1, 1, D_MODEL), index)


CTX_TILES = N_TOK // TM


def _x_specs(x_ctx_block0, x_lat_block0, tile_of=lambda i: i):
    return [
        pl.BlockSpec((TM, D_MODEL), lambda i: (x_ctx_block0 + jnp.minimum(tile_of(i), CTX_TILES - 1), 0)),
        pl.BlockSpec((TM, D_MODEL), lambda i: (x_lat_block0 + jnp.maximum(tile_of(i) - CTX_TILES, 0), 0)),
    ]


def _head_norm(z, seg, gain):
    ss = _dot((z * z).astype(BF16), seg)
    return z * lax.rsqrt(ss * (1.0 / HEAD_DIM) + RMS_EPS) * gain


def _rope(z, cos_t, sin_t):
    lane = lax.broadcasted_iota(I32, z.shape, 1)
    first = (lane & 32) == 0
    n = z.shape[1]
    partner = jnp.where(first, pltpu.roll(z, n - 32, axis=1), pltpu.roll(z, 32, axis=1))
    return z * cos_t + partner * sin_t


def _dup_heads(z):
    outs = []
    for b in range(z.shape[1] // LANES):
        x = z[:, b * LANES:(b + 1) * LANES]
        xr = pltpu.roll(x, HEAD_DIM, axis=1)
        lo = lax.broadcasted_iota(I32, x.shape, 1) < HEAD_DIM
        outs.append(jnp.where(lo, x, xr))
        outs.append(jnp.where(lo, xr, x))
    return jnp.concatenate(outs, axis=1)


def _qkv_kernel(*refs, kv_width, rope):
    it = iter(refs)
    xc_ref, xl_ref, g_ref, shift_ref, scale_ref, w_ref, qg_ref, kg_ref, seg_ref = (next(it) for _ in range(9))
    cos_ref = sin_ref = None
    if rope:
        cos_ref, sin_ref = next(it), next(it)
    q_ref, kb_ref, vb_ref, kh_ref, vh_ref, kf_ref, vf_ref = (next(it) for _ in range(7))
    dup = kv_width == N_KV_HEADS * HEAD_DIM
    is_ctx = _qkv_tile(pl.program_id(0)) < CTX_TILES

    x = jnp.where(is_ctx, xc_ref[...], xl_ref[...])
    ms = jnp.mean(x * x, axis=-1, keepdims=True)
    h = x * lax.rsqrt(ms + RMS_EPS) * g_ref[...]
    h = h * (1.0 + scale_ref[0]) + shift_ref[0]
    hb = h.astype(BF16)
    seg = seg_ref[...]
    if rope:
        cos_t, sin_t = cos_ref[...], sin_ref[...]

    q_width = N_HEADS * HEAD_DIM
    for cidx in range(q_width // MXU_DIM):
        c0 = cidx * MXU_DIM
        z = _head_norm(_dot(hb, w_ref[:, c0:c0 + MXU_DIM]), seg, qg_ref[...])
        if rope:
            z = _rope(z, cos_t, sin_t)
        q_ref[:, c0:c0 + MXU_DIM] = (z * Q_SCALE).astype(BF16)
    for cidx in range(kv_width // MXU_DIM):
        c0 = cidx * MXU_DIM
        z = _head_norm(_dot(hb, w_ref[:, q_width + c0:q_width + c0 + MXU_DIM]), seg, kg_ref[...])
        kf_ref[:, c0:c0 + MXU_DIM] = z
        if rope:
            z = _rope(z, cos_t, sin_t)
        v = _dot(hb, w_ref[:, q_width + kv_width + c0:q_width + kv_width + c0 + MXU_DIM])
        vf_ref[:, c0:c0 + MXU_DIM] = v
        if dup:
            kb_ref[:, 2 * c0:2 * c0 + 2 * MXU_DIM] = _dup_heads(z).astype(BF16)
            vb_ref[:, 2 * c0:2 * c0 + 2 * MXU_DIM] = _dup_heads(v).astype(BF16)
        else:
            kb_ref[:, c0:c0 + MXU_DIM] = z.astype(BF16)
            vb_ref[:, c0:c0 + MXU_DIM] = v.astype(BF16)

    @pl.when(is_ctx)
    def _():
        n_heads = kv_width // HEAD_DIM
        for src_ref, dst_ref in ((kf_ref, kh_ref), (vf_ref, vh_ref)):
            for b in range(kv_width // LANES):
                pair = src_ref[:, b * LANES:(b + 1) * LANES]
                swapped = pltpu.roll(pair, HEAD_DIM, axis=1)
                for hh, val in enumerate((pair, swapped)):
                    dst_ref[pl.ds(2 * b + hh, TM, stride=n_heads), :] = val[:, :HEAD_DIM]


def _qkv_tile(i):
    return (i + CTX_TILES) % (N_STREAMS * CTX_TILES)


def _qkv(x_ctx, x_lat, norm_g, mod3, layer, lat_seq, w_bf16, q_gain, k_gain, seg, kv_width, rope_tables):
    n_tok = N_STREAMS * N_TOK
    n_w = w_bf16.shape[1]
    kvb_width = 2 * kv_width if kv_width == N_KV_HEADS * HEAD_DIM else kv_width
    tile = lambda w: pl.BlockSpec((TM, w), lambda i: (_qkv_tile(i), 0))
    ctx_heads = pl.BlockSpec((TM * (kv_width // HEAD_DIM), HEAD_DIM),
                             lambda i: (jnp.maximum(i - CTX_TILES, 0), 0))
    const = lambda shape: pl.BlockSpec(shape, lambda i: (0,) * len(shape))
    in_specs = _x_specs(x_ctx[1], x_lat[1], _qkv_tile) + [
        const((1, D_MODEL)),
        _mod_spec(layer, 0, lat_seq, _qkv_tile),
        _mod_spec(layer, 1, lat_seq, _qkv_tile),
        const((D_MODEL, n_w)),
        const((1, MXU_DIM)),
        const((1, MXU_DIM)),
        const((MXU_DIM, MXU_DIM)),
    ]
    args = [x_ctx[0], x_lat[0], norm_g.reshape(1, D_MODEL), mod3, mod3, w_bf16,
            jnp.tile(q_gain, MXU_DIM // HEAD_DIM).reshape(1, MXU_DIM),
            jnp.tile(k_gain, MXU_DIM // HEAD_DIM).reshape(1, MXU_DIM), seg]
    if rope_tables is not None:
        seq_tiles = lat_seq // TM
        rope_index = lambda i: (jnp.where(_qkv_tile(i) < CTX_TILES, seq_tiles, _qkv_tile(i) % seq_tiles), 0)
        in_specs += [pl.BlockSpec((TM, MXU_DIM), rope_index)] * 2
        args += list(rope_tables)
    n_heads = kv_width // HEAD_DIM
    out_specs = [tile(D_MODEL), tile(kvb_width), tile(kvb_width), ctx_heads, ctx_heads]
    out_shape = [jax.ShapeDtypeStruct((n_tok, D_MODEL), BF16),
                 jax.ShapeDtypeStruct((n_tok, kvb_width), BF16),
                 jax.ShapeDtypeStruct((n_tok, kvb_width), BF16),
                 jax.ShapeDtypeStruct((N_TOK * n_heads, HEAD_DIM), F32),
                 jax.ShapeDtypeStruct((N_TOK * n_heads, HEAD_DIM), F32)]
    return pl.pallas_call(
        functools.partial(_qkv_kernel, kv_width=kv_width, rope=rope_tables is not None),
        grid=(n_tok // TM,),
        in_specs=in_specs,
        out_specs=out_specs,
        out_shape=out_shape,
        scratch_shapes=[pltpu.VMEM((TM, kv_width), F32)] * 2,
        compiler_params=_params("arbitrary"),
        name="qkv",
    )(*args)


def _two_head_rows(qj):
    lo = lax.broadcasted_iota(I32, qj.shape, 1) < HEAD_DIM
    zero = jnp.zeros_like(qj)
    return jnp.concatenate([jnp.where(lo, qj, zero), jnp.where(lo, zero, qj)], axis=0)


def _merge_two_heads(r):
    tq = r.shape[0] // 2
    lo = lax.broadcasted_iota(I32, (tq, LANES), 1) < HEAD_DIM
    return jnp.where(lo, r[:tq], r[tq:])


def _attn_kernel(*refs, n_kv_blocks, has_ctx):
    if has_ctx:
        q_ref, k_ref, v_ref, ck_ref, cv_ref, o_ref = refs
    else:
        q_ref, k_ref, v_ref, o_ref = refs
    n_q_blocks = D_MODEL // LANES
    for j in range(n_q_blocks):
        kb = (j * n_kv_blocks) // n_q_blocks
        ksl = slice(kb * LANES, (kb + 1) * LANES)
        q2 = _two_head_rows(q_ref[0, :, j * LANES:(j + 1) * LANES])
        s = _dot_nt(q2, k_ref[0, :, ksl])
        m = jnp.max(s, axis=-1, keepdims=True)
        if has_ctx:
            sc = _dot_nt(q2, ck_ref[0, :, ksl])
            m = jnp.maximum(m, jnp.max(sc, axis=-1, keepdims=True))
        p = jnp.exp2(s - m)
        l = jnp.sum(p, axis=-1, keepdims=True)
        r = _dot(p.astype(BF16), v_ref[0, :, ksl])
        if has_ctx:
            pc = jnp.exp2(sc - m)
            l = l + jnp.sum(pc, axis=-1, keepdims=True)
            r = r + _dot(pc.astype(BF16), cv_ref[0, :, ksl])
        r = r / l
        o_ref[0, :, j * LANES:(j + 1) * LANES] = _merge_two_heads(r).astype(BF16)


def _attention(q, k, v, n_batch, batch0, ctx_k=None, ctx_v=None, tq=256):
    t = q.shape[1]
    s, w = k.shape[1], k.shape[2]
    has_ctx = ctx_k is not None
    in_specs = [
        pl.BlockSpec((1, tq, D_MODEL), lambda bi, qi: (bi + batch0, qi, 0)),
        pl.BlockSpec((1, s, w), lambda bi, qi: (bi + batch0, 0, 0)),
        pl.BlockSpec((1, s, w), lambda bi, qi: (bi + batch0, 0, 0)),
    ]
    args = [q, k, v]
    if has_ctx:
        sc = ctx_k.shape[1]
        in_specs += [pl.BlockSpec((1, sc, w), lambda bi, qi: (bi, 0, 0))] * 2
        args += [ctx_k, ctx_v]
    return pl.pallas_call(
        functools.partial(_attn_kernel, n_kv_blocks=w // LANES, has_ctx=has_ctx),
        grid=(n_batch, t // tq),
        in_specs=in_specs,
        out_specs=pl.BlockSpec((1, tq, D_MODEL), lambda bi, qi: (bi, qi, 0)),
        out_shape=jax.ShapeDtypeStruct((n_batch, t, D_MODEL), BF16),
        compiler_params=_params("arbitrary", "arbitrary"),
        name="attn",
    )(*args)


def _na_band_start(blk, rows):
    row_start = jnp.clip(2 * blk - WIN_R // 2, 0, rows - WIN_R)
    return jnp.minimum(row_start, rows - NA_BAND)


def _na_kernel(q_ref, k_ref, v_ref, ck_ref, cv_ref, bias_ref, o_ref, *, rows):
    blk = pl.program_id(1)
    band_start = _na_band_start(blk, rows)
    n_band = NA_BAND * GRID_W
    band = pl.ds(pl.multiple_of(band_start * GRID_W, LANES), n_band)
    shape = (NA_QBLK, n_band)
    q_r = 2 * blk + lax.broadcasted_iota(I32, shape, 0) // GRID_W
    k_r = band_start + lax.broadcasted_iota(I32, shape, 1) // GRID_W
    q_rs = jnp.clip(q_r - WIN_R // 2, 0, rows - WIN_R)
    row_ok1 = (k_r >= q_rs) & (k_r < q_rs + WIN_R)
    row_ok = jnp.concatenate([row_ok1, row_ok1], axis=0)
    pair_shift = (band_start - 2 * blk + WIN_R) // 2
    for j in range(D_MODEL // LANES):
        sl = slice(j * LANES, (j + 1) * LANES)
        q2 = _two_head_rows(q_ref[0, :, sl])
        s = _dot_nt(q2, k_ref[0, band, sl])
        bias = jnp.concatenate(
            [jnp.concatenate([bias_ref[2 * j + hh, pair_shift + m] for m in range(NA_BAND // 2)], axis=1)
             for hh in range(2)], axis=0)
        s = jnp.where(row_ok, s + bias, NEG_INF)
        sc = _dot_nt(q2, ck_ref[0, :, sl])
        m = jnp.maximum(jnp.max(s, axis=-1, keepdims=True), jnp.max(sc, axis=-1, keepdims=True))
        p = jnp.exp2(s - m)
        pc = jnp.exp2(sc - m)
        l = jnp.sum(p, axis=-1, keepdims=True) + jnp.sum(pc, axis=-1, keepdims=True)
        r = _dot(p.astype(BF16), v_ref[0, band, sl]) + _dot(pc.astype(BF16), cv_ref[0, :, sl])
        r = r / l
        o_ref[0, :, sl] = _merge_two_heads(r).astype(BF16)


def _na_bias_kernel(rpb_ref, out_ref, tz_ref):
    h = pl.program_id(0)
    shape = (GRID_W, LANES)
    q_col = lax.broadcasted_iota(I32, shape, 0)
    lane = lax.broadcasted_iota(I32, shape, 1)
    k_col = lane & (GRID_W - 1)
    col_start = jnp.clip(q_col - WIN_C // 2, 0, GRID_W - WIN_C)
    col_ok = (k_col >= col_start) & (k_col < col_start + WIN_C)
    rel_c = jnp.clip(k_col - q_col + WIN_C - 1, 0, N_REL_C - 1)
    for a in range(N_REL_R):
        acc = jnp.zeros(shape, F32)
        for b in range(N_REL_C):
            acc = jnp.where(rel_c == b, rpb_ref[(h * N_REL_R + a) * N_REL_C + b], acc)
        tz_ref[a] = jnp.where(col_ok, acc * LOG2_E, NEG_INF)
    left = lane < GRID_W
    for dd in range(WIN_R + 1):
        for qr in range(2):
            rel = [min(max(2 * dd - WIN_R + kr - qr + WIN_R - 1, 0), N_REL_R - 1) for kr in range(2)]
            out_ref[0, dd, qr * GRID_W:(qr + 1) * GRID_W, :] = jnp.where(left, tz_ref[rel[0]], tz_ref[rel[1]])


def _na_bias_table(rpb):
    return pl.pallas_call(
        _na_bias_kernel,
        grid=(N_HEADS,),
        in_specs=[pl.BlockSpec(memory_space=pltpu.SMEM)],
        out_specs=pl.BlockSpec((1, WIN_R + 1, LANES, LANES), lambda h: (h, 0, 0, 0)),
        out_shape=jax.ShapeDtypeStruct((N_HEADS, WIN_R + 1, LANES, LANES), F32),
        scratch_shapes=[pltpu.VMEM((N_REL_R, GRID_W, LANES), F32)],
        compiler_params=_params("arbitrary"),
        name="na_bias",
    )(rpb.reshape(-1))


def _na_attention(q, k, v, n_batch, batch0, ctx_k, ctx_v, bias_tbl):
    t = q.shape[1]
    sc = ctx_k.shape[1]
    rows = t // GRID_W
    full = lambda n, off: pl.BlockSpec((1, n, D_MODEL), lambda bi, qi: (bi + off, 0, 0))
    return pl.pallas_call(
        functools.partial(_na_kernel, rows=rows),
        grid=(n_batch, t // NA_QBLK),
        in_specs=[
            pl.BlockSpec((1, NA_QBLK, D_MODEL), lambda bi, qi: (bi + batch0, qi, 0)),
            full(t, batch0), full(t, batch0), full(sc, 0), full(sc, 0),
            pl.BlockSpec(bias_tbl.shape, lambda bi, qi: (0, 0, 0, 0)),
        ],
        out_specs=pl.BlockSpec((1, NA_QBLK, D_MODEL), lambda bi, qi: (bi, qi, 0)),
        out_shape=jax.ShapeDtypeStruct((n_batch, t, D_MODEL), BF16),
        compiler_params=_params("arbitrary", "arbitrary"),
        name="na_attn",
    )(q, k, v, ctx_k, ctx_v, bias_tbl)


def _post_kernel(oc_ref, ol_ref, xc_ref, xl_ref, wo_ref, gate_ref, g_ref, shift_ref, scale_ref,
                 wrh_ref, wrl_ref, y_ref, h_ref, lg_ref):
    is_ctx = pl.program_id(0) < CTX_TILES
    o = jnp.where(is_ctx, oc_ref[...], ol_ref[...])
    x = jnp.where(is_ctx, xc_ref[...], xl_ref[...])
    y = x + gate_ref[0] * _dot(o, wo_ref[...])
    y_ref[...] = y
    ms = jnp.mean(y * y, axis=-1, keepdims=True)
    h = y * lax.rsqrt(ms + RMS_EPS) * g_ref[...]
    h = h * (1.0 + scale_ref[0]) + shift_ref[0]
    for k in range(ROW_TILE):
        h_ref[pl.ds(k, TM, stride=ROW_TILE), :] = h[:, k * LANES:(k + 1) * LANES]
    hh, hl = _split_bf16(h)
    lg = _dot(hh, wrh_ref[...]) + _dot(hl, wrh_ref[...]) + _dot(hh, wrl_ref[...])
    lg_ref[...] = jnp.concatenate([lg[t * LANES:(t + 1) * LANES, :].T[:N_EXPERTS, :]
                                   for t in range(TM // LANES)], axis=1)


def _post_attention(o_ctx, o_lat, x_ctx, x_lat, wo_bf16, norm_g, mod3, layer, lat_seq, wr_hi, wr_lo):
    n_tok = N_STREAMS * N_TOK
    tile = lambda w: pl.BlockSpec((TM, w), lambda i: (i, 0))
    const = lambda shape: pl.BlockSpec(shape, lambda i: (0,) * len(shape))
    return pl.pallas_call(
        _post_kernel,
        grid=(n_tok // TM,),
        in_specs=_x_specs(0, 0) + _x_specs(x_ctx[1], x_lat[1]) + [
            const((D_MODEL, D_MODEL)),
            _mod_spec(layer, 2, lat_seq),
            const((1, D_MODEL)),
            _mod_spec(layer, 3, lat_seq),
            _mod_spec(layer, 4, lat_seq),
            const((D_MODEL, LANES)), const((D_MODEL, LANES)),
        ],
        out_specs=[tile(D_MODEL), pl.BlockSpec((TM * ROW_TILE, LANES), lambda i: (i, 0)),
                   pl.BlockSpec((N_EXPERTS, TM), lambda i: (0, i))],
        out_shape=[jax.ShapeDtypeStruct((n_tok, D_MODEL), F32),
                   jax.ShapeDtypeStruct((n_tok * ROW_TILE, LANES), F32),
                   jax.ShapeDtypeStruct((N_EXPERTS, n_tok), F32)],
        compiler_params=_params("arbitrary"),
        name="post_attn",
    )(o_ctx, o_lat, x_ctx[0], x_lat[0], wo_bf16, mod3, norm_g.reshape(1, D_MODEL), mod3, mod3, wr_hi, wr_lo)


def _group_prefix(mask, tri):
    local, offs = [], []
    off = jnp.zeros((mask.shape[0], 1), F32)
    for g in range(mask.shape[1] // GROUP):
        xg = mask[:, g * GROUP:(g + 1) * GROUP]
        offs.append(off)
        local.append(_dot(xg.astype(BF16), tri))
        off = off + jnp.sum(xg, axis=1, keepdims=True)
    offs.append(off)
    return local, offs


def _split3_bf16(x):
    hi = x.astype(BF16)
    r1 = x - hi.astype(F32)
    mid = r1.astype(BF16)
    lo = (r1 - mid.astype(F32)).astype(BF16)
    return hi, mid, lo


def _plan_kernel(lg_ref, tri_ref, idx_ref, gc_ref, cnt_ref, affg_ref):
    lg = lg_ref[...]
    ex = jnp.exp(lg - jnp.max(lg, axis=0, keepdims=True))
    aff = ex / jnp.sum(ex, axis=0, keepdims=True)

    def count_ge(v):
        return jnp.sum(jnp.where(aff >= v, 1.0, 0.0), axis=1, keepdims=True)

    def search(i, thr):
        cand = thr | jnp.left_shift(jnp.int32(1), 30 - i)
        ok = (count_ge(lax.bitcast_convert_type(cand, F32)) >= CAP) & (cand >= F32_MIN_NORMAL_BITS)
        return jnp.where(ok, cand, thr)

    thr = lax.fori_loop(0, 31, search, jnp.zeros((N_EXPERTS, 1), I32))
    lo = lax.bitcast_convert_type(thr, F32)
    hi = lax.bitcast_convert_type(jnp.maximum(thr + 1, F32_MIN_NORMAL_BITS), F32)

    def refine(i, bounds):
        lo, hi = bounds
        mid = lo + (hi - lo) * 0.5
        ok = count_ge(mid) >= CAP
        return jnp.where(ok, mid, lo), jnp.where(ok, hi, mid)

    lo, hi = lax.fori_loop(0, 32, refine, (lo, hi))
    tri = tri_ref[...]
    gt = aff >= hi
    eq = jnp.where((aff >= lo) & (aff < hi), 1.0, 0.0)
    need = CAP - jnp.sum(jnp.where(gt, 1.0, 0.0), axis=1, keepdims=True)
    eq_local, eq_offs = _group_prefix(eq, tri)
    eq_rank = jnp.concatenate([eq_local[g] + eq_offs[g] for g in range(N_GROUPS)], axis=1)
    sel = jnp.where(gt | ((eq > 0.0) & (eq_rank < need)), 1.0, 0.0)
    sel_local, offs = _group_prefix(sel, tri)
    for g in range(N_GROUPS):
        sl = slice(g * GROUP, (g + 1) * GROUP)
        cnt_ref[g] = sel_local[g] + sel[:, sl]
        affg_ref[g] = aff[:, sl]

    lane = lax.broadcasted_iota(I32, (N_EXPERTS, LANES), 1)
    never = jnp.full((N_EXPERTS, LANES), 2.0 * N_TOK, F32)
    grp_lo, grp_hi = never, never
    for g in range(N_GROUPS):
        grp_lo = jnp.where(lane == g, offs[g], grp_lo)
        grp_hi = jnp.where(lane == g, offs[g + 1], grp_hi)
    row = lax.broadcasted_iota(I32, (CAP, LANES), 0).astype(F32)
    in_group_lane = lax.broadcasted_iota(I32, (CAP, GROUP), 1).astype(F32)
    zpad = jnp.zeros((LANES - N_GROUPS, GROUP), BF16)
    for e in range(N_EXPERTS):
        lo_row, hi_row = grp_lo[e:e + 1, :], grp_hi[e:e + 1, :]
        in_grp = (lo_row <= row) & (row < hi_row)
        onehot = jnp.where(in_grp, 1.0, 0.0).astype(BF16)
        counts = _dot(onehot, jnp.concatenate([cnt_ref[:, e, :].astype(BF16), zpad], axis=0))
        rank = row[:, 0:1] - jnp.sum(jnp.where(in_grp, lo_row, 0.0), axis=1, keepdims=True)
        local = jnp.sum(jnp.where(counts <= rank, 1.0, 0.0), axis=1, keepdims=True)
        grp = jnp.sum(jnp.where(hi_row <= row, 1.0, 0.0), axis=1, keepdims=True)
        tok = grp * GROUP + local
        aff_rows = sum(_dot(onehot, jnp.concatenate([part, zpad], axis=0))
                       for part in _split3_bf16(affg_ref[:, e, :]))
        gate = jnp.sum(jnp.where(in_group_lane == local, aff_rows, 0.0), axis=1, keepdims=True)
        gc_ref[0, e] = jnp.broadcast_to(gate, (CAP, LANES))
        tok_b = jnp.broadcast_to(tok, (CAP, LANES))
        tok_row = jnp.concatenate([tok_b[t * LANES:(t + 1) * LANES, :].T[0:1, :]
                                   for t in range(CAP // LANES)], axis=1)
        idx_ref[0, e:e + 1, :] = tok_row.astype(I32)


def _plan(logits_t, tri):
    ns = N_STREAMS
    return pl.pallas_call(
        _plan_kernel,
        grid=(ns,),
        in_specs=[pl.BlockSpec((N_EXPERTS, N_TOK), lambda s: (0, s)),
                  pl.BlockSpec((MXU_DIM, MXU_DIM), lambda s: (0, 0))],
        out_specs=[pl.BlockSpec((1, N_EXPERTS, CAP), lambda s: (s, 0, 0)),
                   pl.BlockSpec((1, N_EXPERTS, CAP, LANES), lambda s: (s, 0, 0, 0))],
        out_shape=[jax.ShapeDtypeStruct((ns, N_EXPERTS, CAP), I32),
                   jax.ShapeDtypeStruct((ns, N_EXPERTS, CAP, LANES), F32)],
        scratch_shapes=[pltpu.VMEM((N_GROUPS, N_EXPERTS, GROUP), F32),
                        pltpu.VMEM((N_GROUPS, N_EXPERTS, GROUP), F32)],
        compiler_params=_params("arbitrary"),
        name="plan",
    )(logits_t, tri)


def _tile_rows(r):
    return pl.ds(pl.multiple_of(r * ROW_TILE, ROW_TILE), ROW_TILE)


def _gather_kernel(idx_ref, h_ref, xe_ref, stage_ref):
    s, e = pl.program_id(0), pl.program_id(1)
    base = (s * N_EXPERTS + e) * CAP

    def move(g, carry):
        r0 = g * MOVE_UNROLL
        first = base + r0
        group = stage_ref.at[pl.ds(pl.multiple_of(r0 * ROW_TILE, MOVE_UNROLL * ROW_TILE),
                                   MOVE_UNROLL * ROW_TILE), :]
        for u in range(MOVE_UNROLL):
            group[u * ROW_TILE:(u + 1) * ROW_TILE, :] = h_ref[0, _tile_rows(idx_ref[first + u]), :]
        return carry

    lax.fori_loop(0, CAP // MOVE_UNROLL, move, 0)
    for k in range(ROW_TILE):
        xe_ref[0, 0, :, k * LANES:(k + 1) * LANES] = (
            stage_ref[pl.ds(k, CAP, stride=ROW_TILE), :].astype(BF16))


def _gather(idx_flat, h2_tiles):
    ns = h2_tiles.shape[0]
    return pl.pallas_call(
        _gather_kernel,
        grid_spec=pltpu.PrefetchScalarGridSpec(
            num_scalar_prefetch=1,
            grid=(ns, N_EXPERTS),
            in_specs=[pl.BlockSpec((1, N_TOK * ROW_TILE, LANES), lambda s, e, idx: (s, 0, 0),
                                   pipeline_mode=pl.Buffered(1))],
            out_specs=pl.BlockSpec((1, 1, CAP, D_MODEL), lambda s, e, idx: (s, e, 0, 0)),
            scratch_shapes=[pltpu.VMEM((CAP * ROW_TILE, LANES), F32)],
        ),
        out_shape=jax.ShapeDtypeStruct((ns, N_EXPERTS, CAP, D_MODEL), BF16),
        compiler_params=_params("arbitrary", "arbitrary"),
        name="moe_gather",
    )(idx_flat, h2_tiles)


def _ffn_kernel(x_ref, wg_ref, wu_ref, wd_ref, gc_ref, ye_ref, acc_ref, *, row_tile):
    f = pl.program_id(1)
    last = pl.num_programs(1) - 1
    ns = x_ref.shape[0]
    wg = wg_ref[0, 0].astype(BF16)
    wu = wu_ref[0, 0].astype(BF16)
    wd = wd_ref[0, 0].astype(BF16)

    def partial_out(s, r0):
        x = x_ref[s, 0, r0:r0 + row_tile, :]
        hid = _silu(_dot(x, wg)) * _dot(x, wu)
        return _dot(hid.astype(BF16), wd)

    tiles = [(s, r0) for s in range(ns) for r0 in range(0, CAP, row_tile)]

    @pl.when(f == 0)
    def _():
        for s, r0 in tiles:
            acc_ref[s, r0:r0 + row_tile, :] = partial_out(s, r0)

    @pl.when((f > 0) & (f < last))
    def _():
        for s, r0 in tiles:
            acc_ref[s, r0:r0 + row_tile, :] += partial_out(s, r0)

    @pl.when(f == last)
    def _():
        for s, r0 in tiles:
            ye = (acc_ref[s, r0:r0 + row_tile, :] + partial_out(s, r0)) * gc_ref[s, 0, r0:r0 + row_tile, 0:1]
            for k in range(ROW_TILE):
                ye_ref[s, 0, pl.ds(r0 * ROW_TILE + k, row_tile, stride=ROW_TILE), :] = (
                    ye[:, k * LANES:(k + 1) * LANES])


def _ffn(xe, gc, w_gate, w_up, w_down, layer):
    ns = xe.shape[0]
    return pl.pallas_call(
        functools.partial(_ffn_kernel, row_tile=512),
        grid=(N_EXPERTS, EXPERT_FF // FF_TILE),
        in_specs=[
            pl.BlockSpec((ns, 1, CAP, D_MODEL), lambda e, f: (0, e, 0, 0)),
            pl.BlockSpec((1, 1, D_MODEL, FF_TILE), lambda e, f: (layer, e, 0, f)),
            pl.BlockSpec((1, 1, D_MODEL, FF_TILE), lambda e, f: (layer, e, 0, f)),
            pl.BlockSpec((1, 1, FF_TILE, D_MODEL), lambda e, f: (layer, e, f, 0)),
            pl.BlockSpec((ns, 1, CAP, LANES), lambda e, f: (0, e, 0, 0)),
        ],
        out_specs=pl.BlockSpec((ns, 1, CAP * ROW_TILE, LANES), lambda e, f: (0, e, 0, 0)),
        out_shape=jax.ShapeDtypeStruct((ns, N_EXPERTS, CAP * ROW_TILE, LANES), F32),
        scratch_shapes=[pltpu.VMEM((ns, CAP, D_MODEL), F32)],
        compiler_params=_params("arbitrary", "arbitrary"),
        name="moe_ffn",
    )(xe, w_gate, w_up, w_down, gc)


def _scatter_kernel(idx_ref, ye_ref, y_ref, gate_ref, *refs, split):
    out_refs, acc_ref = refs[:-1], refs[-1]
    s, j = pl.program_id(0), pl.program_id(1)

    @pl.when(j == 0)
    def _():
        acc_ref[...] = jnp.zeros_like(acc_ref)

    @pl.when(j < N_EXPERTS)
    def _():
        base = (s * N_EXPERTS + j) * CAP

        def add_rows(g, carry):
            r0 = g * MOVE_UNROLL
            first = base + r0
            group = ye_ref.at[0, 0, pl.ds(pl.multiple_of(r0 * ROW_TILE, MOVE_UNROLL * ROW_TILE),
                                          MOVE_UNROLL * ROW_TILE), :]
            dst = [_tile_rows(idx_ref[first + u]) for u in range(MOVE_UNROLL)]
            val = [acc_ref[dst[u], :] + group[u * ROW_TILE:(u + 1) * ROW_TILE, :] for u in range(MOVE_UNROLL)]
            for u in range(MOVE_UNROLL):
                acc_ref[dst[u], :] = val[u]
            return carry

        lax.fori_loop(0, CAP // MOVE_UNROLL, add_rows, 0)

    @pl.when(j >= N_EXPERTS)
    def _():
        c = j - N_EXPERTS
        rows = FIN_TOK * ROW_TILE
        part = acc_ref.at[pl.ds(pl.multiple_of(c * rows, rows), rows), :]
        moe = jnp.concatenate([part[pl.ds(k, FIN_TOK, stride=ROW_TILE), :] for k in range(ROW_TILE)], axis=1)
        res = y_ref[...] + gate_ref[0] * moe
        if split:
            for si, out_ref in enumerate(out_refs):
                @pl.when(s == si)
                def _(out_ref=out_ref):
                    out_ref[...] = res
        else:
            out_refs[0][...] = res


def _scatter(idx_flat, ye_tiles, y, mod3, layer, lat_seq, split):
    ns = ye_tiles.shape[0]
    fin = lambda j: jnp.maximum(j - N_EXPERTS, 0)

    def gate_index(s, j, idx):
        row = _mod_row(s * N_FIN + fin(j), FIN_TOK, lat_seq)
        return ((layer * N_MOD_ROWS + row) * 6 + 5, 0, 0)

    chunk = (FIN_TOK, D_MODEL)
    if split:
        out_specs = [
            pl.BlockSpec(chunk, lambda s, j, idx, si=si: (
                jnp.where(s == si, fin(j), jnp.where(s < si, 0, N_FIN - 1)), 0))
            for si in range(ns)]
        out_shape = [jax.ShapeDtypeStruct((N_TOK, D_MODEL), F32)] * ns
    else:
        out_specs = pl.BlockSpec(chunk, lambda s, j, idx: (s * N_FIN + fin(j), 0))
        out_shape = jax.ShapeDtypeStruct(y.shape, F32)
    return pl.pallas_call(
        functools.partial(_scatter_kernel, split=split),
        grid_spec=pltpu.PrefetchScalarGridSpec(
            num_scalar_prefetch=1,
            grid=(ns, N_EXPERTS + N_FIN),
            in_specs=[
                pl.BlockSpec((1, 1, CAP * ROW_TILE, LANES),
                             lambda s, j, idx: (s, jnp.minimum(j, N_EXPERTS - 1), 0, 0)),
                pl.BlockSpec(chunk, lambda s, j, idx: (s * N_FIN + fin(j), 0)),
                pl.BlockSpec((1, 1, D_MODEL), gate_index),
            ],
            out_specs=out_specs,
            scratch_shapes=[pltpu.VMEM((N_TOK * ROW_TILE, LANES), F32)],
        ),
        out_shape=out_shape,
        compiler_params=_params("arbitrary", "arbitrary"),
        name="moe_scatter",
    )(idx_flat, ye_tiles, y, mod3)


def _rope_tables(n_tokens):
    t = np.arange(n_tokens)
    row = (t // GRID_W).astype(np.float32)
    col = (t % GRID_W).astype(np.float32)
    pairs = HEAD_DIM // 4
    inv_freq = ROPE_THETA ** (-jnp.arange(pairs, dtype=F32) / pairs)
    ang = jnp.concatenate([row[:, None] * inv_freq, col[:, None] * inv_freq], axis=-1)
    cos, sin = jnp.cos(ang), jnp.sin(ang)
    reps = MXU_DIM // HEAD_DIM
    cos_t = jnp.tile(jnp.concatenate([cos, cos], axis=-1), (1, reps))
    sin_t = jnp.tile(jnp.concatenate([-sin, sin], axis=-1), (1, reps))
    return (jnp.concatenate([cos_t, jnp.ones((TM, MXU_DIM), F32)], axis=0),
            jnp.concatenate([sin_t, jnp.zeros((TM, MXU_DIM), F32)], axis=0))


def _dup_cache(cache):
    b, s, hk, hd = cache.shape
    return jnp.broadcast_to(cache[:, :, :, None, :], (b, s, hk, 2, hd)).reshape(b, s, 2 * hk * hd).astype(BF16)


def kernel(x_prompt, x_sample, cache_attn_k, cache_attn_v, cache_na_k, cache_na_v, c, c_ctx,
           norm1_g, norm2_g, w_ada, b_ada, attn_w_qkv, attn_q_gain, attn_k_gain, attn_w_o,
           na_w_qkv, na_q_gain, na_k_gain, na_rpb, na_w_o,
           moe_w_router, moe_w_gate, moe_w_up, moe_w_down):
    bc, tc, _ = x_prompt.shape
    bl, tl, _ = x_sample.shape
    depth = w_ada.shape[0]
    assert bc * tc == N_TOK and bl * tl == N_TOK and 1 + bl <= N_MOD_ROWS
    assert tl % TM == 0 and tl % FIN_TOK == 0
    n_all = N_STREAMS * N_TOK

    cond = jnp.zeros((N_MOD_ROWS, D_MODEL), F32).at[0].set(c_ctx).at[1:1 + bl].set(c)
    mod3 = _ada(cond, w_ada, b_ada).reshape(depth * N_MOD_ROWS * 6, 1, D_MODEL)

    seg = jnp.asarray(np.kron(np.eye(MXU_DIM // HEAD_DIM), np.ones((HEAD_DIM, HEAD_DIM))), BF16)
    tri = jnp.asarray(np.triu(np.ones((MXU_DIM, MXU_DIM)), k=1), BF16)
    rope_tables = _rope_tables(tl)

    x_ctx = (x_prompt.reshape(N_TOK, D_MODEL), 0)
    x_lat = (x_sample.reshape(N_TOK, D_MODEL), 0)
    new_k, new_v = [], []
    for i in range(depth):
        j = i // 2
        gqa = i % 2 == 0
        if gqa:
            w_qkv, q_gain, k_gain, w_o = attn_w_qkv[j], attn_q_gain[j], attn_k_gain[j], attn_w_o[j]
            kv_heads = N_KV_HEADS
        else:
            w_qkv, q_gain, k_gain, w_o = na_w_qkv[j], na_q_gain[j], na_k_gain[j], na_w_o[j]
            kv_heads = N_HEADS
        wr_hi, wr_lo = _split_bf16(jnp.pad(moe_w_router[i], ((0, 0), (0, LANES - N_EXPERTS))))

        q, kb, vb, kf, vf = _qkv(x_ctx, x_lat, norm1_g[i], mod3, i, tl, w_qkv.astype(BF16), q_gain, k_gain,
                                 seg, kv_heads * HEAD_DIM, rope_tables if gqa else None)
        kvb = kb.shape[1]
        o_ctx = _attention(q.reshape(n_all // tc, tc, D_MODEL), kb.reshape(n_all // tc, tc, kvb),
                           vb.reshape(n_all // tc, tc, kvb), bc, 0)
        q_l, kb_l, vb_l = (a.reshape(n_all // tl, tl, a.shape[1]) for a in (q, kb, vb))
        if gqa:
            o_lat = _attention(q_l, kb_l, vb_l, bl, N_TOK // tl,
                               _dup_cache(cache_attn_k[:, j]), _dup_cache(cache_attn_v[:, j]))
        else:
            past = cache_na_k.shape[2]
            o_lat = _na_attention(q_l, kb_l, vb_l, bl, N_TOK // tl,
                                  cache_na_k[:, j].reshape(bl, past, D_MODEL).astype(BF16),
                                  cache_na_v[:, j].reshape(bl, past, D_MODEL).astype(BF16),
                                  _na_bias_table(na_rpb[j]))
        new_k.append(kf.reshape(bc, 1, tc, kv_heads, HEAD_DIM))
        new_v.append(vf.reshape(bc, 1, tc, kv_heads, HEAD_DIM))

        y, h2, logits_t = _post_attention(o_ctx.reshape(N_TOK, D_MODEL), o_lat.reshape(N_TOK, D_MODEL),
                                          x_ctx, x_lat, w_o.astype(BF16), norm2_g[i], mod3, i, tl,
                                          wr_hi, wr_lo)
        idx, gc = _plan(logits_t, tri)
        idx_flat = idx.reshape(-1)
        xe = _gather(idx_flat, h2.reshape(N_STREAMS, N_TOK * ROW_TILE, LANES))
        ye = _ffn(xe, gc, moe_w_gate, moe_w_up, moe_w_down, i)
        y = _scatter(idx_flat, ye, y, mod3, i, tl, split=(i == depth - 1))
        x_ctx, x_lat = (y, 0), (y, CTX_TILES)

    y_ctx, y_lat = y
    return (y_ctx.reshape(bc, tc, D_MODEL), y_lat.reshape(bl, tl, D_MODEL),
            jnp.concatenate(new_k[0::2], axis=1), jnp.concatenate(new_v[0::2], axis=1),
            jnp.concatenate(new_k[1::2], axis=1), jnp.concatenate(new_v[1::2], axis=1))
```

```python
import functools

import jax
import jax.numpy as jnp
import numpy as np
from jax import lax
from jax.experimental import pallas as pl
from jax.experimental.pallas import tpu as pltpu

F32 = jnp.float32
BF16 = jnp.bfloat16
I32 = jnp.int32

D_MODEL = 1024
N_HEADS = 16
N_KV_HEADS = 4
HEAD_DIM = 64
GRID_W = 64
WIN_R = 8
WIN_C = 16
N_EXPERTS = 16
EXPERT_FF = 2048
ROPE_THETA = 10000.0
RMS_EPS = 1e-6
NEG_INF = -1e30
F32_MIN_NORMAL_BITS = 0x00800000
LOG2_E = 1.4426950408889634
Q_SCALE = HEAD_DIM ** -0.5 * LOG2_E

LANES = 128
MXU_DIM = 256
VMEM_LIMIT = 56 * 1024 * 1024

N_STREAMS = 2
N_TOK = 8192
CAP = 2 * N_TOK // N_EXPERTS
TM = 512
N_MOD_ROWS = 16
ROW_TILE = D_MODEL // LANES
GROUP = MXU_DIM
N_GROUPS = N_TOK // GROUP
FIN_TOK = 512
N_FIN = N_TOK // FIN_TOK
MOVE_UNROLL = 8
FF_TILE = 512
NA_QBLK = 256
NA_QROWS = NA_QBLK // GRID_W
NA_BAND = WIN_R + NA_QROWS
NA_MAX_SHIFT = WIN_R - 2 + NA_QROWS
NA_TILES = NA_MAX_SHIFT + 1
N_REL_R = 2 * WIN_R - 1
N_REL_C = 2 * WIN_C - 1


def _params(*sem):
    return pltpu.CompilerParams(dimension_semantics=sem, vmem_limit_bytes=VMEM_LIMIT)


def _dot(a, b):
    return jnp.dot(a, b, preferred_element_type=F32)


def _dot_nt(a, b):
    return lax.dot_general(a, b, (((1,), (1,)), ((), ())), preferred_element_type=F32)


def _split_bf16(x):
    hi = x.astype(BF16)
    lo = (x - hi.astype(F32)).astype(BF16)
    return hi, lo


def _silu(x):
    return x * (1.0 / (1.0 + jnp.exp(-x)))


def _ada_kernel(cond_ref, w_ref, b_ref, out_ref):
    sx = _silu(cond_ref[...])
    xh, xl = _split_bf16(sx)
    wh, wl = _split_bf16(w_ref[0])
    out_ref[0] = _dot(xh, wh) + _dot(xl, wh) + _dot(xh, wl) + b_ref[0]


def _ada(cond, w_ada, b_ada):
    depth = w_ada.shape[0]
    tn = 1024
    n_out = w_ada.shape[2]
    return pl.pallas_call(
        _ada_kernel,
        grid=(depth, n_out // tn),
        in_specs=[
            pl.BlockSpec((N_MOD_ROWS, D_MODEL), lambda l, n: (0, 0)),
            pl.BlockSpec((1, D_MODEL, tn), lambda l, n: (l, 0, n)),
            pl.BlockSpec((1, 1, tn), lambda l, n: (l, 0, n)),
        ],
        out_specs=pl.BlockSpec((1, N_MOD_ROWS, tn), lambda l, n: (l, 0, n)),
        out_shape=jax.ShapeDtypeStruct((depth, N_MOD_ROWS, n_out), F32),
        compiler_params=_params("arbitrary", "arbitrary"),
        name="ada",
    )(cond, w_ada, b_ada.reshape(depth, 1, n_out))


def _mod_row(tile, tile_rows, lat_seq):
    ctx_tiles = N_TOK // tile_rows
    return jnp.where(tile < ctx_tiles, 0, 1 + (tile - ctx_tiles) // (lat_seq // tile_rows))


def _mod_spec(layer, which, lat_seq, tile_of=lambda i: i):
    def index(i):
        return ((layer * N_MOD_ROWS + _mod_row(tile_of(i), TM, lat_seq)) * 6 + which, 0, 0)
    return pl.BlockSpec((1, 1, D_MODEL), index)


CTX_TILES = N_TOK // TM


def _x_specs(x_ctx_block0, x_lat_block0, tile_of=lambda i: i):
    return [
        pl.BlockSpec((TM, D_MODEL), lambda i: (x_ctx_block0 + jnp.minimum(tile_of(i), CTX_TILES - 1), 0)),
        pl.BlockSpec((TM, D_MODEL), lambda i: (x_lat_block0 + jnp.maximum(tile_of(i) - CTX_TILES, 0), 0)),
    ]


def _head_norm(z, seg, gain):
    ms = _dot((z * z).astype(BF16), seg)
    return z * lax.rsqrt(ms + RMS_EPS) * gain


def _rope(z, cos_t, sin_t):
    lane = lax.broadcasted_iota(I32, z.shape, 1)
    first = (lane & 32) == 0
    n = z.shape[1]
    partner = jnp.where(first, pltpu.roll(z, n - 32, axis=1), pltpu.roll(z, 32, axis=1))
    return z * cos_t + partner * sin_t


def _dup_heads(z):
    outs = []
    for b in range(z.shape[1] // LANES):
        x = z[:, b * LANES:(b + 1) * LANES]
        xr = pltpu.roll(x, HEAD_DIM, axis=1)
        lo = lax.broadcasted_iota(I32, x.shape, 1) < HEAD_DIM
        outs.append(jnp.where(lo, x, xr))
        outs.append(jnp.where(lo, xr, x))
    return jnp.concatenate(outs, axis=1)


def _qkv_kernel(*refs, kv_width, rope):
    it = iter(refs)
    xc_ref, xl_ref, g_ref, shift_ref, scale_ref, w_ref, qg_ref, kg_ref, seg_ref = (next(it) for _ in range(9))
    cos_ref = sin_ref = None
    if rope:
        cos_ref, sin_ref = next(it), next(it)
    q_ref, kb_ref, vb_ref, kh_ref, vh_ref, kf_ref, vf_ref = (next(it) for _ in range(7))
    dup = kv_width == N_KV_HEADS * HEAD_DIM
    is_ctx = _qkv_tile(pl.program_id(0)) < CTX_TILES

    x = jnp.where(is_ctx, xc_ref[...], xl_ref[...])
    ms = jnp.mean(x * x, axis=-1, keepdims=True)
    h = x * lax.rsqrt(ms + RMS_EPS) * (g_ref[...] * (1.0 + scale_ref[0])) + shift_ref[0]
    hb = h.astype(BF16)
    seg = seg_ref[...]
    if rope:
        cos_t, sin_t = cos_ref[...], sin_ref[...]

    q_width = N_HEADS * HEAD_DIM
    for cidx in range(q_width // MXU_DIM):
        c0 = cidx * MXU_DIM
        z = _head_norm(_dot(hb, w_ref[:, c0:c0 + MXU_DIM]), seg, qg_ref[...])
        if rope:
            z = _rope(z, cos_t, sin_t)
        q_ref[:, c0:c0 + MXU_DIM] = z.astype(BF16)
    for cidx in range(kv_width // MXU_DIM):
        c0 = cidx * MXU_DIM
        z = _head_norm(_dot(hb, w_ref[:, q_width + c0:q_width + c0 + MXU_DIM]), seg, kg_ref[...])
        kf_ref[:, c0:c0 + MXU_DIM] = z
        if rope:
            z = _rope(z, cos_t, sin_t)
        v = _dot(hb, w_ref[:, q_width + kv_width + c0:q_width + kv_width + c0 + MXU_DIM])
        vf_ref[:, c0:c0 + MXU_DIM] = v
        if dup:
            kb_ref[:, 2 * c0:2 * c0 + 2 * MXU_DIM] = _dup_heads(z).astype(BF16)
            vb_ref[:, 2 * c0:2 * c0 + 2 * MXU_DIM] = _dup_heads(v).astype(BF16)
        else:
            kb_ref[:, c0:c0 + MXU_DIM] = z.astype(BF16)
            vb_ref[:, c0:c0 + MXU_DIM] = v.astype(BF16)

    @pl.when(is_ctx)
    def _():
        n_heads = kv_width // HEAD_DIM
        for src_ref, dst_ref in ((kf_ref, kh_ref), (vf_ref, vh_ref)):
            for b in range(kv_width // LANES):
                pair = src_ref[:, b * LANES:(b + 1) * LANES]
                swapped = pltpu.roll(pair, HEAD_DIM, axis=1)
                for hh, val in enumerate((pair, swapped)):
                    dst_ref[pl.ds(2 * b + hh, TM, stride=n_heads), :] = val[:, :HEAD_DIM]


def _qkv_tile(i):
    return (i + CTX_TILES) % (N_STREAMS * CTX_TILES)


def _qkv(x_ctx, x_lat, norm_g, mod3, layer, lat_seq, w_bf16, q_gain, k_gain, seg, kv_width, rope_tables):
    n_tok = N_STREAMS * N_TOK
    n_w = w_bf16.shape[1]
    kvb_width = 2 * kv_width if kv_width == N_KV_HEADS * HEAD_DIM else kv_width
    tile = lambda w: pl.BlockSpec((TM, w), lambda i: (_qkv_tile(i), 0))
    ctx_heads = pl.BlockSpec((TM * (kv_width // HEAD_DIM), HEAD_DIM),
                             lambda i: (jnp.maximum(i - CTX_TILES, 0), 0))
    const = lambda shape: pl.BlockSpec(shape, lambda i: (0,) * len(shape))
    in_specs = _x_specs(x_ctx[1], x_lat[1], _qkv_tile) + [
        const((1, D_MODEL)),
        _mod_spec(layer, 0, lat_seq, _qkv_tile),
        _mod_spec(layer, 1, lat_seq, _qkv_tile),
        const((D_MODEL, n_w)),
        const((1, MXU_DIM)),
        const((1, MXU_DIM)),
        const((MXU_DIM, MXU_DIM)),
    ]
    args = [x_ctx[0], x_lat[0], norm_g.reshape(1, D_MODEL), mod3, mod3, w_bf16,
            jnp.tile(q_gain * Q_SCALE, MXU_DIM // HEAD_DIM).reshape(1, MXU_DIM),
            jnp.tile(k_gain, MXU_DIM // HEAD_DIM).reshape(1, MXU_DIM), seg]
    if rope_tables is not None:
        seq_tiles = lat_seq // TM
        rope_index = lambda i: (jnp.where(_qkv_tile(i) < CTX_TILES, seq_tiles, _qkv_tile(i) % seq_tiles), 0)
        in_specs += [pl.BlockSpec((TM, MXU_DIM), rope_index)] * 2
        args += list(rope_tables)
    n_heads = kv_width // HEAD_DIM
    out_specs = [tile(D_MODEL), tile(kvb_width), tile(kvb_width), ctx_heads, ctx_heads]
    out_shape = [jax.ShapeDtypeStruct((n_tok, D_MODEL), BF16),
                 jax.ShapeDtypeStruct((n_tok, kvb_width), BF16),
                 jax.ShapeDtypeStruct((n_tok, kvb_width), BF16),
                 jax.ShapeDtypeStruct((N_TOK * n_heads, HEAD_DIM), F32),
                 jax.ShapeDtypeStruct((N_TOK * n_heads, HEAD_DIM), F32)]
    return pl.pallas_call(
        functools.partial(_qkv_kernel, kv_width=kv_width, rope=rope_tables is not None),
        grid=(n_tok // TM,),
        in_specs=in_specs,
        out_specs=out_specs,
        out_shape=out_shape,
        scratch_shapes=[pltpu.VMEM((TM, kv_width), F32)] * 2,
        compiler_params=_params("arbitrary"),
        name="qkv",
    )(*args)


def _two_head_rows(qj):
    lo = lax.broadcasted_iota(I32, qj.shape, 1) < HEAD_DIM
    zero = jnp.zeros_like(qj)
    return jnp.concatenate([jnp.where(lo, qj, zero), jnp.where(lo, zero, qj)], axis=0)


def _merge_two_heads(r):
    tq = r.shape[0] // 2
    lo = lax.broadcasted_iota(I32, (tq, LANES), 1) < HEAD_DIM
    return jnp.where(lo, r[:tq], r[tq:])


def _attn_kernel(*refs, n_kv_blocks, has_ctx):
    if has_ctx:
        q_ref, k_ref, v_ref, ck_ref, cv_ref, o_ref = refs
    else:
        q_ref, k_ref, v_ref, o_ref = refs
    n_q_blocks = D_MODEL // LANES
    for j in range(n_q_blocks):
        kb = (j * n_kv_blocks) // n_q_blocks
        ksl = slice(kb * LANES, (kb + 1) * LANES)
        q2 = _two_head_rows(q_ref[0, :, j * LANES:(j + 1) * LANES])
        s = _dot_nt(q2, k_ref[0, :, ksl])
        m = jnp.max(s, axis=-1, keepdims=True)
        if has_ctx:
            sc = _dot_nt(q2, ck_ref[0, :, ksl])
            m = jnp.maximum(m, jnp.max(sc, axis=-1, keepdims=True))
        p = jnp.exp2(s - m)
        l = jnp.sum(p, axis=-1, keepdims=True)
        r = _dot(p.astype(BF16), v_ref[0, :, ksl])
        if has_ctx:
            pc = jnp.exp2(sc - m)
            l = l + jnp.sum(pc, axis=-1, keepdims=True)
            r = r + _dot(pc.astype(BF16), cv_ref[0, :, ksl])
        r = r / l
        o_ref[0, :, j * LANES:(j + 1) * LANES] = _merge_two_heads(r).astype(BF16)


def _attention(q, k, v, n_batch, batch0, ctx_k=None, ctx_v=None, tq=256):
    t = q.shape[1]
    s, w = k.shape[1], k.shape[2]
    has_ctx = ctx_k is not None
    in_specs = [
        pl.BlockSpec((1, tq, D_MODEL), lambda bi, qi: (bi + batch0, qi, 0)),
        pl.BlockSpec((1, s, w), lambda bi, qi: (bi + batch0, 0, 0)),
        pl.BlockSpec((1, s, w), lambda bi, qi: (bi + batch0, 0, 0)),
    ]
    args = [q, k, v]
    if has_ctx:
        sc = ctx_k.shape[1]
        in_specs += [pl.BlockSpec((1, sc, w), lambda bi, qi: (bi, 0, 0))] * 2
        args += [ctx_k, ctx_v]
    return pl.pallas_call(
        functools.partial(_attn_kernel, n_kv_blocks=w // LANES, has_ctx=has_ctx),
        grid=(n_batch, t // tq),
        in_specs=in_specs,
        out_specs=pl.BlockSpec((1, tq, D_MODEL), lambda bi, qi: (bi, qi, 0)),
        out_shape=jax.ShapeDtypeStruct((n_batch, t, D_MODEL), BF16),
        compiler_params=_params("arbitrary", "arbitrary"),
        name="attn",
    )(*args)


def _na_kernel(q_ref, k_ref, v_ref, ck_ref, cv_ref, bias_ref, o_ref, *, rows):
    blk = pl.program_id(1)
    q_row0 = NA_QROWS * blk
    band_start = jnp.minimum(jnp.clip(q_row0 - WIN_R // 2, 0, rows - WIN_R), rows - NA_BAND)
    n_band = NA_BAND * GRID_W
    band = pl.ds(pl.multiple_of(band_start * GRID_W, LANES), n_band)
    shape = (NA_QBLK, n_band)
    q_r = q_row0 + lax.broadcasted_iota(I32, shape, 0) // GRID_W
    k_r = band_start + lax.broadcasted_iota(I32, shape, 1) // GRID_W
    q_rs = jnp.clip(q_r - WIN_R // 2, 0, rows - WIN_R)
    row_ok1 = (k_r >= q_rs) & (k_r < q_rs + WIN_R)
    row_ok = jnp.concatenate([row_ok1, row_ok1], axis=0)
    shift0 = (band_start - q_row0 + NA_MAX_SHIFT) // 2
    for j in range(D_MODEL // LANES):
        sl = slice(j * LANES, (j + 1) * LANES)
        q2 = _two_head_rows(q_ref[0, :, sl])
        s = _dot_nt(q2, k_ref[0, band, sl])
        bias = jnp.concatenate(
            [jnp.concatenate([bias_ref[2 * j + hh, shift0 + m - u] for m in range(NA_BAND // 2)], axis=1)
             for hh in range(2) for u in range(NA_QROWS // 2)], axis=0)
        s = jnp.where(row_ok, s + bias, NEG_INF)
        sc = _dot_nt(q2, ck_ref[0, :, sl])
        m = jnp.maximum(jnp.max(s, axis=-1, keepdims=True), jnp.max(sc, axis=-1, keepdims=True))
        p = jnp.exp2(s - m)
        pc = jnp.exp2(sc - m)
        l = jnp.sum(p, axis=-1, keepdims=True) + jnp.sum(pc, axis=-1, keepdims=True)
        r = _dot(p.astype(BF16), v_ref[0, band, sl]) + _dot(pc.astype(BF16), cv_ref[0, :, sl])
        r = r / l
        o_ref[0, :, sl] = _merge_two_heads(r).astype(BF16)


def _na_bias_kernel(rpb_ref, out_ref, tz_ref):
    h = pl.program_id(0)
    shape = (GRID_W, LANES)
    q_col = lax.broadcasted_iota(I32, shape, 0)
    lane = lax.broadcasted_iota(I32, shape, 1)
    k_col = lane & (GRID_W - 1)
    col_start = jnp.clip(q_col - WIN_C // 2, 0, GRID_W - WIN_C)
    col_ok = (k_col >= col_start) & (k_col < col_start + WIN_C)
    rel_c = jnp.clip(k_col - q_col + WIN_C - 1, 0, N_REL_C - 1)
    for a in range(N_REL_R):
        acc = jnp.zeros(shape, F32)
        for b in range(N_REL_C):
            acc = jnp.where(rel_c == b, rpb_ref[(h * N_REL_R + a) * N_REL_C + b], acc)
        tz_ref[a] = jnp.where(col_ok, acc * LOG2_E, NEG_INF)
    left = lane < GRID_W
    for dd in range(NA_TILES):
        for qr in range(2):
            rel = [min(max(2 * dd - NA_MAX_SHIFT + kr - qr + WIN_R - 1, 0), N_REL_R - 1) for kr in range(2)]
            out_ref[0, dd, qr * GRID_W:(qr + 1) * GRID_W, :] = jnp.where(left, tz_ref[rel[0]], tz_ref[rel[1]])


def _na_bias_table(rpb):
    return pl.pallas_call(
        _na_bias_kernel,
        grid=(N_HEADS,),
        in_specs=[pl.BlockSpec(memory_space=pltpu.SMEM)],
        out_specs=pl.BlockSpec((1, NA_TILES, LANES, LANES), lambda h: (h, 0, 0, 0)),
        out_shape=jax.ShapeDtypeStruct((N_HEADS, NA_TILES, LANES, LANES), F32),
        scratch_shapes=[pltpu.VMEM((N_REL_R, GRID_W, LANES), F32)],
        compiler_params=_params("arbitrary"),
        name="na_bias",
    )(rpb.reshape(-1))


def _na_attention(q, k, v, n_batch, batch0, ctx_k, ctx_v, bias_tbl):
    t = q.shape[1]
    sc = ctx_k.shape[1]
    rows = t // GRID_W
    full = lambda n, off: pl.BlockSpec((1, n, D_MODEL), lambda bi, qi: (bi + off, 0, 0))
    return pl.pallas_call(
        functools.partial(_na_kernel, rows=rows),
        grid=(n_batch, t // NA_QBLK),
        in_specs=[
            pl.BlockSpec((1, NA_QBLK, D_MODEL), lambda bi, qi: (bi + batch0, qi, 0)),
            full(t, batch0), full(t, batch0), full(sc, 0), full(sc, 0),
            pl.BlockSpec(bias_tbl.shape, lambda bi, qi: (0, 0, 0, 0)),
        ],
        out_specs=pl.BlockSpec((1, NA_QBLK, D_MODEL), lambda bi, qi: (bi, qi, 0)),
        out_shape=jax.ShapeDtypeStruct((n_batch, t, D_MODEL), BF16),
        compiler_params=_params("arbitrary", "arbitrary"),
        name="na_attn",
    )(q, k, v, ctx_k, ctx_v, bias_tbl)


def _post_kernel(oc_ref, ol_ref, xc_ref, xl_ref, wo_ref, gate_ref, g_ref, shift_ref, scale_ref,
                 wrh_ref, wrl_ref, y_ref, h_ref, lg_ref):
    is_ctx = pl.program_id(0) < CTX_TILES
    o = jnp.where(is_ctx, oc_ref[...], ol_ref[...])
    x = jnp.where(is_ctx, xc_ref[...], xl_ref[...])
    y = x + gate_ref[0] * _dot(o, wo_ref[...])
    y_ref[...] = y
    ms = jnp.mean(y * y, axis=-1, keepdims=True)
    h = y * lax.rsqrt(ms + RMS_EPS) * (g_ref[...] * (1.0 + scale_ref[0])) + shift_ref[0]
    for k in range(ROW_TILE):
        h_ref[pl.ds(k, TM, stride=ROW_TILE), :] = h[:, k * LANES:(k + 1) * LANES]
    hh, hl = _split_bf16(h)
    lg_ref[...] = _dot_nt(wrh_ref[...], hh) + _dot_nt(wrh_ref[...], hl) + _dot_nt(wrl_ref[...], hh)


def _post_attention(o_ctx, o_lat, x_ctx, x_lat, wo_bf16, norm_g, mod3, layer, lat_seq, wr_hi, wr_lo):
    n_tok = N_STREAMS * N_TOK
    tile = lambda w: pl.BlockSpec((TM, w), lambda i: (i, 0))
    const = lambda shape: pl.BlockSpec(shape, lambda i: (0,) * len(shape))
    return pl.pallas_call(
        _post_kernel,
        grid=(n_tok // TM,),
        in_specs=_x_specs(0, 0) + _x_specs(x_ctx[1], x_lat[1]) + [
            const((D_MODEL, D_MODEL)),
            _mod_spec(layer, 2, lat_seq),
            const((1, D_MODEL)),
            _mod_spec(layer, 3, lat_seq),
            _mod_spec(layer, 4, lat_seq),
            const((N_EXPERTS, D_MODEL)), const((N_EXPERTS, D_MODEL)),
        ],
        out_specs=[tile(D_MODEL), pl.BlockSpec((TM * ROW_TILE, LANES), lambda i: (i, 0)),
                   pl.BlockSpec((N_EXPERTS, TM), lambda i: (0, i))],
        out_shape=[jax.ShapeDtypeStruct((n_tok, D_MODEL), F32),
                   jax.ShapeDtypeStruct((n_tok * ROW_TILE, LANES), F32),
                   jax.ShapeDtypeStruct((N_EXPERTS, n_tok), F32)],
        compiler_params=_params("arbitrary"),
        name="post_attn",
    )(o_ctx, o_lat, x_ctx[0], x_lat[0], wo_bf16, mod3, norm_g.reshape(1, D_MODEL), mod3, mod3, wr_hi, wr_lo)


def _group_prefix(mask, tri):
    local, offs = [], []
    off = jnp.zeros((mask.shape[0], 1), F32)
    for g in range(mask.shape[1] // GROUP):
        xg = mask[:, g * GROUP:(g + 1) * GROUP]
        offs.append(off)
        local.append(_dot(xg.astype(BF16), tri))
        off = off + jnp.sum(xg, axis=1, keepdims=True)
    offs.append(off)
    return local, offs


def _split3_bf16(x):
    hi = x.astype(BF16)
    r1 = x - hi.astype(F32)
    mid = r1.astype(BF16)
    lo = (r1 - mid.astype(F32)).astype(BF16)
    return hi, mid, lo


def _plan_kernel(lg_ref, tri_ref, idx_ref, gc_ref, cnt_ref, affg_ref):
    lg = lg_ref[...]
    ex = jnp.exp(lg - jnp.max(lg, axis=0, keepdims=True))
    aff = ex / jnp.sum(ex, axis=0, keepdims=True)

    def count_ge(v):
        return jnp.sum(jnp.where(aff >= v, 1.0, 0.0), axis=1, keepdims=True)

    def search(i, thr):
        cand = thr | jnp.left_shift(jnp.int32(1), 30 - i)
        ok = (count_ge(lax.bitcast_convert_type(cand, F32)) >= CAP) & (cand >= F32_MIN_NORMAL_BITS)
        return jnp.where(ok, cand, thr)

    thr = lax.fori_loop(0, 31, search, jnp.zeros((N_EXPERTS, 1), I32))
    lo = lax.bitcast_convert_type(thr, F32)
    hi = lax.bitcast_convert_type(jnp.maximum(thr + 1, F32_MIN_NORMAL_BITS), F32)

    def refine(i, bounds):
        lo, hi = bounds
        mid = lo + (hi - lo) * 0.5
        ok = count_ge(mid) >= CAP
        return jnp.where(ok, mid, lo), jnp.where(ok, hi, mid)

    lo, hi = lax.fori_loop(0, 32, refine, (lo, hi))
    tri = tri_ref[...]
    gt = aff >= hi
    eq = jnp.where((aff >= lo) & (aff < hi), 1.0, 0.0)
    need = CAP - jnp.sum(jnp.where(gt, 1.0, 0.0), axis=1, keepdims=True)
    eq_local, eq_offs = _group_prefix(eq, tri)
    eq_rank = jnp.concatenate([eq_local[g] + eq_offs[g] for g in range(N_GROUPS)], axis=1)
    sel = jnp.where(gt | ((eq > 0.0) & (eq_rank < need)), 1.0, 0.0)
    sel_local, offs = _group_prefix(sel, tri)
    for g in range(N_GROUPS):
        sl = slice(g * GROUP, (g + 1) * GROUP)
        cnt_ref[g] = sel_local[g] + sel[:, sl]
        affg_ref[g] = aff[:, sl]

    lane = lax.broadcasted_iota(I32, (N_EXPERTS, LANES), 1)
    never = jnp.full((N_EXPERTS, LANES), 2.0 * N_TOK, F32)
    grp_lo, grp_hi = never, never
    for g in range(N_GROUPS):
        grp_lo = jnp.where(lane == g, offs[g], grp_lo)
        grp_hi = jnp.where(lane == g, offs[g + 1], grp_hi)
    row = lax.broadcasted_iota(I32, (CAP, LANES), 0).astype(F32)
    in_group_lane = lax.broadcasted_iota(I32, (CAP, GROUP), 1).astype(F32)
    zpad = jnp.zeros((LANES - N_GROUPS, GROUP), BF16)
    for e in range(N_EXPERTS):
        lo_row, hi_row = grp_lo[e:e + 1, :], grp_hi[e:e + 1, :]
        in_grp = (lo_row <= row) & (row < hi_row)
        onehot = jnp.where(in_grp, 1.0, 0.0).astype(BF16)
        counts = _dot(onehot, jnp.concatenate([cnt_ref[:, e, :].astype(BF16), zpad], axis=0))
        rank = row[:, 0:1] - jnp.sum(jnp.where(in_grp, lo_row, 0.0), axis=1, keepdims=True)
        local = jnp.sum(jnp.where(counts <= rank, 1.0, 0.0), axis=1, keepdims=True)
        grp = jnp.sum(jnp.where(hi_row <= row, 1.0, 0.0), axis=1, keepdims=True)
        tok = grp * GROUP + local
        aff_rows = sum(_dot(onehot, jnp.concatenate([part, zpad], axis=0))
                       for part in _split3_bf16(affg_ref[:, e, :]))
        gate = jnp.sum(jnp.where(in_group_lane == local, aff_rows, 0.0), axis=1, keepdims=True)
        gc_ref[0, e] = jnp.broadcast_to(gate, (CAP, LANES))
        tok_b = jnp.broadcast_to(tok, (CAP, LANES))
        tok_row = jnp.concatenate([tok_b[t * LANES:(t + 1) * LANES, :].T[0:1, :]
                                   for t in range(CAP // LANES)], axis=1)
        idx_ref[0, e:e + 1, :] = tok_row.astype(I32)


def _plan(logits_t, tri):
    ns = N_STREAMS
    return pl.pallas_call(
        _plan_kernel,
        grid=(ns,),
        in_specs=[pl.BlockSpec((N_EXPERTS, N_TOK), lambda s: (0, s)),
                  pl.BlockSpec((MXU_DIM, MXU_DIM), lambda s: (0, 0))],
        out_specs=[pl.BlockSpec((1, N_EXPERTS, CAP), lambda s: (s, 0, 0)),
                   pl.BlockSpec((1, N_EXPERTS, CAP, LANES), lambda s: (s, 0, 0, 0))],
        out_shape=[jax.ShapeDtypeStruct((ns, N_EXPERTS, CAP), I32),
                   jax.ShapeDtypeStruct((ns, N_EXPERTS, CAP, LANES), F32)],
        scratch_shapes=[pltpu.VMEM((N_GROUPS, N_EXPERTS, GROUP), F32),
                        pltpu.VMEM((N_GROUPS, N_EXPERTS, GROUP), F32)],
        compiler_params=_params("arbitrary"),
        name="plan",
    )(logits_t, tri)


def _tile_rows(r):
    return pl.ds(pl.multiple_of(r * ROW_TILE, ROW_TILE), ROW_TILE)


def _gather_kernel(idx_ref, h_ref, xe_ref, stage_ref):
    s, e = pl.program_id(0), pl.program_id(1)
    base = (s * N_EXPERTS + e) * CAP

    def move(g, carry):
        for u in range(MOVE_UNROLL):
            r = g * MOVE_UNROLL + u
            stage_ref[_tile_rows(r), :] = h_ref[0, _tile_rows(idx_ref[base + r]), :]
        return carry

    lax.fori_loop(0, CAP // MOVE_UNROLL, move, 0)
    for k in range(ROW_TILE):
        xe_ref[0, 0, :, k * LANES:(k + 1) * LANES] = (
            stage_ref[pl.ds(k, CAP, stride=ROW_TILE), :].astype(BF16))


def _gather(idx_flat, h2_tiles):
    ns = h2_tiles.shape[0]
    return pl.pallas_call(
        _gather_kernel,
        grid_spec=pltpu.PrefetchScalarGridSpec(
            num_scalar_prefetch=1,
            grid=(ns, N_EXPERTS),
            in_specs=[pl.BlockSpec((1, N_TOK * ROW_TILE, LANES), lambda s, e, idx: (s, 0, 0),
                                   pipeline_mode=pl.Buffered(1))],
            out_specs=pl.BlockSpec((1, 1, CAP, D_MODEL), lambda s, e, idx: (s, e, 0, 0)),
            scratch_shapes=[pltpu.VMEM((CAP * ROW_TILE, LANES), F32)],
        ),
        out_shape=jax.ShapeDtypeStruct((ns, N_EXPERTS, CAP, D_MODEL), BF16),
        compiler_params=_params("arbitrary", "arbitrary"),
        name="moe_gather",
    )(idx_flat, h2_tiles)


def _ffn_kernel(x_ref, wg_ref, wu_ref, wd_ref, gc_ref, ye_ref, acc_ref, *, row_tile):
    f = pl.program_id(1)
    last = pl.num_programs(1) - 1
    ns = x_ref.shape[0]
    wg = wg_ref[0, 0].astype(BF16)
    wu = wu_ref[0, 0].astype(BF16)
    wd = wd_ref[0, 0].astype(BF16)

    def partial_out(s, r0):
        x = x_ref[s, 0, r0:r0 + row_tile, :]
        hid = _silu(_dot(x, wg)) * _dot(x, wu)
        return _dot(hid.astype(BF16), wd)

    tiles = [(s, r0) for s in range(ns) for r0 in range(0, CAP, row_tile)]

    @pl.when(f == 0)
    def _():
        for s, r0 in tiles:
            acc_ref[s, r0:r0 + row_tile, :] = partial_out(s, r0)

    @pl.when((f > 0) & (f < last))
    def _():
        for s, r0 in tiles:
            acc_ref[s, r0:r0 + row_tile, :] += partial_out(s, r0)

    @pl.when(f == last)
    def _():
        for s, r0 in tiles:
            ye = (acc_ref[s, r0:r0 + row_tile, :] + partial_out(s, r0)) * gc_ref[s, 0, r0:r0 + row_tile, 0:1]
            for k in range(ROW_TILE):
                ye_ref[s, 0, pl.ds(r0 * ROW_TILE + k, row_tile, stride=ROW_TILE), :] = (
                    ye[:, k * LANES:(k + 1) * LANES])


def _ffn(xe, gc, w_gate, w_up, w_down, layer):
    ns = xe.shape[0]
    return pl.pallas_call(
        functools.partial(_ffn_kernel, row_tile=512),
        grid=(N_EXPERTS, EXPERT_FF // FF_TILE),
        in_specs=[
            pl.BlockSpec((ns, 1, CAP, D_MODEL), lambda e, f: (0, e, 0, 0)),
            pl.BlockSpec((1, 1, D_MODEL, FF_TILE), lambda e, f: (layer, e, 0, f)),
            pl.BlockSpec((1, 1, D_MODEL, FF_TILE), lambda e, f: (layer, e, 0, f)),
            pl.BlockSpec((1, 1, FF_TILE, D_MODEL), lambda e, f: (layer, e, f, 0)),
            pl.BlockSpec((ns, 1, CAP, LANES), lambda e, f: (0, e, 0, 0)),
        ],
        out_specs=pl.BlockSpec((ns, 1, CAP * ROW_TILE, LANES), lambda e, f: (0, e, 0, 0)),
        out_shape=jax.ShapeDtypeStruct((ns, N_EXPERTS, CAP * ROW_TILE, LANES), F32),
        scratch_shapes=[pltpu.VMEM((ns, CAP, D_MODEL), F32)],
        compiler_params=_params("arbitrary", "arbitrary"),
        name="moe_ffn",
    )(xe, w_gate, w_up, w_down, gc)


def _scatter_kernel(idx_ref, ye_ref, y_ref, gate_ref, *refs, split):
    out_refs, acc_ref = refs[:-1], refs[-1]
    s, j = pl.program_id(0), pl.program_id(1)

    @pl.when(j == 0)
    def _():
        acc_ref[...] = jnp.zeros_like(acc_ref)

    @pl.when(j < N_EXPERTS)
    def _():
        base = (s * N_EXPERTS + j) * CAP

        def add_rows(g, carry):
            r0 = g * MOVE_UNROLL
            first = base + r0
            group = ye_ref.at[0, 0, pl.ds(pl.multiple_of(r0 * ROW_TILE, MOVE_UNROLL * ROW_TILE),
                                          MOVE_UNROLL * ROW_TILE), :]
            dst = [_tile_rows(idx_ref[first + u]) for u in range(MOVE_UNROLL)]
            val = [acc_ref[dst[u], :] + group[u * ROW_TILE:(u + 1) * ROW_TILE, :] for u in range(MOVE_UNROLL)]
            for u in range(MOVE_UNROLL):
                acc_ref[dst[u], :] = val[u]
            return carry

        lax.fori_loop(0, CAP // MOVE_UNROLL, add_rows, 0)

    @pl.when(j >= N_EXPERTS)
    def _():
        c = j - N_EXPERTS
        rows = FIN_TOK * ROW_TILE
        part = acc_ref.at[pl.ds(pl.multiple_of(c * rows, rows), rows), :]
        moe = jnp.concatenate([part[pl.ds(k, FIN_TOK, stride=ROW_TILE), :] for k in range(ROW_TILE)], axis=1)
        res = y_ref[...] + gate_ref[0] * moe
        if split:
            for si, out_ref in enumerate(out_refs):
                @pl.when(s == si)
                def _(out_ref=out_ref):
                    out_ref[...] = res
        else:
            out_refs[0][...] = res


def _scatter(idx_flat, ye_tiles, y, mod3, layer, lat_seq, split):
    ns = ye_tiles.shape[0]
    fin = lambda j: jnp.maximum(j - N_EXPERTS, 0)

    def gate_index(s, j, idx):
        row = _mod_row(s * N_FIN + fin(j), FIN_TOK, lat_seq)
        return ((layer * N_MOD_ROWS + row) * 6 + 5, 0, 0)

    chunk = (FIN_TOK, D_MODEL)
    if split:
        out_specs = [
            pl.BlockSpec(chunk, lambda s, j, idx, si=si: (
                jnp.where(s == si, fin(j), jnp.where(s < si, 0, N_FIN - 1)), 0))
            for si in range(ns)]
        out_shape = [jax.ShapeDtypeStruct((N_TOK, D_MODEL), F32)] * ns
    else:
        out_specs = pl.BlockSpec(chunk, lambda s, j, idx: (s * N_FIN + fin(j), 0))
        out_shape = jax.ShapeDtypeStruct(y.shape, F32)
    return pl.pallas_call(
        functools.partial(_scatter_kernel, split=split),
        grid_spec=pltpu.PrefetchScalarGridSpec(
            num_scalar_prefetch=1,
            grid=(ns, N_EXPERTS + N_FIN),
            in_specs=[
                pl.BlockSpec((1, 1, CAP * ROW_TILE, LANES),
                             lambda s, j, idx: (s, jnp.minimum(j, N_EXPERTS - 1), 0, 0)),
                pl.BlockSpec(chunk, lambda s, j, idx: (s * N_FIN + fin(j), 0)),
                pl.BlockSpec((1, 1, D_MODEL), gate_index),
            ],
            out_specs=out_specs,
            scratch_shapes=[pltpu.VMEM((N_TOK * ROW_TILE, LANES), F32)],
        ),
        out_shape=out_shape,
        compiler_params=_params("arbitrary", "arbitrary"),
        name="moe_scatter",
    )(idx_flat, ye_tiles, y, mod3)


def _rope_tables(n_tokens):
    t = np.arange(n_tokens)
    row = (t // GRID_W).astype(np.float32)
    col = (t % GRID_W).astype(np.float32)
    pairs = HEAD_DIM // 4
    inv_freq = ROPE_THETA ** (-jnp.arange(pairs, dtype=F32) / pairs)
    ang = jnp.concatenate([row[:, None] * inv_freq, col[:, None] * inv_freq], axis=-1)
    cos, sin = jnp.cos(ang), jnp.sin(ang)
    reps = MXU_DIM // HEAD_DIM
    cos_t = jnp.tile(jnp.concatenate([cos, cos], axis=-1), (1, reps))
    sin_t = jnp.tile(jnp.concatenate([-sin, sin], axis=-1), (1, reps))
    return (jnp.concatenate([cos_t, jnp.ones((TM, MXU_DIM), F32)], axis=0),
            jnp.concatenate([sin_t, jnp.zeros((TM, MXU_DIM), F32)], axis=0))


def _dup_cache(cache):
    b, s, hk, hd = cache.shape
    return jnp.broadcast_to(cache[:, :, :, None, :], (b, s, hk, 2, hd)).reshape(b, s, 2 * hk * hd).astype(BF16)


def kernel(x_prompt, x_sample, cache_attn_k, cache_attn_v, cache_na_k, cache_na_v, c, c_ctx,
           norm1_g, norm2_g, w_ada, b_ada, attn_w_qkv, attn_q_gain, attn_k_gain, attn_w_o,
           na_w_qkv, na_q_gain, na_k_gain, na_rpb, na_w_o,
           moe_w_router, moe_w_gate, moe_w_up, moe_w_down):
    bc, tc, _ = x_prompt.shape
    bl, tl, _ = x_sample.shape
    depth = w_ada.shape[0]
    assert bc * tc == N_TOK and bl * tl == N_TOK and 1 + bl <= N_MOD_ROWS
    assert tl % TM == 0 and tl % FIN_TOK == 0
    n_all = N_STREAMS * N_TOK

    cond = jnp.zeros((N_MOD_ROWS, D_MODEL), F32).at[0].set(c_ctx).at[1:1 + bl].set(c)
    mod3 = _ada(cond, w_ada, b_ada).reshape(depth * N_MOD_ROWS * 6, 1, D_MODEL)

    seg = jnp.asarray(np.kron(np.eye(MXU_DIM // HEAD_DIM), np.ones((HEAD_DIM, HEAD_DIM))) / HEAD_DIM, BF16)
    tri = jnp.asarray(np.triu(np.ones((MXU_DIM, MXU_DIM)), k=1), BF16)
    rope_tables = _rope_tables(tl)

    x_ctx = (x_prompt.reshape(N_TOK, D_MODEL), 0)
    x_lat = (x_sample.reshape(N_TOK, D_MODEL), 0)
    new_k, new_v = [], []
    for i in range(depth):
        j = i // 2
        gqa = i % 2 == 0
        if gqa:
            w_qkv, q_gain, k_gain, w_o = attn_w_qkv[j], attn_q_gain[j], attn_k_gain[j], attn_w_o[j]
            kv_heads = N_KV_HEADS
        else:
            w_qkv, q_gain, k_gain, w_o = na_w_qkv[j], na_q_gain[j], na_k_gain[j], na_w_o[j]
            kv_heads = N_HEADS
        wr_hi, wr_lo = _split_bf16(moe_w_router[i].T)

        q, kb, vb, kf, vf = _qkv(x_ctx, x_lat, norm1_g[i], mod3, i, tl, w_qkv.astype(BF16), q_gain, k_gain,
                                 seg, kv_heads * HEAD_DIM, rope_tables if gqa else None)
        kvb = kb.shape[1]
        o_ctx = _attention(q.reshape(n_all // tc, tc, D_MODEL), kb.reshape(n_all // tc, tc, kvb),
                           vb.reshape(n_all // tc, tc, kvb), bc, 0)
        q_l, kb_l, vb_l = (a.reshape(n_all // tl, tl, a.shape[1]) for a in (q, kb, vb))
        if gqa:
            o_lat = _attention(q_l, kb_l, vb_l, bl, N_TOK // tl,
                               _dup_cache(cache_attn_k[:, j]), _dup_cache(cache_attn_v[:, j]))
        else:
            past = cache_na_k.shape[2]
            o_lat = _na_attention(q_l, kb_l, vb_l, bl, N_TOK // tl,
                                  cache_na_k[:, j].reshape(bl, past, D_MODEL).astype(BF16),
                                  cache_na_v[:, j].reshape(bl, past, D_MODEL).astype(BF16),
                                  _na_bias_table(na_rpb[j]))
        new_k.append(kf.reshape(bc, 1, tc, kv_heads, HEAD_DIM))
        new_v.append(vf.reshape(bc, 1, tc, kv_heads, HEAD_DIM))

        y, h2, logits_t = _post_attention(o_ctx.reshape(N_TOK, D_MODEL), o_lat.reshape(N_TOK, D_MODEL),
                                          x_ctx, x_lat, w_o.astype(BF16), norm2_g[i], mod3, i, tl,
                                          wr_hi, wr_lo)
        idx, gc = _plan(logits_t, tri)
        idx_flat = idx.reshape(-1)
        xe = _gather(idx_flat, h2.reshape(N_STREAMS, N_TOK * ROW_TILE, LANES))
        ye = _ffn(xe, gc, moe_w_gate, moe_w_up, moe_w_down, i)
        y = _scatter(idx_flat, ye, y, mod3, i, tl, split=(i == depth - 1))
        x_ctx, x_lat = (y, 0), (y, CTX_TILES)

    y_ctx, y_lat = y
    return (y_ctx.reshape(bc, tc, D_MODEL), y_lat.reshape(bl, tl, D_MODEL),
            jnp.concatenate(new_k[0::2], axis=1), jnp.concatenate(new_v[0::2], axis=1),
            jnp.concatenate(new_k[1::2], axis=1), jnp.concatenate(new_v[1::2], axis=1))
```

```python
import functools

import jax
import jax.numpy as jnp
import numpy as np
from jax import lax
from jax.experimental import pallas as pl
from jax.experimental.pallas import tpu as pltpu

F32 = jnp.float32
BF16 = jnp.bfloat16
I32 = jnp.int32

D_MODEL = 1024
N_HEADS = 16
N_KV_HEADS = 4
HEAD_DIM = 64
GRID_W = 64
WIN_R = 8
WIN_C = 16
N_EXPERTS = 16
EXPERT_FF = 2048
ROPE_THETA = 10000.0
RMS_EPS = 1e-6
NEG_INF = -1e30
F32_MIN_NORMAL_BITS = 0x00800000
LOG2_E = 1.4426950408889634
Q_SCALE = HEAD_DIM ** -0.5 * LOG2_E

LANES = 128
MXU_DIM = 256
VMEM_LIMIT = 56 * 1024 * 1024

N_STREAMS = 2
N_TOK = 8192
CAP = 2 * N_TOK // N_EXPERTS
TM = 512
N_MOD_ROWS = 16
ROW_TILE = D_MODEL // LANES
GROUP = MXU_DIM
N_GROUPS = N_TOK // GROUP
FIN_TOK = 512
N_FIN = N_TOK // FIN_TOK
MOVE_UNROLL = 8
FF_TILE = 512
NA_QBLK = 256
NA_QROWS = NA_QBLK // GRID_W
NA_BAND = WIN_R + NA_QROWS
NA_MAX_SHIFT = WIN_R - 2 + NA_QROWS
NA_TILES = NA_MAX_SHIFT + 1
N_REL_R = 2 * WIN_R - 1
N_REL_C = 2 * WIN_C - 1


def _params(*sem):
    return pltpu.CompilerParams(dimension_semantics=sem, vmem_limit_bytes=VMEM_LIMIT)


def _dot(a, b):
    return jnp.dot(a, b, preferred_element_type=F32)


def _dot_nt(a, b):
    return lax.dot_general(a, b, (((1,), (1,)), ((), ())), preferred_element_type=F32)


def _split_bf16(x):
    hi = x.astype(BF16)
    lo = (x - hi.astype(F32)).astype(BF16)
    return hi, lo


def _silu(x):
    return x * (1.0 / (1.0 + jnp.exp(-x)))


def _ada_kernel(cond_ref, w_ref, b_ref, out_ref):
    sx = _silu(cond_ref[...])
    xh, xl = _split_bf16(sx)
    wh, wl = _split_bf16(w_ref[0])
    out_ref[0] = _dot(xh, wh) + _dot(xl, wh) + _dot(xh, wl) + b_ref[0]


def _ada(cond, w_ada, b_ada):
    depth = w_ada.shape[0]
    tn = 1024
    n_out = w_ada.shape[2]
    return pl.pallas_call(
        _ada_kernel,
        grid=(depth, n_out // tn),
        in_specs=[
            pl.BlockSpec((N_MOD_ROWS, D_MODEL), lambda l, n: (0, 0)),
            pl.BlockSpec((1, D_MODEL, tn), lambda l, n: (l, 0, n)),
            pl.BlockSpec((1, 1, tn), lambda l, n: (l, 0, n)),
        ],
        out_specs=pl.BlockSpec((1, N_MOD_ROWS, tn), lambda l, n: (l, 0, n)),
        out_shape=jax.ShapeDtypeStruct((depth, N_MOD_ROWS, n_out), F32),
        compiler_params=_params("arbitrary", "arbitrary"),
        name="ada",
    )(cond, w_ada, b_ada.reshape(depth, 1, n_out))


def _mod_row(tile, tile_rows, lat_seq):
    ctx_tiles = N_TOK // tile_rows
    return jnp.where(tile < ctx_tiles, 0, 1 + (tile - ctx_tiles) // (lat_seq // tile_rows))


def _mod_spec(layer, which, lat_seq, tile_of=lambda i: i):
    def index(i):
        return ((layer * N_MOD_ROWS + _mod_row(tile_of(i), TM, lat_seq)) * 6 + which, 0, 0)
    return pl.BlockSpec((1, 1, D_MODEL), index)


CTX_TILES = N_TOK // TM


def _x_specs(x_ctx_block0, x_lat_block0, tile_of=lambda i: i):
    return [
        pl.BlockSpec((TM, D_MODEL), lambda i: (x_ctx_block0 + jnp.minimum(tile_of(i), CTX_TILES - 1), 0)),
        pl.BlockSpec((TM, D_MODEL), lambda i: (x_lat_block0 + jnp.maximum(tile_of(i) - CTX_TILES, 0), 0)),
    ]


def _head_norm(z, seg, gain):
    ms = _dot((z * z).astype(BF16), seg)
    return z * lax.rsqrt(ms + RMS_EPS) * gain


def _rope(z, cos_t, sin_t):
    lane = lax.broadcasted_iota(I32, z.shape, 1)
    first = (lane & 32) == 0
    n = z.shape[1]
    partner = jnp.where(first, pltpu.roll(z, n - 32, axis=1), pltpu.roll(z, 32, axis=1))
    return z * cos_t + partner * sin_t


def _dup_heads(z):
    outs = []
    for b in range(z.shape[1] // LANES):
        x = z[:, b * LANES:(b + 1) * LANES]
        xr = pltpu.roll(x, HEAD_DIM, axis=1)
        lo = lax.broadcasted_iota(I32, x.shape, 1) < HEAD_DIM
        outs.append(jnp.where(lo, x, xr))
        outs.append(jnp.where(lo, xr, x))
    return jnp.concatenate(outs, axis=1)


def _store_head_rows(dst_ref, c0, z, n_heads):
    for b in range(z.shape[1] // LANES):
        pair = z[:, b * LANES:(b + 1) * LANES]
        swapped = pltpu.roll(pair, HEAD_DIM, axis=1)
        for hh, val in enumerate((pair, swapped)):
            head = c0 // HEAD_DIM + 2 * b + hh
            dst_ref[pl.ds(head, TM, stride=n_heads), :] = val[:, :HEAD_DIM]


def _qkv_kernel(*refs, kv_width, rope, emit_heads):
    it = iter(refs)
    x_ref, g_ref, shift_ref, scale_ref, w_ref, qg_ref, kg_ref, seg_ref = (next(it) for _ in range(8))
    cos_ref = sin_ref = None
    if rope:
        cos_ref, sin_ref = next(it), next(it)
    q_ref, kb_ref, vb_ref = (next(it) for _ in range(3))
    kh_ref = vh_ref = None
    if emit_heads:
        kh_ref, vh_ref = next(it), next(it)
    dup = kv_width == N_KV_HEADS * HEAD_DIM
    n_heads = kv_width // HEAD_DIM

    x = x_ref[...]
    ms = jnp.mean(x * x, axis=-1, keepdims=True)
    h = x * lax.rsqrt(ms + RMS_EPS) * (g_ref[...] * (1.0 + scale_ref[0])) + shift_ref[0]
    hb = h.astype(BF16)
    seg = seg_ref[...]
    if rope:
        cos_t, sin_t = cos_ref[...], sin_ref[...]

    q_width = N_HEADS * HEAD_DIM

    def finish_q(c0, raw):
        z = _head_norm(raw, seg, qg_ref[...])
        if rope:
            z = _rope(z, cos_t, sin_t)
        q_ref[:, c0:c0 + MXU_DIM] = z.astype(BF16)

    def finish_k(c0, raw):
        z = _head_norm(raw, seg, kg_ref[...])
        if emit_heads:
            _store_head_rows(kh_ref, c0, z, n_heads)
        if rope:
            z = _rope(z, cos_t, sin_t)
        if dup:
            kb_ref[:, 2 * c0:2 * c0 + 2 * MXU_DIM] = _dup_heads(z).astype(BF16)
        else:
            kb_ref[:, c0:c0 + MXU_DIM] = z.astype(BF16)

    def finish_v(c0, raw):
        if emit_heads:
            _store_head_rows(vh_ref, c0, raw, n_heads)
        if dup:
            vb_ref[:, 2 * c0:2 * c0 + 2 * MXU_DIM] = _dup_heads(raw).astype(BF16)
        else:
            vb_ref[:, c0:c0 + MXU_DIM] = raw.astype(BF16)

    q_chunks = [(finish_q, c0, c0) for c0 in range(0, q_width, MXU_DIM)]
    k_chunks = [(finish_k, c0, q_width + c0) for c0 in range(0, kv_width, MXU_DIM)]
    v_chunks = [(finish_v, c0, q_width + kv_width + c0) for c0 in range(0, kv_width, MXU_DIM)]
    normed = q_chunks + k_chunks
    chunks = []
    for i, chunk in enumerate(normed):
        chunks.append(chunk)
        if i * len(v_chunks) // len(normed) != (i + 1) * len(v_chunks) // len(normed):
            chunks.append(v_chunks[i * len(v_chunks) // len(normed)])

    project = lambda col: _dot(hb, w_ref[:, col:col + MXU_DIM])
    raw = project(chunks[0][2])
    for i, (finish, c0, _) in enumerate(chunks):
        nxt = project(chunks[i + 1][2]) if i + 1 < len(chunks) else None
        finish(c0, raw)
        raw = nxt

def _qkv(x_src, is_ctx, norm_g, mod3, layer, lat_seq, w_bf16, q_gain, k_gain, seg, kv_width, rope_tables):
    n_w = w_bf16.shape[1]
    kvb_width = 2 * kv_width if kv_width == N_KV_HEADS * HEAD_DIM else kv_width
    n_heads = kv_width // HEAD_DIM
    tile_of = (lambda i: i) if is_ctx else (lambda i: i + CTX_TILES)
    tile = lambda w: pl.BlockSpec((TM, w), lambda i: (i, 0))
    const = lambda shape: pl.BlockSpec(shape, lambda i: (0,) * len(shape))
    in_specs = [
        pl.BlockSpec((TM, D_MODEL), lambda i: (x_src[1] + i, 0)),
        const((1, D_MODEL)),
        _mod_spec(layer, 0, lat_seq, tile_of),
        _mod_spec(layer, 1, lat_seq, tile_of),
        const((D_MODEL, n_w)),
        const((1, MXU_DIM)),
        const((1, MXU_DIM)),
        const((MXU_DIM, MXU_DIM)),
    ]
    args = [x_src[0], norm_g.reshape(1, D_MODEL), mod3, mod3, w_bf16,
            jnp.tile(q_gain * Q_SCALE, MXU_DIM // HEAD_DIM).reshape(1, MXU_DIM),
            jnp.tile(k_gain, MXU_DIM // HEAD_DIM).reshape(1, MXU_DIM), seg]
    if rope_tables is not None:
        seq_tiles = lat_seq // TM
        in_specs += [pl.BlockSpec((TM, MXU_DIM), lambda i: (i % seq_tiles, 0))] * 2
        args += list(rope_tables)
    out_specs = [tile(D_MODEL), tile(kvb_width), tile(kvb_width)]
    out_shape = [jax.ShapeDtypeStruct((N_TOK, D_MODEL), BF16),
                 jax.ShapeDtypeStruct((N_TOK, kvb_width), BF16),
                 jax.ShapeDtypeStruct((N_TOK, kvb_width), BF16)]
    if is_ctx:
        out_specs += [pl.BlockSpec((TM * n_heads, HEAD_DIM), lambda i: (i, 0))] * 2
        out_shape += [jax.ShapeDtypeStruct((N_TOK * n_heads, HEAD_DIM), F32)] * 2
    return pl.pallas_call(
        functools.partial(_qkv_kernel, kv_width=kv_width, rope=rope_tables is not None, emit_heads=is_ctx),
        grid=(N_TOK // TM,),
        in_specs=in_specs,
        out_specs=out_specs,
        out_shape=out_shape,
        compiler_params=_params("arbitrary"),
        name="qkv",
    )(*args)


def _two_head_rows(qj):
    lo = lax.broadcasted_iota(I32, qj.shape, 1) < HEAD_DIM
    zero = jnp.zeros_like(qj)
    return jnp.concatenate([jnp.where(lo, qj, zero), jnp.where(lo, zero, qj)], axis=0)


def _merge_two_heads(r):
    tq = r.shape[0] // 2
    lo = lax.broadcasted_iota(I32, (tq, LANES), 1) < HEAD_DIM
    return jnp.where(lo, r[:tq], r[tq:])


def _attn_kernel(*refs, n_kv_blocks, has_ctx):
    if has_ctx:
        q_ref, k_ref, v_ref, ck_ref, cv_ref, o_ref = refs
    else:
        q_ref, k_ref, v_ref, o_ref = refs
    n_q_blocks = D_MODEL // LANES
    key_lanes = lambda j: slice(((j * n_kv_blocks) // n_q_blocks) * LANES,
                                ((j * n_kv_blocks) // n_q_blocks + 1) * LANES)

    def scores(j):
        q2 = _two_head_rows(q_ref[0, :, j * LANES:(j + 1) * LANES])
        s = _dot_nt(q2, k_ref[0, :, key_lanes(j)])
        sc = _dot_nt(q2, ck_ref[0, :, key_lanes(j)]) if has_ctx else None
        return s, sc

    nxt = scores(0)
    for j in range(n_q_blocks):
        (s, sc), ksl = nxt, key_lanes(j)
        if j + 1 < n_q_blocks:
            nxt = scores(j + 1)
        m = jnp.max(s, axis=-1, keepdims=True)
        if has_ctx:
            m = jnp.maximum(m, jnp.max(sc, axis=-1, keepdims=True))
        p = jnp.exp2(s - m)
        l = jnp.sum(p, axis=-1, keepdims=True)
        r = _dot(p.astype(BF16), v_ref[0, :, ksl])
        if has_ctx:
            pc = jnp.exp2(sc - m)
            l = l + jnp.sum(pc, axis=-1, keepdims=True)
            r = r + _dot(pc.astype(BF16), cv_ref[0, :, ksl])
        r = r / l
        o_ref[0, :, j * LANES:(j + 1) * LANES] = _merge_two_heads(r).astype(BF16)


def _attention(q, k, v, n_batch, batch0, ctx_k=None, ctx_v=None, tq=256):
    t = q.shape[1]
    s, w = k.shape[1], k.shape[2]
    has_ctx = ctx_k is not None
    in_specs = [
        pl.BlockSpec((1, tq, D_MODEL), lambda bi, qi: (bi + batch0, qi, 0)),
        pl.BlockSpec((1, s, w), lambda bi, qi: (bi + batch0, 0, 0)),
        pl.BlockSpec((1, s, w), lambda bi, qi: (bi + batch0, 0, 0)),
    ]
    args = [q, k, v]
    if has_ctx:
        sc = ctx_k.shape[1]
        in_specs += [pl.BlockSpec((1, sc, w), lambda bi, qi: (bi, 0, 0))] * 2
        args += [ctx_k, ctx_v]
    return pl.pallas_call(
        functools.partial(_attn_kernel, n_kv_blocks=w // LANES, has_ctx=has_ctx),
        grid=(n_batch, t // tq),
        in_specs=in_specs,
        out_specs=pl.BlockSpec((1, tq, D_MODEL), lambda bi, qi: (bi, qi, 0)),
        out_shape=jax.ShapeDtypeStruct((n_batch, t, D_MODEL), BF16),
        compiler_params=_params("arbitrary", "arbitrary"),
        name="attn",
    )(*args)


def _na_kernel(q_ref, k_ref, v_ref, ck_ref, cv_ref, bias_ref, o_ref, *, rows):
    blk = pl.program_id(1)
    q_row0 = NA_QROWS * blk
    band_start = jnp.minimum(jnp.clip(q_row0 - WIN_R // 2, 0, rows - WIN_R), rows - NA_BAND)
    n_band = NA_BAND * GRID_W
    band = pl.ds(pl.multiple_of(band_start * GRID_W, LANES), n_band)
    shape = (NA_QBLK, n_band)
    q_r = q_row0 + lax.broadcasted_iota(I32, shape, 0) // GRID_W
    k_r = band_start + lax.broadcasted_iota(I32, shape, 1) // GRID_W
    q_rs = jnp.clip(q_r - WIN_R // 2, 0, rows - WIN_R)
    row_ok1 = (k_r >= q_rs) & (k_r < q_rs + WIN_R)
    row_ok = jnp.concatenate([row_ok1, row_ok1], axis=0)
    shift0 = (band_start - q_row0 + NA_MAX_SHIFT) // 2
    n_q_blocks = D_MODEL // LANES

    def scores(j):
        sl = slice(j * LANES, (j + 1) * LANES)
        q2 = _two_head_rows(q_ref[0, :, sl])
        return _dot_nt(q2, k_ref[0, band, sl]), _dot_nt(q2, ck_ref[0, :, sl])

    nxt = scores(0)
    for j in range(n_q_blocks):
        sl = slice(j * LANES, (j + 1) * LANES)
        s, sc = nxt
        if j + 1 < n_q_blocks:
            nxt = scores(j + 1)
        bias = jnp.concatenate(
            [jnp.concatenate([bias_ref[2 * j + hh, shift0 + m - u] for m in range(NA_BAND // 2)], axis=1)
             for hh in range(2) for u in range(NA_QROWS // 2)], axis=0)
        s = jnp.where(row_ok, s + bias, NEG_INF)
        m = jnp.maximum(jnp.max(s, axis=-1, keepdims=True), jnp.max(sc, axis=-1, keepdims=True))
        p = jnp.exp2(s - m)
        pc = jnp.exp2(sc - m)
        l = jnp.sum(p, axis=-1, keepdims=True) + jnp.sum(pc, axis=-1, keepdims=True)
        r = _dot(p.astype(BF16), v_ref[0, band, sl]) + _dot(pc.astype(BF16), cv_ref[0, :, sl])
        r = r / l
        o_ref[0, :, sl] = _merge_two_heads(r).astype(BF16)


def _na_bias_kernel(rpb_ref, out_ref, tz_ref):
    h = pl.program_id(0)
    shape = (GRID_W, LANES)
    q_col = lax.broadcasted_iota(I32, shape, 0)
    lane = lax.broadcasted_iota(I32, shape, 1)
    k_col = lane & (GRID_W - 1)
    col_start = jnp.clip(q_col - WIN_C // 2, 0, GRID_W - WIN_C)
    col_ok = (k_col >= col_start) & (k_col < col_start + WIN_C)
    rel_c = jnp.clip(k_col - q_col + WIN_C - 1, 0, N_REL_C - 1)
    for a in range(N_REL_R):
        acc = jnp.zeros(shape, F32)
        for b in range(N_REL_C):
            acc = jnp.where(rel_c == b, rpb_ref[(h * N_REL_R + a) * N_REL_C + b], acc)
        tz_ref[a] = jnp.where(col_ok, acc * LOG2_E, NEG_INF)
    left = lane < GRID_W
    for dd in range(NA_TILES):
        for qr in range(2):
            rel = [min(max(2 * dd - NA_MAX_SHIFT + kr - qr + WIN_R - 1, 0), N_REL_R - 1) for kr in range(2)]
            out_ref[0, dd, qr * GRID_W:(qr + 1) * GRID_W, :] = jnp.where(left, tz_ref[rel[0]], tz_ref[rel[1]])


def _na_bias_table(rpb):
    return pl.pallas_call(
        _na_bias_kernel,
        grid=(N_HEADS,),
        in_specs=[pl.BlockSpec(memory_space=pltpu.SMEM)],
        out_specs=pl.BlockSpec((1, NA_TILES, LANES, LANES), lambda h: (h, 0, 0, 0)),
        out_shape=jax.ShapeDtypeStruct((N_HEADS, NA_TILES, LANES, LANES), F32),
        scratch_shapes=[pltpu.VMEM((N_REL_R, GRID_W, LANES), F32)],
        compiler_params=_params("arbitrary"),
        name="na_bias",
    )(rpb.reshape(-1))


def _na_attention(q, k, v, n_batch, batch0, ctx_k, ctx_v, bias_tbl):
    t = q.shape[1]
    sc = ctx_k.shape[1]
    rows = t // GRID_W
    full = lambda n, off: pl.BlockSpec((1, n, D_MODEL), lambda bi, qi: (bi + off, 0, 0))
    return pl.pallas_call(
        functools.partial(_na_kernel, rows=rows),
        grid=(n_batch, t // NA_QBLK),
        in_specs=[
            pl.BlockSpec((1, NA_QBLK, D_MODEL), lambda bi, qi: (bi + batch0, qi, 0)),
            full(t, batch0), full(t, batch0), full(sc, 0), full(sc, 0),
            pl.BlockSpec(bias_tbl.shape, lambda bi, qi: (0, 0, 0, 0)),
        ],
        out_specs=pl.BlockSpec((1, NA_QBLK, D_MODEL), lambda bi, qi: (bi, qi, 0)),
        out_shape=jax.ShapeDtypeStruct((n_batch, t, D_MODEL), BF16),
        compiler_params=_params("arbitrary", "arbitrary"),
        name="na_attn",
    )(q, k, v, ctx_k, ctx_v, bias_tbl)


def _post_kernel(oc_ref, ol_ref, xc_ref, xl_ref, wo_ref, gate_ref, g_ref, shift_ref, scale_ref,
                 wrh_ref, wrl_ref, y_ref, h_ref, lg_ref):
    is_ctx = pl.program_id(0) < CTX_TILES
    o = jnp.where(is_ctx, oc_ref[...], ol_ref[...])
    x = jnp.where(is_ctx, xc_ref[...], xl_ref[...])
    y = x + gate_ref[0] * _dot(o, wo_ref[...])
    y_ref[...] = y
    ms = jnp.mean(y * y, axis=-1, keepdims=True)
    h = y * lax.rsqrt(ms + RMS_EPS) * (g_ref[...] * (1.0 + scale_ref[0])) + shift_ref[0]
    for k in range(ROW_TILE):
        h_ref[pl.ds(k, TM, stride=ROW_TILE), :] = h[:, k * LANES:(k + 1) * LANES]
    hh, hl = _split_bf16(h)
    lg_ref[...] = _dot_nt(wrh_ref[...], hh) + _dot_nt(wrh_ref[...], hl) + _dot_nt(wrl_ref[...], hh)


def _post_attention(o_ctx, o_lat, x_ctx, x_lat, wo_bf16, norm_g, mod3, layer, lat_seq, wr_hi, wr_lo):
    n_tok = N_STREAMS * N_TOK
    tile = lambda w: pl.BlockSpec((TM, w), lambda i: (i, 0))
    const = lambda shape: pl.BlockSpec(shape, lambda i: (0,) * len(shape))
    return pl.pallas_call(
        _post_kernel,
        grid=(n_tok // TM,),
        in_specs=_x_specs(0, 0) + _x_specs(x_ctx[1], x_lat[1]) + [
            const((D_MODEL, D_MODEL)),
            _mod_spec(layer, 2, lat_seq),
            const((1, D_MODEL)),
            _mod_spec(layer, 3, lat_seq),
            _mod_spec(layer, 4, lat_seq),
            const((N_EXPERTS, D_MODEL)), const((N_EXPERTS, D_MODEL)),
        ],
        out_specs=[tile(D_MODEL), pl.BlockSpec((TM * ROW_TILE, LANES), lambda i: (i, 0)),
                   pl.BlockSpec((N_EXPERTS, TM), lambda i: (0, i))],
        out_shape=[jax.ShapeDtypeStruct((n_tok, D_MODEL), F32),
                   jax.ShapeDtypeStruct((n_tok * ROW_TILE, LANES), F32),
                   jax.ShapeDtypeStruct((N_EXPERTS, n_tok), F32)],
        compiler_params=_params("arbitrary"),
        name="post_attn",
    )(o_ctx, o_lat, x_ctx[0], x_lat[0], wo_bf16, mod3, norm_g.reshape(1, D_MODEL), mod3, mod3, wr_hi, wr_lo)


def _group_prefix(mask, tri):
    local, offs = [], []
    off = jnp.zeros((mask.shape[0], 1), F32)
    for g in range(mask.shape[1] // GROUP):
        xg = mask[:, g * GROUP:(g + 1) * GROUP]
        offs.append(off)
        local.append(_dot(xg.astype(BF16), tri))
        off = off + jnp.sum(xg, axis=1, keepdims=True)
    offs.append(off)
    return local, offs


def _split3_bf16(x):
    hi = x.astype(BF16)
    r1 = x - hi.astype(F32)
    mid = r1.astype(BF16)
    lo = (r1 - mid.astype(F32)).astype(BF16)
    return hi, mid, lo


def _plan_kernel(lg_ref, tri_ref, idx_ref, gc_ref, cnt_ref, affg_ref):
    lg = lg_ref[...]
    ex = jnp.exp(lg - jnp.max(lg, axis=0, keepdims=True))
    aff = ex / jnp.sum(ex, axis=0, keepdims=True)

    def count_ge(v):
        return jnp.sum(jnp.where(aff >= v, 1.0, 0.0), axis=1, keepdims=True)

    def search(i, thr):
        cand = thr | jnp.left_shift(jnp.int32(1), 30 - i)
        ok = (count_ge(lax.bitcast_convert_type(cand, F32)) >= CAP) & (cand >= F32_MIN_NORMAL_BITS)
        return jnp.where(ok, cand, thr)

    thr = lax.fori_loop(0, 31, search, jnp.zeros((N_EXPERTS, 1), I32))
    lo = lax.bitcast_convert_type(thr, F32)
    hi = lax.bitcast_convert_type(jnp.maximum(thr + 1, F32_MIN_NORMAL_BITS), F32)

    def refine(i, bounds):
        lo, hi = bounds
        mid = lo + (hi - lo) * 0.5
        ok = count_ge(mid) >= CAP
        return jnp.where(ok, mid, lo), jnp.where(ok, hi, mid)

    lo, hi = lax.fori_loop(0, 32, refine, (lo, hi))
    tri = tri_ref[...]
    gt = aff >= hi
    eq = jnp.where((aff >= lo) & (aff < hi), 1.0, 0.0)
    need = CAP - jnp.sum(jnp.where(gt, 1.0, 0.0), axis=1, keepdims=True)
    eq_local, eq_offs = _group_prefix(eq, tri)
    eq_rank = jnp.concatenate([eq_local[g] + eq_offs[g] for g in range(N_GROUPS)], axis=1)
    sel = jnp.where(gt | ((eq > 0.0) & (eq_rank < need)), 1.0, 0.0)
    sel_local, offs = _group_prefix(sel, tri)
    for g in range(N_GROUPS):
        sl = slice(g * GROUP, (g + 1) * GROUP)
        cnt_ref[g] = sel_local[g] + sel[:, sl]
        affg_ref[g] = aff[:, sl]

    lane = lax.broadcasted_iota(I32, (N_EXPERTS, LANES), 1)
    never = jnp.full((N_EXPERTS, LANES), 2.0 * N_TOK, F32)
    grp_lo, grp_hi = never, never
    for g in range(N_GROUPS):
        grp_lo = jnp.where(lane == g, offs[g], grp_lo)
        grp_hi = jnp.where(lane == g, offs[g + 1], grp_hi)
    row = lax.broadcasted_iota(I32, (CAP, LANES), 0).astype(F32)
    in_group_lane = lax.broadcasted_iota(I32, (CAP, GROUP), 1).astype(F32)
    zpad = jnp.zeros((LANES - N_GROUPS, GROUP), BF16)
    for e in range(N_EXPERTS):
        lo_row, hi_row = grp_lo[e:e + 1, :], grp_hi[e:e + 1, :]
        in_grp = (lo_row <= row) & (row < hi_row)
        onehot = jnp.where(in_grp, 1.0, 0.0).astype(BF16)
        counts = _dot(onehot, jnp.concatenate([cnt_ref[:, e, :].astype(BF16), zpad], axis=0))
        rank = row[:, 0:1] - jnp.sum(jnp.where(in_grp, lo_row, 0.0), axis=1, keepdims=True)
        local = jnp.sum(jnp.where(counts <= rank, 1.0, 0.0), axis=1, keepdims=True)
        grp = jnp.sum(jnp.where(hi_row <= row, 1.0, 0.0), axis=1, keepdims=True)
        tok = grp * GROUP + local
        aff_rows = sum(_dot(onehot, jnp.concatenate([part, zpad], axis=0))
                       for part in _split3_bf16(affg_ref[:, e, :]))
        gate = jnp.sum(jnp.where(in_group_lane == local, aff_rows, 0.0), axis=1, keepdims=True)
        gc_ref[0, e] = jnp.broadcast_to(gate, (CAP, LANES))
        tok_b = jnp.broadcast_to(tok, (CAP, LANES))
        tok_row = jnp.concatenate([tok_b[t * LANES:(t + 1) * LANES, :].T[0:1, :]
                                   for t in range(CAP // LANES)], axis=1)
        idx_ref[0, e:e + 1, :] = tok_row.astype(I32)


def _plan(logits_t, tri):
    ns = N_STREAMS
    return pl.pallas_call(
        _plan_kernel,
        grid=(ns,),
        in_specs=[pl.BlockSpec((N_EXPERTS, N_TOK), lambda s: (0, s)),
                  pl.BlockSpec((MXU_DIM, MXU_DIM), lambda s: (0, 0))],
        out_specs=[pl.BlockSpec((1, N_EXPERTS, CAP), lambda s: (s, 0, 0)),
                   pl.BlockSpec((1, N_EXPERTS, CAP, LANES), lambda s: (s, 0, 0, 0))],
        out_shape=[jax.ShapeDtypeStruct((ns, N_EXPERTS, CAP), I32),
                   jax.ShapeDtypeStruct((ns, N_EXPERTS, CAP, LANES), F32)],
        scratch_shapes=[pltpu.VMEM((N_GROUPS, N_EXPERTS, GROUP), F32),
                        pltpu.VMEM((N_GROUPS, N_EXPERTS, GROUP), F32)],
        compiler_params=_params("arbitrary"),
        name="plan",
    )(logits_t, tri)


def _tile_rows(r):
    return pl.ds(pl.multiple_of(r * ROW_TILE, ROW_TILE), ROW_TILE)


def _gather_kernel(idx_ref, h_ref, xe_ref, stage_ref):
    s, e = pl.program_id(0), pl.program_id(1)
    base = (s * N_EXPERTS + e) * CAP

    def move(g, carry):
        for u in range(MOVE_UNROLL):
            r = g * MOVE_UNROLL + u
            stage_ref[_tile_rows(r), :] = h_ref[0, _tile_rows(idx_ref[base + r]), :]
        return carry

    lax.fori_loop(0, CAP // MOVE_UNROLL, move, 0)
    for k in range(ROW_TILE):
        xe_ref[0, 0, :, k * LANES:(k + 1) * LANES] = (
            stage_ref[pl.ds(k, CAP, stride=ROW_TILE), :].astype(BF16))


def _gather(idx_flat, h2_tiles):
    ns = h2_tiles.shape[0]
    return pl.pallas_call(
        _gather_kernel,
        grid_spec=pltpu.PrefetchScalarGridSpec(
            num_scalar_prefetch=1,
            grid=(ns, N_EXPERTS),
            in_specs=[pl.BlockSpec((1, N_TOK * ROW_TILE, LANES), lambda s, e, idx: (s, 0, 0),
                                   pipeline_mode=pl.Buffered(1))],
            out_specs=pl.BlockSpec((1, 1, CAP, D_MODEL), lambda s, e, idx: (s, e, 0, 0)),
            scratch_shapes=[pltpu.VMEM((CAP * ROW_TILE, LANES), F32)],
        ),
        out_shape=jax.ShapeDtypeStruct((ns, N_EXPERTS, CAP, D_MODEL), BF16),
        compiler_params=_params("arbitrary", "arbitrary"),
        name="moe_gather",
    )(idx_flat, h2_tiles)


def _ffn_kernel(x_ref, wg_ref, wu_ref, wd_ref, gc_ref, ye_ref, acc_ref, *, row_tile):
    f = pl.program_id(1)
    last = pl.num_programs(1) - 1
    ns = x_ref.shape[0]
    wg = wg_ref[0, 0].astype(BF16)
    wu = wu_ref[0, 0].astype(BF16)
    wd = wd_ref[0, 0].astype(BF16)

    def partial_out(s, r0):
        x = x_ref[s, 0, r0:r0 + row_tile, :]
        hid = _silu(_dot(x, wg)) * _dot(x, wu)
        return _dot(hid.astype(BF16), wd)

    tiles = [(s, r0) for s in range(ns) for r0 in range(0, CAP, row_tile)]

    @pl.when(f == 0)
    def _():
        for s, r0 in tiles:
            acc_ref[s, r0:r0 + row_tile, :] = partial_out(s, r0)

    @pl.when((f > 0) & (f < last))
    def _():
        for s, r0 in tiles:
            acc_ref[s, r0:r0 + row_tile, :] += partial_out(s, r0)

    @pl.when(f == last)
    def _():
        for s, r0 in tiles:
            ye = (acc_ref[s, r0:r0 + row_tile, :] + partial_out(s, r0)) * gc_ref[s, 0, r0:r0 + row_tile, 0:1]
            for k in range(ROW_TILE):
                ye_ref[s, 0, pl.ds(r0 * ROW_TILE + k, row_tile, stride=ROW_TILE), :] = (
                    ye[:, k * LANES:(k + 1) * LANES])


def _ffn(xe, gc, w_gate, w_up, w_down, layer):
    ns = xe.shape[0]
    return pl.pallas_call(
        functools.partial(_ffn_kernel, row_tile=512),
        grid=(N_EXPERTS, EXPERT_FF // FF_TILE),
        in_specs=[
            pl.BlockSpec((ns, 1, CAP, D_MODEL), lambda e, f: (0, e, 0, 0)),
            pl.BlockSpec((1, 1, D_MODEL, FF_TILE), lambda e, f: (layer, e, 0, f)),
            pl.BlockSpec((1, 1, D_MODEL, FF_TILE), lambda e, f: (layer, e, 0, f)),
            pl.BlockSpec((1, 1, FF_TILE, D_MODEL), lambda e, f: (layer, e, f, 0)),
            pl.BlockSpec((ns, 1, CAP, LANES), lambda e, f: (0, e, 0, 0)),
        ],
        out_specs=pl.BlockSpec((ns, 1, CAP * ROW_TILE, LANES), lambda e, f: (0, e, 0, 0)),
        out_shape=jax.ShapeDtypeStruct((ns, N_EXPERTS, CAP * ROW_TILE, LANES), F32),
        scratch_shapes=[pltpu.VMEM((ns, CAP, D_MODEL), F32)],
        compiler_params=_params("arbitrary", "arbitrary"),
        name="moe_ffn",
    )(xe, w_gate, w_up, w_down, gc)


def _scatter_kernel(idx_ref, ye_ref, y_ref, gate_ref, *refs, split):
    out_refs, acc_ref = refs[:-1], refs[-1]
    s, j = pl.program_id(0), pl.program_id(1)

    @pl.when(j == 0)
    def _():
        acc_ref[...] = jnp.zeros_like(acc_ref)

    @pl.when(j < N_EXPERTS)
    def _():
        base = (s * N_EXPERTS + j) * CAP

        def add_rows(g, carry):
            r0 = g * MOVE_UNROLL
            first = base + r0
            group = ye_ref.at[0, 0, pl.ds(pl.multiple_of(r0 * ROW_TILE, MOVE_UNROLL * ROW_TILE),
                                          MOVE_UNROLL * ROW_TILE), :]
            dst = [_tile_rows(idx_ref[first + u]) for u in range(MOVE_UNROLL)]
            val = [acc_ref[dst[u], :] + group[u * ROW_TILE:(u + 1) * ROW_TILE, :] for u in range(MOVE_UNROLL)]
            for u in range(MOVE_UNROLL):
                acc_ref[dst[u], :] = val[u]
            return carry

        lax.fori_loop(0, CAP // MOVE_UNROLL, add_rows, 0)

    @pl.when(j >= N_EXPERTS)
    def _():
        c = j - N_EXPERTS
        rows = FIN_TOK * ROW_TILE
        part = acc_ref.at[pl.ds(pl.multiple_of(c * rows, rows), rows), :]
        moe = jnp.concatenate([part[pl.ds(k, FIN_TOK, stride=ROW_TILE), :] for k in range(ROW_TILE)], axis=1)
        res = y_ref[...] + gate_ref[0] * moe
        if split:
            for si, out_ref in enumerate(out_refs):
                @pl.when(s == si)
                def _(out_ref=out_ref):
                    out_ref[...] = res
        else:
            out_refs[0][...] = res


def _scatter(idx_flat, ye_tiles, y, mod3, layer, lat_seq, split):
    ns = ye_tiles.shape[0]
    fin = lambda j: jnp.maximum(j - N_EXPERTS, 0)

    def gate_index(s, j, idx):
        row = _mod_row(s * N_FIN + fin(j), FIN_TOK, lat_seq)
        return ((layer * N_MOD_ROWS + row) * 6 + 5, 0, 0)

    chunk = (FIN_TOK, D_MODEL)
    if split:
        out_specs = [
            pl.BlockSpec(chunk, lambda s, j, idx, si=si: (
                jnp.where(s == si, fin(j), jnp.where(s < si, 0, N_FIN - 1)), 0))
            for si in range(ns)]
        out_shape = [jax.ShapeDtypeStruct((N_TOK, D_MODEL), F32)] * ns
    else:
        out_specs = pl.BlockSpec(chunk, lambda s, j, idx: (s * N_FIN + fin(j), 0))
        out_shape = jax.ShapeDtypeStruct(y.shape, F32)
    return pl.pallas_call(
        functools.partial(_scatter_kernel, split=split),
        grid_spec=pltpu.PrefetchScalarGridSpec(
            num_scalar_prefetch=1,
            grid=(ns, N_EXPERTS + N_FIN),
            in_specs=[
                pl.BlockSpec((1, 1, CAP * ROW_TILE, LANES),
                             lambda s, j, idx: (s, jnp.minimum(j, N_EXPERTS - 1), 0, 0)),
                pl.BlockSpec(chunk, lambda s, j, idx: (s * N_FIN + fin(j), 0)),
                pl.BlockSpec((1, 1, D_MODEL), gate_index),
            ],
            out_specs=out_specs,
            scratch_shapes=[pltpu.VMEM((N_TOK * ROW_TILE, LANES), F32)],
        ),
        out_shape=out_shape,
        compiler_params=_params("arbitrary", "arbitrary"),
        name="moe_scatter",
    )(idx_flat, ye_tiles, y, mod3)


def _rope_tables(n_tokens):
    t = np.arange(n_tokens)
    row = (t // GRID_W).astype(np.float32)
    col = (t % GRID_W).astype(np.float32)
    pairs = HEAD_DIM // 4
    inv_freq = ROPE_THETA ** (-jnp.arange(pairs, dtype=F32) / pairs)
    ang = jnp.concatenate([row[:, None] * inv_freq, col[:, None] * inv_freq], axis=-1)
    cos, sin = jnp.cos(ang), jnp.sin(ang)
    reps = MXU_DIM // HEAD_DIM
    return (jnp.tile(jnp.concatenate([cos, cos], axis=-1), (1, reps)),
            jnp.tile(jnp.concatenate([-sin, sin], axis=-1), (1, reps)))


def _dup_cache(cache):
    b, s, hk, hd = cache.shape
    return jnp.broadcast_to(cache[:, :, :, None, :], (b, s, hk, 2, hd)).reshape(b, s, 2 * hk * hd).astype(BF16)


def kernel(x_prompt, x_sample, cache_attn_k, cache_attn_v, cache_na_k, cache_na_v, c, c_ctx,
           norm1_g, norm2_g, w_ada, b_ada, attn_w_qkv, attn_q_gain, attn_k_gain, attn_w_o,
           na_w_qkv, na_q_gain, na_k_gain, na_rpb, na_w_o,
           moe_w_router, moe_w_gate, moe_w_up, moe_w_down):
    bc, tc, _ = x_prompt.shape
    bl, tl, _ = x_sample.shape
    depth = w_ada.shape[0]
    assert bc * tc == N_TOK and bl * tl == N_TOK and 1 + bl <= N_MOD_ROWS
    assert tl % TM == 0 and tl % FIN_TOK == 0
    n_all = N_STREAMS * N_TOK

    cond = jnp.zeros((N_MOD_ROWS, D_MODEL), F32).at[0].set(c_ctx).at[1:1 + bl].set(c)
    mod3 = _ada(cond, w_ada, b_ada).reshape(depth * N_MOD_ROWS * 6, 1, D_MODEL)

    seg = jnp.asarray(np.kron(np.eye(MXU_DIM // HEAD_DIM), np.ones((HEAD_DIM, HEAD_DIM))) / HEAD_DIM, BF16)
    tri = jnp.asarray(np.triu(np.ones((MXU_DIM, MXU_DIM)), k=1), BF16)
    rope_tables = _rope_tables(tl)

    x_ctx = (x_prompt.reshape(N_TOK, D_MODEL), 0)
    x_lat = (x_sample.reshape(N_TOK, D_MODEL), 0)
    new_k, new_v = [], []
    for i in range(depth):
        j = i // 2
        gqa = i % 2 == 0
        if gqa:
            w_qkv, q_gain, k_gain, w_o = attn_w_qkv[j], attn_q_gain[j], attn_k_gain[j], attn_w_o[j]
            kv_heads = N_KV_HEADS
        else:
            w_qkv, q_gain, k_gain, w_o = na_w_qkv[j], na_q_gain[j], na_k_gain[j], na_w_o[j]
            kv_heads = N_HEADS
        wr_hi, wr_lo = _split_bf16(moe_w_router[i].T)

        w_qkv_b = w_qkv.astype(BF16)
        qkv_args = (norm1_g[i], mod3, i, tl, w_qkv_b, q_gain, k_gain, seg, kv_heads * HEAD_DIM)
        q_c, kb_c, vb_c, kf, vf = _qkv(x_ctx, True, *qkv_args, None)
        q_l, kb_l, vb_l = _qkv(x_lat, False, *qkv_args, rope_tables if gqa else None)
        o_ctx = _attention(*(a.reshape(bc, tc, a.shape[1]) for a in (q_c, kb_c, vb_c)), bc, 0)
        q_l, kb_l, vb_l = (a.reshape(bl, tl, a.shape[1]) for a in (q_l, kb_l, vb_l))
        if gqa:
            o_lat = _attention(q_l, kb_l, vb_l, bl, 0,
                               _dup_cache(cache_attn_k[:, j]), _dup_cache(cache_attn_v[:, j]), tq=512)
        else:
            past = cache_na_k.shape[2]
            o_lat = _na_attention(q_l, kb_l, vb_l, bl, 0,
                                  cache_na_k[:, j].reshape(bl, past, D_MODEL).astype(BF16),
                                  cache_na_v[:, j].reshape(bl, past, D_MODEL).astype(BF16),
                                  _na_bias_table(na_rpb[j]))
        new_k.append(kf.reshape(bc, 1, tc, kv_heads, HEAD_DIM))
        new_v.append(vf.reshape(bc, 1, tc, kv_heads, HEAD_DIM))

        y, h2, logits_t = _post_attention(o_ctx.reshape(N_TOK, D_MODEL), o_lat.reshape(N_TOK, D_MODEL),
                                          x_ctx, x_lat, w_o.astype(BF16), norm2_g[i], mod3, i, tl,
                                          wr_hi, wr_lo)
        idx, gc = _plan(logits_t, tri)
        idx_flat = idx.reshape(-1)
        xe = _gather(idx_flat, h2.reshape(N_STREAMS, N_TOK * ROW_TILE, LANES))
        ye = _ffn(xe, gc, moe_w_gate, moe_w_up, moe_w_down, i)
        y = _scatter(idx_flat, ye, y, mod3, i, tl, split=(i == depth - 1))
        x_ctx, x_lat = (y, 0), (y, CTX_TILES)

    y_ctx, y_lat = y
    return (y_ctx.reshape(bc, tc, D_MODEL), y_lat.reshape(bl, tl, D_MODEL),
            jnp.concatenate(new_k[0::2], axis=1), jnp.concatenate(new_v[0::2], axis=1),
            jnp.concatenate(new_k[1::2], axis=1), jnp.concatenate(new_v[1::2], axis=1))
```

```python
import functools

import jax
import jax.numpy as jnp
import numpy as np
from jax import lax
from jax.experimental import pallas as pl
from jax.experimental.pallas import tpu as pltpu

F32 = jnp.float32
BF16 = jnp.bfloat16
I32 = jnp.int32

D_MODEL = 1024
N_HEADS = 16
N_KV_HEADS = 4
HEAD_DIM = 64
GRID_W = 64
WIN_R = 8
WIN_C = 16
N_EXPERTS = 16
EXPERT_FF = 2048
ROPE_THETA = 10000.0
RMS_EPS = 1e-6
NEG_INF = -1e30
F32_MIN_NORMAL_BITS = 0x00800000
LOG2_E = 1.4426950408889634
Q_SCALE = HEAD_DIM ** -0.5 * LOG2_E

LANES = 128
MXU_DIM = 256
VMEM_LIMIT = 56 * 1024 * 1024

N_STREAMS = 2
N_TOK = 8192
CAP = 2 * N_TOK // N_EXPERTS
TM = 512
N_MOD_ROWS = 16
ROW_TILE = D_MODEL // LANES
GROUP = MXU_DIM
N_GROUPS = N_TOK // GROUP
FIN_TOK = 512
N_FIN = N_TOK // FIN_TOK
MOVE_UNROLL = 16
FF_TILE = 512
NA_QBLK = 256
NA_QROWS = NA_QBLK // GRID_W
NA_BAND = WIN_R + NA_QROWS
NA_MAX_SHIFT = WIN_R - 2 + NA_QROWS
NA_TILES = NA_MAX_SHIFT + 1
N_REL_R = 2 * WIN_R - 1
N_REL_C = 2 * WIN_C - 1


def _params(*sem):
    return pltpu.CompilerParams(dimension_semantics=sem, vmem_limit_bytes=VMEM_LIMIT)


def _dot(a, b):
    return jnp.dot(a, b, preferred_element_type=F32)


def _dot_nt(a, b):
    return lax.dot_general(a, b, (((1,), (1,)), ((), ())), preferred_element_type=F32)


def _split_bf16(x):
    hi = x.astype(BF16)
    lo = (x - hi.astype(F32)).astype(BF16)
    return hi, lo


def _silu(x):
    return x * (1.0 / (1.0 + jnp.exp(-x)))


def _ada_kernel(cond_ref, w_ref, b_ref, out_ref):
    sx = _silu(cond_ref[...])
    xh, xl = _split_bf16(sx)
    wh, wl = _split_bf16(w_ref[0])
    out_ref[0] = _dot(xh, wh) + _dot(xl, wh) + _dot(xh, wl) + b_ref[0]


def _ada(cond, w_ada, b_ada):
    depth = w_ada.shape[0]
    tn = 1024
    n_out = w_ada.shape[2]
    return pl.pallas_call(
        _ada_kernel,
        grid=(depth, n_out // tn),
        in_specs=[
            pl.BlockSpec((N_MOD_ROWS, D_MODEL), lambda l, n: (0, 0)),
            pl.BlockSpec((1, D_MODEL, tn), lambda l, n: (l, 0, n)),
            pl.BlockSpec((1, 1, tn), lambda l, n: (l, 0, n)),
        ],
        out_specs=pl.BlockSpec((1, N_MOD_ROWS, tn), lambda l, n: (l, 0, n)),
        out_shape=jax.ShapeDtypeStruct((depth, N_MOD_ROWS, n_out), F32),
        compiler_params=_params("arbitrary", "arbitrary"),
        name="ada",
    )(cond, w_ada, b_ada.reshape(depth, 1, n_out))


def _mod_row(tile, tile_rows, lat_seq):
    ctx_tiles = N_TOK // tile_rows
    return jnp.where(tile < ctx_tiles, 0, 1 + (tile - ctx_tiles) // (lat_seq // tile_rows))


def _mod_spec(layer, which, lat_seq, tile_of=lambda i: i):
    def index(i):
        return ((layer * N_MOD_ROWS + _mod_row(tile_of(i), TM, lat_seq)) * 6 + which, 0, 0)
    return pl.BlockSpec((1, 1, D_MODEL), index)


CTX_TILES = N_TOK // TM


def _x_specs(x_ctx_block0, x_lat_block0, tile_of=lambda i: i):
    return [
        pl.BlockSpec((TM, D_MODEL), lambda i: (x_ctx_block0 + jnp.minimum(tile_of(i), CTX_TILES - 1), 0)),
        pl.BlockSpec((TM, D_MODEL), lambda i: (x_lat_block0 + jnp.maximum(tile_of(i) - CTX_TILES, 0), 0)),
    ]


def _head_norm(z, seg, gain):
    ms = _dot((z * z).astype(BF16), seg)
    return z * lax.rsqrt(ms + RMS_EPS) * gain


def _rope(z, cos_t, sin_t):
    lane = lax.broadcasted_iota(I32, z.shape, 1)
    first = (lane & 32) == 0
    n = z.shape[1]
    partner = jnp.where(first, pltpu.roll(z, n - 32, axis=1), pltpu.roll(z, 32, axis=1))
    return z * cos_t + partner * sin_t


def _dup_heads(z):
    outs = []
    for b in range(z.shape[1] // LANES):
        x = z[:, b * LANES:(b + 1) * LANES]
        xr = pltpu.roll(x, HEAD_DIM, axis=1)
        lo = lax.broadcasted_iota(I32, x.shape, 1) < HEAD_DIM
        outs.append(jnp.where(lo, x, xr))
        outs.append(jnp.where(lo, xr, x))
    return jnp.concatenate(outs, axis=1)


def _store_head_rows(dst_ref, c0, z, n_heads):
    for b in range(z.shape[1] // LANES):
        pair = z[:, b * LANES:(b + 1) * LANES]
        swapped = pltpu.roll(pair, HEAD_DIM, axis=1)
        for hh, val in enumerate((pair, swapped)):
            head = c0 // HEAD_DIM + 2 * b + hh
            dst_ref[pl.ds(head, TM, stride=n_heads), :] = val[:, :HEAD_DIM]


def _qkv_kernel(*refs, kv_width, rope, emit_heads):
    it = iter(refs)
    x_ref, g_ref, shift_ref, scale_ref, w_ref, qg_ref, kg_ref, seg_ref = (next(it) for _ in range(8))
    cos_ref = sin_ref = None
    if rope:
        cos_ref, sin_ref = next(it), next(it)
    q_ref, kb_ref, vb_ref = (next(it) for _ in range(3))
    kh_ref = vh_ref = None
    if emit_heads:
        kh_ref, vh_ref = next(it), next(it)
    dup = kv_width == N_KV_HEADS * HEAD_DIM
    n_heads = kv_width // HEAD_DIM

    x = x_ref[...]
    ms = jnp.mean(x * x, axis=-1, keepdims=True)
    h = x * lax.rsqrt(ms + RMS_EPS) * (g_ref[...] * (1.0 + scale_ref[0])) + shift_ref[0]
    hb = h.astype(BF16)
    seg = seg_ref[...]
    if rope:
        cos_t, sin_t = cos_ref[...], sin_ref[...]

    q_width = N_HEADS * HEAD_DIM

    def finish_q(c0, raw):
        z = _head_norm(raw, seg, qg_ref[...])
        if rope:
            z = _rope(z, cos_t, sin_t)
        q_ref[:, c0:c0 + MXU_DIM] = z.astype(BF16)

    def finish_k(c0, raw):
        z = _head_norm(raw, seg, kg_ref[...])
        if emit_heads:
            _store_head_rows(kh_ref, c0, z, n_heads)
        if rope:
            z = _rope(z, cos_t, sin_t)
        if dup:
            kb_ref[:, 2 * c0:2 * c0 + 2 * MXU_DIM] = _dup_heads(z).astype(BF16)
        else:
            kb_ref[:, c0:c0 + MXU_DIM] = z.astype(BF16)

    def finish_v(c0, raw):
        if emit_heads:
            _store_head_rows(vh_ref, c0, raw, n_heads)
        if dup:
            vb_ref[:, 2 * c0:2 * c0 + 2 * MXU_DIM] = _dup_heads(raw).astype(BF16)
        else:
            vb_ref[:, c0:c0 + MXU_DIM] = raw.astype(BF16)

    q_chunks = [(finish_q, c0, c0) for c0 in range(0, q_width, MXU_DIM)]
    k_chunks = [(finish_k, c0, q_width + c0) for c0 in range(0, kv_width, MXU_DIM)]
    v_chunks = [(finish_v, c0, q_width + kv_width + c0) for c0 in range(0, kv_width, MXU_DIM)]
    normed = q_chunks + k_chunks
    chunks = []
    for i, chunk in enumerate(normed):
        chunks.append(chunk)
        if i * len(v_chunks) // len(normed) != (i + 1) * len(v_chunks) // len(normed):
            chunks.append(v_chunks[i * len(v_chunks) // len(normed)])

    project = lambda col: _dot(hb, w_ref[:, col:col + MXU_DIM])
    raw = project(chunks[0][2])
    for i, (finish, c0, _) in enumerate(chunks):
        nxt = project(chunks[i + 1][2]) if i + 1 < len(chunks) else None
        finish(c0, raw)
        raw = nxt

def _qkv(x_src, is_ctx, norm_g, mod3, layer, lat_seq, w_bf16, q_gain, k_gain, seg, kv_width, rope_tables):
    n_w = w_bf16.shape[1]
    kvb_width = 2 * kv_width if kv_width == N_KV_HEADS * HEAD_DIM else kv_width
    n_heads = kv_width // HEAD_DIM
    tile_of = (lambda i: i) if is_ctx else (lambda i: i + CTX_TILES)
    tile = lambda w: pl.BlockSpec((TM, w), lambda i: (i, 0))
    const = lambda shape: pl.BlockSpec(shape, lambda i: (0,) * len(shape))
    in_specs = [
        pl.BlockSpec((TM, D_MODEL), lambda i: (x_src[1] + i, 0)),
        const((1, D_MODEL)),
        _mod_spec(layer, 0, lat_seq, tile_of),
        _mod_spec(layer, 1, lat_seq, tile_of),
        const((D_MODEL, n_w)),
        const((1, MXU_DIM)),
        const((1, MXU_DIM)),
        const((MXU_DIM, MXU_DIM)),
    ]
    args = [x_src[0], norm_g.reshape(1, D_MODEL), mod3, mod3, w_bf16,
            jnp.tile(q_gain * Q_SCALE, MXU_DIM // HEAD_DIM).reshape(1, MXU_DIM),
            jnp.tile(k_gain, MXU_DIM // HEAD_DIM).reshape(1, MXU_DIM), seg]
    if rope_tables is not None:
        seq_tiles = lat_seq // TM
        in_specs += [pl.BlockSpec((TM, MXU_DIM), lambda i: (i % seq_tiles, 0))] * 2
        args += list(rope_tables)
    out_specs = [tile(D_MODEL), tile(kvb_width), tile(kvb_width)]
    out_shape = [jax.ShapeDtypeStruct((N_TOK, D_MODEL), BF16),
                 jax.ShapeDtypeStruct((N_TOK, kvb_width), BF16),
                 jax.ShapeDtypeStruct((N_TOK, kvb_width), BF16)]
    if is_ctx:
        out_specs += [pl.BlockSpec((TM * n_heads, HEAD_DIM), lambda i: (i, 0))] * 2
        out_shape += [jax.ShapeDtypeStruct((N_TOK * n_heads, HEAD_DIM), F32)] * 2
    return pl.pallas_call(
        functools.partial(_qkv_kernel, kv_width=kv_width, rope=rope_tables is not None, emit_heads=is_ctx),
        grid=(N_TOK // TM,),
        in_specs=in_specs,
        out_specs=out_specs,
        out_shape=out_shape,
        compiler_params=_params("arbitrary"),
        name="qkv",
    )(*args)


def _two_head_rows(qj):
    lo = lax.broadcasted_iota(I32, qj.shape, 1) < HEAD_DIM
    zero = jnp.zeros_like(qj)
    return jnp.concatenate([jnp.where(lo, qj, zero), jnp.where(lo, zero, qj)], axis=0)


def _merge_two_heads(r):
    tq = r.shape[0] // 2
    lo = lax.broadcasted_iota(I32, (tq, LANES), 1) < HEAD_DIM
    return jnp.where(lo, r[:tq], r[tq:])


def _attn_kernel(*refs, n_kv_blocks, has_ctx):
    if has_ctx:
        q_ref, k_ref, v_ref, ck_ref, cv_ref, o_ref = refs
    else:
        q_ref, k_ref, v_ref, o_ref = refs
    n_q_blocks = D_MODEL // LANES
    key_lanes = lambda j: slice(((j * n_kv_blocks) // n_q_blocks) * LANES,
                                ((j * n_kv_blocks) // n_q_blocks + 1) * LANES)

    def scores(j):
        q2 = _two_head_rows(q_ref[0, :, j * LANES:(j + 1) * LANES])
        s = _dot_nt(q2, k_ref[0, :, key_lanes(j)])
        sc = _dot_nt(q2, ck_ref[0, :, key_lanes(j)]) if has_ctx else None
        return s, sc

    nxt = scores(0)
    for j in range(n_q_blocks):
        (s, sc), ksl = nxt, key_lanes(j)
        if j + 1 < n_q_blocks:
            nxt = scores(j + 1)
        m = jnp.max(s, axis=-1, keepdims=True)
        if has_ctx:
            m = jnp.maximum(m, jnp.max(sc, axis=-1, keepdims=True))
        p = jnp.exp2(s - m)
        l = jnp.sum(p, axis=-1, keepdims=True)
        r = _dot(p.astype(BF16), v_ref[0, :, ksl])
        if has_ctx:
            pc = jnp.exp2(sc - m)
            l = l + jnp.sum(pc, axis=-1, keepdims=True)
            r = r + _dot(pc.astype(BF16), cv_ref[0, :, ksl])
        r = r / l
        o_ref[0, :, j * LANES:(j + 1) * LANES] = _merge_two_heads(r).astype(BF16)


def _attention(q, k, v, n_batch, batch0, ctx_k=None, ctx_v=None, tq=256):
    t = q.shape[1]
    s, w = k.shape[1], k.shape[2]
    has_ctx = ctx_k is not None
    in_specs = [
        pl.BlockSpec((1, tq, D_MODEL), lambda bi, qi: (bi + batch0, qi, 0)),
        pl.BlockSpec((1, s, w), lambda bi, qi: (bi + batch0, 0, 0)),
        pl.BlockSpec((1, s, w), lambda bi, qi: (bi + batch0, 0, 0)),
    ]
    args = [q, k, v]
    if has_ctx:
        sc = ctx_k.shape[1]
        in_specs += [pl.BlockSpec((1, sc, w), lambda bi, qi: (bi, 0, 0))] * 2
        args += [ctx_k, ctx_v]
    return pl.pallas_call(
        functools.partial(_attn_kernel, n_kv_blocks=w // LANES, has_ctx=has_ctx),
        grid=(n_batch, t // tq),
        in_specs=in_specs,
        out_specs=pl.BlockSpec((1, tq, D_MODEL), lambda bi, qi: (bi, qi, 0)),
        out_shape=jax.ShapeDtypeStruct((n_batch, t, D_MODEL), BF16),
        compiler_params=_params("arbitrary", "arbitrary"),
        name="attn",
    )(*args)


def _na_kernel(q_ref, k_ref, v_ref, ck_ref, cv_ref, bias_ref, o_ref, *, rows):
    blk = pl.program_id(1)
    q_row0 = NA_QROWS * blk
    band_start = jnp.minimum(jnp.clip(q_row0 - WIN_R // 2, 0, rows - WIN_R), rows - NA_BAND)
    n_band = NA_BAND * GRID_W
    band = pl.ds(pl.multiple_of(band_start * GRID_W, LANES), n_band)
    shape = (NA_QBLK, n_band)
    q_r = q_row0 + lax.broadcasted_iota(I32, shape, 0) // GRID_W
    k_r = band_start + lax.broadcasted_iota(I32, shape, 1) // GRID_W
    q_rs = jnp.clip(q_r - WIN_R // 2, 0, rows - WIN_R)
    row_ok1 = (k_r >= q_rs) & (k_r < q_rs + WIN_R)
    row_ok = jnp.concatenate([row_ok1, row_ok1], axis=0)
    shift0 = (band_start - q_row0 + NA_MAX_SHIFT) // 2
    n_q_blocks = D_MODEL // LANES

    def scores(j):
        sl = slice(j * LANES, (j + 1) * LANES)
        q2 = _two_head_rows(q_ref[0, :, sl])
        return _dot_nt(q2, k_ref[0, band, sl]), _dot_nt(q2, ck_ref[0, :, sl])

    nxt = scores(0)
    for j in range(n_q_blocks):
        sl = slice(j * LANES, (j + 1) * LANES)
        s, sc = nxt
        if j + 1 < n_q_blocks:
            nxt = scores(j + 1)
        bias = jnp.concatenate(
            [jnp.concatenate([bias_ref[2 * j + hh, shift0 + m - u] for m in range(NA_BAND // 2)], axis=1)
             for hh in range(2) for u in range(NA_QROWS // 2)], axis=0)
        s = jnp.where(row_ok, s + bias, NEG_INF)
        m = jnp.maximum(jnp.max(s, axis=-1, keepdims=True), jnp.max(sc, axis=-1, keepdims=True))
        p = jnp.exp2(s - m)
        pc = jnp.exp2(sc - m)
        l = jnp.sum(p, axis=-1, keepdims=True) + jnp.sum(pc, axis=-1, keepdims=True)
        r = _dot(p.astype(BF16), v_ref[0, band, sl]) + _dot(pc.astype(BF16), cv_ref[0, :, sl])
        r = r / l
        o_ref[0, :, sl] = _merge_two_heads(r).astype(BF16)


def _na_bias_kernel(rpb_ref, out_ref, tz_ref):
    h = pl.program_id(0)
    shape = (GRID_W, LANES)
    q_col = lax.broadcasted_iota(I32, shape, 0)
    lane = lax.broadcasted_iota(I32, shape, 1)
    k_col = lane & (GRID_W - 1)
    col_start = jnp.clip(q_col - WIN_C // 2, 0, GRID_W - WIN_C)
    col_ok = (k_col >= col_start) & (k_col < col_start + WIN_C)
    rel_c = jnp.clip(k_col - q_col + WIN_C - 1, 0, N_REL_C - 1)
    for a in range(N_REL_R):
        acc = jnp.zeros(shape, F32)
        for b in range(N_REL_C):
            acc = jnp.where(rel_c == b, rpb_ref[(h * N_REL_R + a) * N_REL_C + b], acc)
        tz_ref[a] = jnp.where(col_ok, acc * LOG2_E, NEG_INF)
    left = lane < GRID_W
    for dd in range(NA_TILES):
        for qr in range(2):
            rel = [min(max(2 * dd - NA_MAX_SHIFT + kr - qr + WIN_R - 1, 0), N_REL_R - 1) for kr in range(2)]
            out_ref[0, dd, qr * GRID_W:(qr + 1) * GRID_W, :] = jnp.where(left, tz_ref[rel[0]], tz_ref[rel[1]])


def _na_bias_table(rpb):
    return pl.pallas_call(
        _na_bias_kernel,
        grid=(N_HEADS,),
        in_specs=[pl.BlockSpec(memory_space=pltpu.SMEM)],
        out_specs=pl.BlockSpec((1, NA_TILES, LANES, LANES), lambda h: (h, 0, 0, 0)),
        out_shape=jax.ShapeDtypeStruct((N_HEADS, NA_TILES, LANES, LANES), F32),
        scratch_shapes=[pltpu.VMEM((N_REL_R, GRID_W, LANES), F32)],
        compiler_params=_params("arbitrary"),
        name="na_bias",
    )(rpb.reshape(-1))


def _na_attention(q, k, v, n_batch, batch0, ctx_k, ctx_v, bias_tbl):
    t = q.shape[1]
    sc = ctx_k.shape[1]
    rows = t // GRID_W
    full = lambda n, off: pl.BlockSpec((1, n, D_MODEL), lambda bi, qi: (bi + off, 0, 0))
    return pl.pallas_call(
        functools.partial(_na_kernel, rows=rows),
        grid=(n_batch, t // NA_QBLK),
        in_specs=[
            pl.BlockSpec((1, NA_QBLK, D_MODEL), lambda bi, qi: (bi + batch0, qi, 0)),
            full(t, batch0), full(t, batch0), full(sc, 0), full(sc, 0),
            pl.BlockSpec(bias_tbl.shape, lambda bi, qi: (0, 0, 0, 0)),
        ],
        out_specs=pl.BlockSpec((1, NA_QBLK, D_MODEL), lambda bi, qi: (bi, qi, 0)),
        out_shape=jax.ShapeDtypeStruct((n_batch, t, D_MODEL), BF16),
        compiler_params=_params("arbitrary", "arbitrary"),
        name="na_attn",
    )(q, k, v, ctx_k, ctx_v, bias_tbl)


def _post_kernel(oc_ref, ol_ref, xc_ref, xl_ref, wo_ref, gate_ref, g_ref, shift_ref, scale_ref,
                 wrh_ref, wrl_ref, y_ref, h_ref, lg_ref):
    is_ctx = pl.program_id(0) < CTX_TILES
    o = jnp.where(is_ctx, oc_ref[...], ol_ref[...])
    x = jnp.where(is_ctx, xc_ref[...], xl_ref[...])
    y = x + gate_ref[0] * _dot(o, wo_ref[...])
    y_ref[...] = y
    ms = jnp.mean(y * y, axis=-1, keepdims=True)
    h = y * lax.rsqrt(ms + RMS_EPS) * (g_ref[...] * (1.0 + scale_ref[0])) + shift_ref[0]
    for k in range(ROW_TILE):
        h_ref[pl.ds(k, TM, stride=ROW_TILE), :] = h[:, k * LANES:(k + 1) * LANES]
    hh, hl = _split_bf16(h)
    lg_ref[...] = _dot_nt(wrh_ref[...], hh) + _dot_nt(wrh_ref[...], hl) + _dot_nt(wrl_ref[...], hh)


def _post_attention(o_ctx, o_lat, x_ctx, x_lat, wo_bf16, norm_g, mod3, layer, lat_seq, wr_hi, wr_lo):
    n_tok = N_STREAMS * N_TOK
    tile = lambda w: pl.BlockSpec((TM, w), lambda i: (i, 0))
    const = lambda shape: pl.BlockSpec(shape, lambda i: (0,) * len(shape))
    return pl.pallas_call(
        _post_kernel,
        grid=(n_tok // TM,),
        in_specs=_x_specs(0, 0) + _x_specs(x_ctx[1], x_lat[1]) + [
            const((D_MODEL, D_MODEL)),
            _mod_spec(layer, 2, lat_seq),
            const((1, D_MODEL)),
            _mod_spec(layer, 3, lat_seq),
            _mod_spec(layer, 4, lat_seq),
            const((N_EXPERTS, D_MODEL)), const((N_EXPERTS, D_MODEL)),
        ],
        out_specs=[tile(D_MODEL), pl.BlockSpec((TM * ROW_TILE, LANES), lambda i: (i, 0)),
                   pl.BlockSpec((N_EXPERTS, TM), lambda i: (0, i))],
        out_shape=[jax.ShapeDtypeStruct((n_tok, D_MODEL), F32),
                   jax.ShapeDtypeStruct((n_tok * ROW_TILE, LANES), F32),
                   jax.ShapeDtypeStruct((N_EXPERTS, n_tok), F32)],
        compiler_params=_params("arbitrary"),
        name="post_attn",
    )(o_ctx, o_lat, x_ctx[0], x_lat[0], wo_bf16, mod3, norm_g.reshape(1, D_MODEL), mod3, mod3, wr_hi, wr_lo)


def _group_prefix(mask, tri):
    local, offs = [], []
    off = jnp.zeros((mask.shape[0], 1), F32)
    for g in range(mask.shape[1] // GROUP):
        xg = mask[:, g * GROUP:(g + 1) * GROUP]
        offs.append(off)
        local.append(_dot(xg.astype(BF16), tri))
        off = off + jnp.sum(xg, axis=1, keepdims=True)
    offs.append(off)
    return local, offs


def _split3_bf16(x):
    hi = x.astype(BF16)
    r1 = x - hi.astype(F32)
    mid = r1.astype(BF16)
    lo = (r1 - mid.astype(F32)).astype(BF16)
    return hi, mid, lo


def _plan_kernel(lg_ref, tri_ref, idx_ref, gc_ref, cnt_ref, affg_ref):
    lg = lg_ref[...]
    ex = jnp.exp(lg - jnp.max(lg, axis=0, keepdims=True))
    aff = ex / jnp.sum(ex, axis=0, keepdims=True)

    def count_ge(v):
        return jnp.sum(jnp.where(aff >= v, 1.0, 0.0), axis=1, keepdims=True)

    def search(i, thr):
        cand = thr | jnp.left_shift(jnp.int32(1), 30 - i)
        ok = (count_ge(lax.bitcast_convert_type(cand, F32)) >= CAP) & (cand >= F32_MIN_NORMAL_BITS)
        return jnp.where(ok, cand, thr)

    thr = lax.fori_loop(0, 31, search, jnp.zeros((N_EXPERTS, 1), I32))
    lo = lax.bitcast_convert_type(thr, F32)
    hi = lax.bitcast_convert_type(jnp.maximum(thr + 1, F32_MIN_NORMAL_BITS), F32)

    def refine(i, bounds):
        lo, hi = bounds
        mid = lo + (hi - lo) * 0.5
        ok = count_ge(mid) >= CAP
        return jnp.where(ok, mid, lo), jnp.where(ok, hi, mid)

    lo, hi = lax.fori_loop(0, 32, refine, (lo, hi))
    tri = tri_ref[...]
    gt = aff >= hi
    eq = jnp.where((aff >= lo) & (aff < hi), 1.0, 0.0)
    need = CAP - jnp.sum(jnp.where(gt, 1.0, 0.0), axis=1, keepdims=True)
    eq_local, eq_offs = _group_prefix(eq, tri)
    eq_rank = jnp.concatenate([eq_local[g] + eq_offs[g] for g in range(N_GROUPS)], axis=1)
    sel = jnp.where(gt | ((eq > 0.0) & (eq_rank < need)), 1.0, 0.0)
    sel_local, offs = _group_prefix(sel, tri)
    for g in range(N_GROUPS):
        sl = slice(g * GROUP, (g + 1) * GROUP)
        cnt_ref[g] = sel_local[g] + sel[:, sl]
        affg_ref[g] = aff[:, sl]

    lane = lax.broadcasted_iota(I32, (N_EXPERTS, LANES), 1)
    never = jnp.full((N_EXPERTS, LANES), 2.0 * N_TOK, F32)
    grp_lo, grp_hi = never, never
    for g in range(N_GROUPS):
        grp_lo = jnp.where(lane == g, offs[g], grp_lo)
        grp_hi = jnp.where(lane == g, offs[g + 1], grp_hi)
    row = lax.broadcasted_iota(I32, (CAP, LANES), 0).astype(F32)
    in_group_lane = lax.broadcasted_iota(I32, (CAP, GROUP), 1).astype(F32)
    zpad = jnp.zeros((LANES - N_GROUPS, GROUP), BF16)
    for e in range(N_EXPERTS):
        lo_row, hi_row = grp_lo[e:e + 1, :], grp_hi[e:e + 1, :]
        in_grp = (lo_row <= row) & (row < hi_row)
        onehot = jnp.where(in_grp, 1.0, 0.0).astype(BF16)
        counts = _dot(onehot, jnp.concatenate([cnt_ref[:, e, :].astype(BF16), zpad], axis=0))
        rank = row[:, 0:1] - jnp.sum(jnp.where(in_grp, lo_row, 0.0), axis=1, keepdims=True)
        local = jnp.sum(jnp.where(counts <= rank, 1.0, 0.0), axis=1, keepdims=True)
        grp = jnp.sum(jnp.where(hi_row <= row, 1.0, 0.0), axis=1, keepdims=True)
        tok = grp * GROUP + local
        aff_rows = sum(_dot(onehot, jnp.concatenate([part, zpad], axis=0))
                       for part in _split3_bf16(affg_ref[:, e, :]))
        gate = jnp.sum(jnp.where(in_group_lane == local, aff_rows, 0.0), axis=1, keepdims=True)
        gc_ref[0, e] = jnp.broadcast_to(gate, (CAP, LANES))
        tok_b = jnp.broadcast_to(tok, (CAP, LANES))
        tok_row = jnp.concatenate([tok_b[t * LANES:(t + 1) * LANES, :].T[0:1, :]
                                   for t in range(CAP // LANES)], axis=1)
        idx_ref[0, e:e + 1, :] = tok_row.astype(I32)


def _plan(logits_t, tri):
    ns = N_STREAMS
    return pl.pallas_call(
        _plan_kernel,
        grid=(ns,),
        in_specs=[pl.BlockSpec((N_EXPERTS, N_TOK), lambda s: (0, s)),
                  pl.BlockSpec((MXU_DIM, MXU_DIM), lambda s: (0, 0))],
        out_specs=[pl.BlockSpec((1, N_EXPERTS, CAP), lambda s: (s, 0, 0)),
                   pl.BlockSpec((1, N_EXPERTS, CAP, LANES), lambda s: (s, 0, 0, 0))],
        out_shape=[jax.ShapeDtypeStruct((ns, N_EXPERTS, CAP), I32),
                   jax.ShapeDtypeStruct((ns, N_EXPERTS, CAP, LANES), F32)],
        scratch_shapes=[pltpu.VMEM((N_GROUPS, N_EXPERTS, GROUP), F32),
                        pltpu.VMEM((N_GROUPS, N_EXPERTS, GROUP), F32)],
        compiler_params=_params("arbitrary"),
        name="plan",
    )(logits_t, tri)


def _tile_rows(r):
    return pl.ds(pl.multiple_of(r * ROW_TILE, ROW_TILE), ROW_TILE)


def _gather_kernel(idx_ref, h_ref, xe_ref, stage_ref):
    def move(g, carry):
        for u in range(MOVE_UNROLL):
            r = g * MOVE_UNROLL + u
            stage_ref[_tile_rows(r), :] = h_ref[0, _tile_rows(idx_ref[0, 0, 0, r]), :]
        return carry

    lax.fori_loop(0, CAP // MOVE_UNROLL, move, 0)
    for k in range(ROW_TILE):
        xe_ref[0, 0, :, k * LANES:(k + 1) * LANES] = (
            stage_ref[pl.ds(k, CAP, stride=ROW_TILE), :].astype(BF16))


def _gather(idx, h2_tiles):
    ns = h2_tiles.shape[0]
    return pl.pallas_call(
        _gather_kernel,
        grid=(ns, N_EXPERTS),
        in_specs=[pl.BlockSpec((1, 1, 1, CAP), lambda s, e: (s, e, 0, 0), memory_space=pltpu.SMEM),
                  pl.BlockSpec((1, N_TOK * ROW_TILE, LANES), lambda s, e: (s, 0, 0),
                               pipeline_mode=pl.Buffered(1))],
        out_specs=pl.BlockSpec((1, 1, CAP, D_MODEL), lambda s, e: (s, e, 0, 0)),
        scratch_shapes=[pltpu.VMEM((CAP * ROW_TILE, LANES), F32)],
        out_shape=jax.ShapeDtypeStruct((ns, N_EXPERTS, CAP, D_MODEL), BF16),
        compiler_params=_params("arbitrary", "arbitrary"),
        name="moe_gather",
    )(idx.reshape(ns, N_EXPERTS, 1, CAP), h2_tiles)


def _ffn_kernel(x_ref, wg_ref, wu_ref, wd_ref, gc_ref, ye_ref, acc_ref, *, row_tile):
    f = pl.program_id(1)
    last = pl.num_programs(1) - 1
    ns = x_ref.shape[0]
    wg = wg_ref[0, 0].astype(BF16)
    wu = wu_ref[0, 0].astype(BF16)
    wd = wd_ref[0, 0].astype(BF16)

    tiles = [(s, r0) for s in range(ns) for r0 in range(0, CAP, row_tile)]

    def gate_up(s, r0):
        x = x_ref[s, 0, r0:r0 + row_tile, :]
        return _dot(x, wg), _dot(x, wu)

    def for_each_tile(emit):
        nxt = gate_up(*tiles[0])
        for i, (s, r0) in enumerate(tiles):
            gate, up = nxt
            if i + 1 < len(tiles):
                nxt = gate_up(*tiles[i + 1])
            emit(s, r0, _dot((_silu(gate) * up).astype(BF16), wd))

    def first(s, r0, part):
        acc_ref[s, r0:r0 + row_tile, :] = part

    def middle(s, r0, part):
        acc_ref[s, r0:r0 + row_tile, :] += part

    def final(s, r0, part):
        ye = (acc_ref[s, r0:r0 + row_tile, :] + part) * gc_ref[s, 0, r0:r0 + row_tile, 0:1]
        for k in range(ROW_TILE):
            ye_ref[s, 0, pl.ds(r0 * ROW_TILE + k, row_tile, stride=ROW_TILE), :] = (
                ye[:, k * LANES:(k + 1) * LANES])

    pl.when(f == 0)(lambda: for_each_tile(first))
    pl.when((f > 0) & (f < last))(lambda: for_each_tile(middle))
    pl.when(f == last)(lambda: for_each_tile(final))


def _ffn(xe, gc, w_gate, w_up, w_down, layer):
    ns = xe.shape[0]
    return pl.pallas_call(
        functools.partial(_ffn_kernel, row_tile=512),
        grid=(N_EXPERTS, EXPERT_FF // FF_TILE),
        in_specs=[
            pl.BlockSpec((ns, 1, CAP, D_MODEL), lambda e, f: (0, e, 0, 0)),
            pl.BlockSpec((1, 1, D_MODEL, FF_TILE), lambda e, f: (layer, e, 0, f)),
            pl.BlockSpec((1, 1, D_MODEL, FF_TILE), lambda e, f: (layer, e, 0, f)),
            pl.BlockSpec((1, 1, FF_TILE, D_MODEL), lambda e, f: (layer, e, f, 0)),
            pl.BlockSpec((ns, 1, CAP, LANES), lambda e, f: (0, e, 0, 0)),
        ],
        out_specs=pl.BlockSpec((ns, 1, CAP * ROW_TILE, LANES), lambda e, f: (0, e, 0, 0)),
        out_shape=jax.ShapeDtypeStruct((ns, N_EXPERTS, CAP * ROW_TILE, LANES), F32),
        scratch_shapes=[pltpu.VMEM((ns, CAP, D_MODEL), F32)],
        compiler_params=_params("arbitrary", "arbitrary"),
        name="moe_ffn",
    )(xe, w_gate, w_up, w_down, gc)


def _scatter_kernel(idx_ref, ye_ref, y_ref, gate_ref, *refs, split):
    out_refs, acc_ref = refs[:-1], refs[-1]
    s, j = pl.program_id(0), pl.program_id(1)

    @pl.when(j == 0)
    def _():
        acc_ref[...] = jnp.zeros_like(acc_ref)

    @pl.when(j < N_EXPERTS)
    def _():
        def add_rows(g, carry):
            r0 = g * MOVE_UNROLL
            group = ye_ref.at[0, 0, pl.ds(pl.multiple_of(r0 * ROW_TILE, MOVE_UNROLL * ROW_TILE),
                                          MOVE_UNROLL * ROW_TILE), :]
            dst = [_tile_rows(idx_ref[0, 0, 0, r0 + u]) for u in range(MOVE_UNROLL)]
            val = [acc_ref[dst[u], :] + group[u * ROW_TILE:(u + 1) * ROW_TILE, :] for u in range(MOVE_UNROLL)]
            for u in range(MOVE_UNROLL):
                acc_ref[dst[u], :] = val[u]
            return carry

        lax.fori_loop(0, CAP // MOVE_UNROLL, add_rows, 0)

    @pl.when(j >= N_EXPERTS)
    def _():
        c = j - N_EXPERTS
        rows = FIN_TOK * ROW_TILE
        part = acc_ref.at[pl.ds(pl.multiple_of(c * rows, rows), rows), :]
        moe = jnp.concatenate([part[pl.ds(k, FIN_TOK, stride=ROW_TILE), :] for k in range(ROW_TILE)], axis=1)
        res = y_ref[...] + gate_ref[0] * moe
        if split:
            for si, out_ref in enumerate(out_refs):
                @pl.when(s == si)
                def _(out_ref=out_ref):
                    out_ref[...] = res
        else:
            out_refs[0][...] = res


def _scatter(idx, ye_tiles, y, mod3, layer, lat_seq, split):
    ns = ye_tiles.shape[0]
    fin = lambda j: jnp.maximum(j - N_EXPERTS, 0)

    def gate_index(s, j):
        row = _mod_row(s * N_FIN + fin(j), FIN_TOK, lat_seq)
        return ((layer * N_MOD_ROWS + row) * 6 + 5, 0, 0)

    chunk = (FIN_TOK, D_MODEL)
    if split:
        out_specs = [
            pl.BlockSpec(chunk, lambda s, j, si=si: (
                jnp.where(s == si, fin(j), jnp.where(s < si, 0, N_FIN - 1)), 0))
            for si in range(ns)]
        out_shape = [jax.ShapeDtypeStruct((N_TOK, D_MODEL), F32)] * ns
    else:
        out_specs = pl.BlockSpec(chunk, lambda s, j: (s * N_FIN + fin(j), 0))
        out_shape = jax.ShapeDtypeStruct(y.shape, F32)
    expert = lambda j: jnp.minimum(j, N_EXPERTS - 1)
    return pl.pallas_call(
        functools.partial(_scatter_kernel, split=split),
        grid=(ns, N_EXPERTS + N_FIN),
        in_specs=[
            pl.BlockSpec((1, 1, 1, CAP), lambda s, j: (s, expert(j), 0, 0), memory_space=pltpu.SMEM),
            pl.BlockSpec((1, 1, CAP * ROW_TILE, LANES), lambda s, j: (s, expert(j), 0, 0)),
            pl.BlockSpec(chunk, lambda s, j: (s * N_FIN + fin(j), 0)),
            pl.BlockSpec((1, 1, D_MODEL), gate_index),
        ],
        out_specs=out_specs,
        scratch_shapes=[pltpu.VMEM((N_TOK * ROW_TILE, LANES), F32)],
        out_shape=out_shape,
        compiler_params=_params("arbitrary", "arbitrary"),
        name="moe_scatter",
    )(idx.reshape(ns, N_EXPERTS, 1, CAP), ye_tiles, y, mod3)


def _rope_tables(n_tokens):
    t = np.arange(n_tokens)
    row = (t // GRID_W).astype(np.float32)
    col = (t % GRID_W).astype(np.float32)
    pairs = HEAD_DIM // 4
    inv_freq = ROPE_THETA ** (-jnp.arange(pairs, dtype=F32) / pairs)
    ang = jnp.concatenate([row[:, None] * inv_freq, col[:, None] * inv_freq], axis=-1)
    cos, sin = jnp.cos(ang), jnp.sin(ang)
    reps = MXU_DIM // HEAD_DIM
    return (jnp.tile(jnp.concatenate([cos, cos], axis=-1), (1, reps)),
            jnp.tile(jnp.concatenate([-sin, sin], axis=-1), (1, reps)))


def _dup_cache(cache):
    b, s, hk, hd = cache.shape
    return jnp.broadcast_to(cache[:, :, :, None, :], (b, s, hk, 2, hd)).reshape(b, s, 2 * hk * hd).astype(BF16)


def kernel(x_prompt, x_sample, cache_attn_k, cache_attn_v, cache_na_k, cache_na_v, c, c_ctx,
           norm1_g, norm2_g, w_ada, b_ada, attn_w_qkv, attn_q_gain, attn_k_gain, attn_w_o,
           na_w_qkv, na_q_gain, na_k_gain, na_rpb, na_w_o,
           moe_w_router, moe_w_gate, moe_w_up, moe_w_down):
    bc, tc, _ = x_prompt.shape
    bl, tl, _ = x_sample.shape
    depth = w_ada.shape[0]
    assert bc * tc == N_TOK and bl * tl == N_TOK and 1 + bl <= N_MOD_ROWS
    assert tl % TM == 0 and tl % FIN_TOK == 0
    n_all = N_STREAMS * N_TOK

    cond = jnp.zeros((N_MOD_ROWS, D_MODEL), F32).at[0].set(c_ctx).at[1:1 + bl].set(c)
    mod3 = _ada(cond, w_ada, b_ada).reshape(depth * N_MOD_ROWS * 6, 1, D_MODEL)

    seg = jnp.asarray(np.kron(np.eye(MXU_DIM // HEAD_DIM), np.ones((HEAD_DIM, HEAD_DIM))) / HEAD_DIM, BF16)
    tri = jnp.asarray(np.triu(np.ones((MXU_DIM, MXU_DIM)), k=1), BF16)
    rope_tables = _rope_tables(tl)

    x_ctx = (x_prompt.reshape(N_TOK, D_MODEL), 0)
    x_lat = (x_sample.reshape(N_TOK, D_MODEL), 0)
    new_k, new_v = [], []
    for i in range(depth):
        j = i // 2
        gqa = i % 2 == 0
        if gqa:
            w_qkv, q_gain, k_gain, w_o = attn_w_qkv[j], attn_q_gain[j], attn_k_gain[j], attn_w_o[j]
            kv_heads = N_KV_HEADS
        else:
            w_qkv, q_gain, k_gain, w_o = na_w_qkv[j], na_q_gain[j], na_k_gain[j], na_w_o[j]
            kv_heads = N_HEADS
        wr_hi, wr_lo = _split_bf16(moe_w_router[i].T)

        w_qkv_b = w_qkv.astype(BF16)
        qkv_args = (norm1_g[i], mod3, i, tl, w_qkv_b, q_gain, k_gain, seg, kv_heads * HEAD_DIM)
        q_c, kb_c, vb_c, kf, vf = _qkv(x_ctx, True, *qkv_args, None)
        q_l, kb_l, vb_l = _qkv(x_lat, False, *qkv_args, rope_tables if gqa else None)
        o_ctx = _attention(*(a.reshape(bc, tc, a.shape[1]) for a in (q_c, kb_c, vb_c)), bc, 0)
        q_l, kb_l, vb_l = (a.reshape(bl, tl, a.shape[1]) for a in (q_l, kb_l, vb_l))
        if gqa:
            o_lat = _attention(q_l, kb_l, vb_l, bl, 0,
                               _dup_cache(cache_attn_k[:, j]), _dup_cache(cache_attn_v[:, j]), tq=512)
        else:
            past = cache_na_k.shape[2]
            o_lat = _na_attention(q_l, kb_l, vb_l, bl, 0,
                                  cache_na_k[:, j].reshape(bl, past, D_MODEL).astype(BF16),
                                  cache_na_v[:, j].reshape(bl, past, D_MODEL).astype(BF16),
                                  _na_bias_table(na_rpb[j]))
        new_k.append(kf.reshape(bc, 1, tc, kv_heads, HEAD_DIM))
        new_v.append(vf.reshape(bc, 1, tc, kv_heads, HEAD_DIM))

        y, h2, logits_t = _post_attention(o_ctx.reshape(N_TOK, D_MODEL), o_lat.reshape(N_TOK, D_MODEL),
                                          x_ctx, x_lat, w_o.astype(BF16), norm2_g[i], mod3, i, tl,
                                          wr_hi, wr_lo)
        idx, gc = _plan(logits_t, tri)
        xe = _gather(idx, h2.reshape(N_STREAMS, N_TOK * ROW_TILE, LANES))
        ye = _ffn(xe, gc, moe_w_gate, moe_w_up, moe_w_down, i)
        y = _scatter(idx, ye, y, mod3, i, tl, split=(i == depth - 1))
        x_ctx, x_lat = (y, 0), (y, CTX_TILES)

    y_ctx, y_lat = y
    return (y_ctx.reshape(bc, tc, D_MODEL), y_lat.reshape(bl, tl, D_MODEL),
            jnp.concatenate(new_k[0::2], axis=1), jnp.concatenate(new_v[0::2], axis=1),
            jnp.concatenate(new_k[1::2], axis=1), jnp.concatenate(new_v[1::2], axis=1))
```

```python
import functools

import jax
import jax.numpy as jnp
import numpy as np
from jax import lax
from jax.experimental import pallas as pl
from jax.experimental.pallas import tpu as pltpu

F32 = jnp.float32
BF16 = jnp.bfloat16
I32 = jnp.int32

D_MODEL = 1024
N_HEADS = 16
N_KV_HEADS = 4
HEAD_DIM = 64
GRID_W = 64
WIN_R = 8
WIN_C = 16
N_EXPERTS = 16
EXPERT_FF = 2048
ROPE_THETA = 10000.0
RMS_EPS = 1e-6
NEG_INF = -1e30
F32_MIN_NORMAL_BITS = 0x00800000
F32_VALUE_BITS = 31
REFINE_STEPS = 32
LOG2_E = 1.4426950408889634
Q_SCALE = HEAD_DIM ** -0.5 * LOG2_E

LANES = 128
MXU_DIM = 256
VMEM_LIMIT = 56 * 1024 * 1024

N_STREAMS = 2
N_TOK = 8192
CAP = 2 * N_TOK // N_EXPERTS
TM = 512
N_MOD_ROWS = 16
ROW_TILE = D_MODEL // LANES
GROUP = MXU_DIM
N_GROUPS = N_TOK // GROUP
FIN_TOK = 512
N_FIN = N_TOK // FIN_TOK
MOVE_UNROLL = 16
FF_TILE = 512
NA_QBLK = 256
NA_QROWS = NA_QBLK // GRID_W
NA_BAND = WIN_R + NA_QROWS
NA_MAX_SHIFT = WIN_R - 2 + NA_QROWS
NA_TILES = NA_MAX_SHIFT + 1
N_REL_R = 2 * WIN_R - 1
N_REL_C = 2 * WIN_C - 1


def _params(*sem):
    return pltpu.CompilerParams(dimension_semantics=sem, vmem_limit_bytes=VMEM_LIMIT)


def _dot(a, b):
    return jnp.dot(a, b, preferred_element_type=F32)


def _dot_nt(a, b):
    return lax.dot_general(a, b, (((1,), (1,)), ((), ())), preferred_element_type=F32)


def _split_bf16(x):
    hi = x.astype(BF16)
    lo = (x - hi.astype(F32)).astype(BF16)
    return hi, lo


def _silu(x):
    return x * (1.0 / (1.0 + jnp.exp(-x)))


def _ada_kernel(cond_ref, w_ref, b_ref, out_ref):
    sx = _silu(cond_ref[...])
    xh, xl = _split_bf16(sx)
    wh, wl = _split_bf16(w_ref[0])
    out_ref[0] = _dot(xh, wh) + _dot(xl, wh) + _dot(xh, wl) + b_ref[0]


def _ada(cond, w_ada, b_ada):
    depth = w_ada.shape[0]
    tn = 1024
    n_out = w_ada.shape[2]
    return pl.pallas_call(
        _ada_kernel,
        grid=(depth, n_out // tn),
        in_specs=[
            pl.BlockSpec((N_MOD_ROWS, D_MODEL), lambda l, n: (0, 0)),
            pl.BlockSpec((1, D_MODEL, tn), lambda l, n: (l, 0, n)),
            pl.BlockSpec((1, 1, tn), lambda l, n: (l, 0, n)),
        ],
        out_specs=pl.BlockSpec((1, N_MOD_ROWS, tn), lambda l, n: (l, 0, n)),
        out_shape=jax.ShapeDtypeStruct((depth, N_MOD_ROWS, n_out), F32),
        compiler_params=_params("arbitrary", "arbitrary"),
        name="ada",
    )(cond, w_ada, b_ada.reshape(depth, 1, n_out))


def _mod_row(tile, tile_rows, lat_seq):
    ctx_tiles = N_TOK // tile_rows
    return jnp.where(tile < ctx_tiles, 0, 1 + (tile - ctx_tiles) // (lat_seq // tile_rows))


def _mod_spec(layer, which, lat_seq, tile_of=lambda i: i):
    def index(i):
        return ((layer * N_MOD_ROWS + _mod_row(tile_of(i), TM, lat_seq)) * 6 + which, 0, 0)
    return pl.BlockSpec((1, 1, D_MODEL), index)


CTX_TILES = N_TOK // TM


def _x_specs(x_ctx_block0, x_lat_block0, tile_of=lambda i: i):
    return [
        pl.BlockSpec((TM, D_MODEL), lambda i: (x_ctx_block0 + jnp.minimum(tile_of(i), CTX_TILES - 1), 0)),
        pl.BlockSpec((TM, D_MODEL), lambda i: (x_lat_block0 + jnp.maximum(tile_of(i) - CTX_TILES, 0), 0)),
    ]


def _head_norm(z, seg, gain):
    ms = _dot((z * z).astype(BF16), seg)
    return z * lax.rsqrt(ms + RMS_EPS) * gain


def _rope(z, cos_t, sin_t):
    lane = lax.broadcasted_iota(I32, z.shape, 1)
    first = (lane & 32) == 0
    n = z.shape[1]
    partner = jnp.where(first, pltpu.roll(z, n - 32, axis=1), pltpu.roll(z, 32, axis=1))
    return z * cos_t + partner * sin_t


def _dup_heads(z):
    outs = []
    for b in range(z.shape[1] // LANES):
        x = z[:, b * LANES:(b + 1) * LANES]
        xr = pltpu.roll(x, HEAD_DIM, axis=1)
        lo = lax.broadcasted_iota(I32, x.shape, 1) < HEAD_DIM
        outs.append(jnp.where(lo, x, xr))
        outs.append(jnp.where(lo, xr, x))
    return jnp.concatenate(outs, axis=1)


def _store_head_rows(dst_ref, c0, z, n_heads):
    for b in range(z.shape[1] // LANES):
        pair = z[:, b * LANES:(b + 1) * LANES]
        swapped = pltpu.roll(pair, HEAD_DIM, axis=1)
        for hh, val in enumerate((pair, swapped)):
            head = c0 // HEAD_DIM + 2 * b + hh
            dst_ref[pl.ds(head, TM, stride=n_heads), :] = val[:, :HEAD_DIM]


def _qkv_kernel(*refs, kv_width, rope, emit_heads):
    it = iter(refs)
    x_ref, g_ref, shift_ref, scale_ref, w_ref, qg_ref, kg_ref, seg_ref = (next(it) for _ in range(8))
    cos_ref = sin_ref = None
    if rope:
        cos_ref, sin_ref = next(it), next(it)
    q_ref, kb_ref, vb_ref = (next(it) for _ in range(3))
    kh_ref = vh_ref = None
    if emit_heads:
        kh_ref, vh_ref = next(it), next(it)
    dup = kv_width == N_KV_HEADS * HEAD_DIM
    n_heads = kv_width // HEAD_DIM

    x = x_ref[...]
    ms = jnp.mean(x * x, axis=-1, keepdims=True)
    h = x * lax.rsqrt(ms + RMS_EPS) * (g_ref[...] * (1.0 + scale_ref[0])) + shift_ref[0]
    hb = h.astype(BF16)
    seg = seg_ref[...]
    if rope:
        cos_t, sin_t = cos_ref[...], sin_ref[...]

    q_width = N_HEADS * HEAD_DIM

    def finish_q(c0, raw):
        z = _head_norm(raw, seg, qg_ref[...])
        if rope:
            z = _rope(z, cos_t, sin_t)
        q_ref[:, c0:c0 + MXU_DIM] = z.astype(BF16)

    def finish_k(c0, raw):
        z = _head_norm(raw, seg, kg_ref[...])
        if emit_heads:
            _store_head_rows(kh_ref, c0, z, n_heads)
        if rope:
            z = _rope(z, cos_t, sin_t)
        if dup:
            kb_ref[:, 2 * c0:2 * c0 + 2 * MXU_DIM] = _dup_heads(z).astype(BF16)
        else:
            kb_ref[:, c0:c0 + MXU_DIM] = z.astype(BF16)

    def finish_v(c0, raw):
        if emit_heads:
            _store_head_rows(vh_ref, c0, raw, n_heads)
        if dup:
            vb_ref[:, 2 * c0:2 * c0 + 2 * MXU_DIM] = _dup_heads(raw).astype(BF16)
        else:
            vb_ref[:, c0:c0 + MXU_DIM] = raw.astype(BF16)

    q_chunks = [(finish_q, c0, c0) for c0 in range(0, q_width, MXU_DIM)]
    k_chunks = [(finish_k, c0, q_width + c0) for c0 in range(0, kv_width, MXU_DIM)]
    v_chunks = [(finish_v, c0, q_width + kv_width + c0) for c0 in range(0, kv_width, MXU_DIM)]
    normed = q_chunks + k_chunks
    chunks = []
    for i, chunk in enumerate(normed):
        chunks.append(chunk)
        if i * len(v_chunks) // len(normed) != (i + 1) * len(v_chunks) // len(normed):
            chunks.append(v_chunks[i * len(v_chunks) // len(normed)])

    project = lambda col: _dot(hb, w_ref[:, col:col + MXU_DIM])
    raw = project(chunks[0][2])
    for i, (finish, c0, _) in enumerate(chunks):
        nxt = project(chunks[i + 1][2]) if i + 1 < len(chunks) else None
        finish(c0, raw)
        raw = nxt

def _qkv(x_src, is_ctx, norm_g, mod3, layer, lat_seq, w_bf16, q_gain, k_gain, seg, kv_width, rope_tables):
    n_w = w_bf16.shape[1]
    kvb_width = 2 * kv_width if kv_width == N_KV_HEADS * HEAD_DIM else kv_width
    n_heads = kv_width // HEAD_DIM
    tile_of = (lambda i: i) if is_ctx else (lambda i: i + CTX_TILES)
    tile = lambda w: pl.BlockSpec((TM, w), lambda i: (i, 0))
    const = lambda shape: pl.BlockSpec(shape, lambda i: (0,) * len(shape))
    in_specs = [
        pl.BlockSpec((TM, D_MODEL), lambda i: (x_src[1] + i, 0)),
        const((1, D_MODEL)),
        _mod_spec(layer, 0, lat_seq, tile_of),
        _mod_spec(layer, 1, lat_seq, tile_of),
        const((D_MODEL, n_w)),
        const((1, MXU_DIM)),
        const((1, MXU_DIM)),
        const((MXU_DIM, MXU_DIM)),
    ]
    args = [x_src[0], norm_g.reshape(1, D_MODEL), mod3, mod3, w_bf16,
            jnp.tile(q_gain * Q_SCALE, MXU_DIM // HEAD_DIM).reshape(1, MXU_DIM),
            jnp.tile(k_gain, MXU_DIM // HEAD_DIM).reshape(1, MXU_DIM), seg]
    if rope_tables is not None:
        seq_tiles = lat_seq // TM
        in_specs += [pl.BlockSpec((TM, MXU_DIM), lambda i: (i % seq_tiles, 0))] * 2
        args += list(rope_tables)
    out_specs = [tile(D_MODEL), tile(kvb_width), tile(kvb_width)]
    out_shape = [jax.ShapeDtypeStruct((N_TOK, D_MODEL), BF16),
                 jax.ShapeDtypeStruct((N_TOK, kvb_width), BF16),
                 jax.ShapeDtypeStruct((N_TOK, kvb_width), BF16)]
    if is_ctx:
        out_specs += [pl.BlockSpec((TM * n_heads, HEAD_DIM), lambda i: (i, 0))] * 2
        out_shape += [jax.ShapeDtypeStruct((N_TOK * n_heads, HEAD_DIM), F32)] * 2
    return pl.pallas_call(
        functools.partial(_qkv_kernel, kv_width=kv_width, rope=rope_tables is not None, emit_heads=is_ctx),
        grid=(N_TOK // TM,),
        in_specs=in_specs,
        out_specs=out_specs,
        out_shape=out_shape,
        compiler_params=_params("arbitrary"),
        name="qkv",
    )(*args)


def _two_head_rows(qj):
    lo = lax.broadcasted_iota(I32, qj.shape, 1) < HEAD_DIM
    zero = jnp.zeros_like(qj)
    return jnp.concatenate([jnp.where(lo, qj, zero), jnp.where(lo, zero, qj)], axis=0)


def _merge_two_heads(r):
    tq = r.shape[0] // 2
    lo = lax.broadcasted_iota(I32, (tq, LANES), 1) < HEAD_DIM
    return jnp.where(lo, r[:tq], r[tq:])


def _attn_kernel(*refs, n_kv_blocks, has_ctx):
    if has_ctx:
        q_ref, k_ref, v_ref, ck_ref, cv_ref, o_ref = refs
    else:
        q_ref, k_ref, v_ref, o_ref = refs
    n_q_blocks = D_MODEL // LANES
    key_lanes = lambda j: slice(((j * n_kv_blocks) // n_q_blocks) * LANES,
                                ((j * n_kv_blocks) // n_q_blocks + 1) * LANES)

    def scores(j):
        q2 = _two_head_rows(q_ref[0, :, j * LANES:(j + 1) * LANES])
        s = _dot_nt(q2, k_ref[0, :, key_lanes(j)])
        sc = _dot_nt(q2, ck_ref[0, :, key_lanes(j)]) if has_ctx else None
        return s, sc

    nxt = scores(0)
    for j in range(n_q_blocks):
        (s, sc), ksl = nxt, key_lanes(j)
        if j + 1 < n_q_blocks:
            nxt = scores(j + 1)
        m = jnp.max(s, axis=-1, keepdims=True)
        if has_ctx:
            m = jnp.maximum(m, jnp.max(sc, axis=-1, keepdims=True))
        p = jnp.exp2(s - m)
        l = jnp.sum(p, axis=-1, keepdims=True)
        r = _dot(p.astype(BF16), v_ref[0, :, ksl])
        if has_ctx:
            pc = jnp.exp2(sc - m)
            l = l + jnp.sum(pc, axis=-1, keepdims=True)
            r = r + _dot(pc.astype(BF16), cv_ref[0, :, ksl])
        r = r / l
        o_ref[0, :, j * LANES:(j + 1) * LANES] = _merge_two_heads(r).astype(BF16)


def _attention(q, k, v, n_batch, batch0, ctx_k=None, ctx_v=None, tq=256):
    t = q.shape[1]
    s, w = k.shape[1], k.shape[2]
    has_ctx = ctx_k is not None
    in_specs = [
        pl.BlockSpec((1, tq, D_MODEL), lambda bi, qi: (bi + batch0, qi, 0)),
        pl.BlockSpec((1, s, w), lambda bi, qi: (bi + batch0, 0, 0)),
        pl.BlockSpec((1, s, w), lambda bi, qi: (bi + batch0, 0, 0)),
    ]
    args = [q, k, v]
    if has_ctx:
        sc = ctx_k.shape[1]
        in_specs += [pl.BlockSpec((1, sc, w), lambda bi, qi: (bi, 0, 0))] * 2
        args += [ctx_k, ctx_v]
    return pl.pallas_call(
        functools.partial(_attn_kernel, n_kv_blocks=w // LANES, has_ctx=has_ctx),
        grid=(n_batch, t // tq),
        in_specs=in_specs,
        out_specs=pl.BlockSpec((1, tq, D_MODEL), lambda bi, qi: (bi, qi, 0)),
        out_shape=jax.ShapeDtypeStruct((n_batch, t, D_MODEL), BF16),
        compiler_params=_params("arbitrary", "arbitrary"),
        name="attn",
    )(*args)


def _na_kernel(q_ref, k_ref, v_ref, ck_ref, cv_ref, bias_ref, o_ref, *, rows):
    blk = pl.program_id(1)
    q_row0 = NA_QROWS * blk
    band_start = jnp.minimum(jnp.clip(q_row0 - WIN_R // 2, 0, rows - WIN_R), rows - NA_BAND)
    n_band = NA_BAND * GRID_W
    band = pl.ds(pl.multiple_of(band_start * GRID_W, LANES), n_band)
    shape = (NA_QBLK, n_band)
    q_r = q_row0 + lax.broadcasted_iota(I32, shape, 0) // GRID_W
    k_r = band_start + lax.broadcasted_iota(I32, shape, 1) // GRID_W
    q_rs = jnp.clip(q_r - WIN_R // 2, 0, rows - WIN_R)
    row_ok1 = (k_r >= q_rs) & (k_r < q_rs + WIN_R)
    row_ok = jnp.concatenate([row_ok1, row_ok1], axis=0)
    shift0 = (band_start - q_row0 + NA_MAX_SHIFT) // 2
    n_q_blocks = D_MODEL // LANES

    def scores(j):
        sl = slice(j * LANES, (j + 1) * LANES)
        q2 = _two_head_rows(q_ref[0, :, sl])
        return _dot_nt(q2, k_ref[0, band, sl]), _dot_nt(q2, ck_ref[0, :, sl])

    nxt = scores(0)
    for j in range(n_q_blocks):
        sl = slice(j * LANES, (j + 1) * LANES)
        s, sc = nxt
        if j + 1 < n_q_blocks:
            nxt = scores(j + 1)
        bias = jnp.concatenate(
            [jnp.concatenate([bias_ref[2 * j + hh, shift0 + m - u] for m in range(NA_BAND // 2)], axis=1)
             for hh in range(2) for u in range(NA_QROWS // 2)], axis=0)
        s = jnp.where(row_ok, s + bias, NEG_INF)
        m = jnp.maximum(jnp.max(s, axis=-1, keepdims=True), jnp.max(sc, axis=-1, keepdims=True))
        p = jnp.exp2(s - m)
        pc = jnp.exp2(sc - m)
        l = jnp.sum(p, axis=-1, keepdims=True) + jnp.sum(pc, axis=-1, keepdims=True)
        r = _dot(p.astype(BF16), v_ref[0, band, sl]) + _dot(pc.astype(BF16), cv_ref[0, :, sl])
        r = r / l
        o_ref[0, :, sl] = _merge_two_heads(r).astype(BF16)


def _na_bias_kernel(rpb_ref, out_ref, tz_ref):
    h = pl.program_id(0)
    shape = (GRID_W, LANES)
    q_col = lax.broadcasted_iota(I32, shape, 0)
    lane = lax.broadcasted_iota(I32, shape, 1)
    k_col = lane & (GRID_W - 1)
    col_start = jnp.clip(q_col - WIN_C // 2, 0, GRID_W - WIN_C)
    col_ok = (k_col >= col_start) & (k_col < col_start + WIN_C)
    rel_c = jnp.clip(k_col - q_col + WIN_C - 1, 0, N_REL_C - 1)
    for a in range(N_REL_R):
        acc = jnp.zeros(shape, F32)
        for b in range(N_REL_C):
            acc = jnp.where(rel_c == b, rpb_ref[(h * N_REL_R + a) * N_REL_C + b], acc)
        tz_ref[a] = jnp.where(col_ok, acc * LOG2_E, NEG_INF)
    left = lane < GRID_W
    for dd in range(NA_TILES):
        for qr in range(2):
            rel = [min(max(2 * dd - NA_MAX_SHIFT + kr - qr + WIN_R - 1, 0), N_REL_R - 1) for kr in range(2)]
            out_ref[0, dd, qr * GRID_W:(qr + 1) * GRID_W, :] = jnp.where(left, tz_ref[rel[0]], tz_ref[rel[1]])


def _na_bias_table(rpb):
    return pl.pallas_call(
        _na_bias_kernel,
        grid=(N_HEADS,),
        in_specs=[pl.BlockSpec(memory_space=pltpu.SMEM)],
        out_specs=pl.BlockSpec((1, NA_TILES, LANES, LANES), lambda h: (h, 0, 0, 0)),
        out_shape=jax.ShapeDtypeStruct((N_HEADS, NA_TILES, LANES, LANES), F32),
        scratch_shapes=[pltpu.VMEM((N_REL_R, GRID_W, LANES), F32)],
        compiler_params=_params("arbitrary"),
        name="na_bias",
    )(rpb.reshape(-1))


def _na_attention(q, k, v, n_batch, batch0, ctx_k, ctx_v, bias_tbl):
    t = q.shape[1]
    sc = ctx_k.shape[1]
    rows = t // GRID_W
    full = lambda n, off: pl.BlockSpec((1, n, D_MODEL), lambda bi, qi: (bi + off, 0, 0))
    return pl.pallas_call(
        functools.partial(_na_kernel, rows=rows),
        grid=(n_batch, t // NA_QBLK),
        in_specs=[
            pl.BlockSpec((1, NA_QBLK, D_MODEL), lambda bi, qi: (bi + batch0, qi, 0)),
            full(t, batch0), full(t, batch0), full(sc, 0), full(sc, 0),
            pl.BlockSpec(bias_tbl.shape, lambda bi, qi: (0, 0, 0, 0)),
        ],
        out_specs=pl.BlockSpec((1, NA_QBLK, D_MODEL), lambda bi, qi: (bi, qi, 0)),
        out_shape=jax.ShapeDtypeStruct((n_batch, t, D_MODEL), BF16),
        compiler_params=_params("arbitrary", "arbitrary"),
        name="na_attn",
    )(q, k, v, ctx_k, ctx_v, bias_tbl)


def _post_kernel(oc_ref, ol_ref, xc_ref, xl_ref, wo_ref, gate_ref, g_ref, shift_ref, scale_ref,
                 wrh_ref, wrl_ref, y_ref, h_ref, lg_ref):
    is_ctx = pl.program_id(0) < CTX_TILES
    o = jnp.where(is_ctx, oc_ref[...], ol_ref[...])
    x = jnp.where(is_ctx, xc_ref[...], xl_ref[...])
    y = x + gate_ref[0] * _dot(o, wo_ref[...])
    y_ref[...] = y
    ms = jnp.mean(y * y, axis=-1, keepdims=True)
    h = y * lax.rsqrt(ms + RMS_EPS) * (g_ref[...] * (1.0 + scale_ref[0])) + shift_ref[0]
    for k in range(ROW_TILE):
        h_ref[pl.ds(k, TM, stride=ROW_TILE), :] = h[:, k * LANES:(k + 1) * LANES]
    hh, hl = _split_bf16(h)
    lg_ref[...] = _dot_nt(wrh_ref[...], hh) + _dot_nt(wrh_ref[...], hl) + _dot_nt(wrl_ref[...], hh)


def _post_attention(o_ctx, o_lat, x_ctx, x_lat, wo_bf16, norm_g, mod3, layer, lat_seq, wr_hi, wr_lo):
    n_tok = N_STREAMS * N_TOK
    tile = lambda w: pl.BlockSpec((TM, w), lambda i: (i, 0))
    const = lambda shape: pl.BlockSpec(shape, lambda i: (0,) * len(shape))
    return pl.pallas_call(
        _post_kernel,
        grid=(n_tok // TM,),
        in_specs=_x_specs(0, 0) + _x_specs(x_ctx[1], x_lat[1]) + [
            const((D_MODEL, D_MODEL)),
            _mod_spec(layer, 2, lat_seq),
            const((1, D_MODEL)),
            _mod_spec(layer, 3, lat_seq),
            _mod_spec(layer, 4, lat_seq),
            const((N_EXPERTS, D_MODEL)), const((N_EXPERTS, D_MODEL)),
        ],
        out_specs=[tile(D_MODEL), pl.BlockSpec((TM * ROW_TILE, LANES), lambda i: (i, 0)),
                   pl.BlockSpec((N_EXPERTS, TM), lambda i: (0, i))],
        out_shape=[jax.ShapeDtypeStruct((n_tok, D_MODEL), F32),
                   jax.ShapeDtypeStruct((n_tok * ROW_TILE, LANES), F32),
                   jax.ShapeDtypeStruct((N_EXPERTS, n_tok), F32)],
        compiler_params=_params("arbitrary"),
        name="post_attn",
    )(o_ctx, o_lat, x_ctx[0], x_lat[0], wo_bf16, mod3, norm_g.reshape(1, D_MODEL), mod3, mod3, wr_hi, wr_lo)


def _group_prefix(mask, tri):
    local, offs = [], []
    off = jnp.zeros((mask.shape[0], 1), F32)
    for g in range(mask.shape[1] // GROUP):
        xg = mask[:, g * GROUP:(g + 1) * GROUP]
        offs.append(off)
        local.append(_dot(xg.astype(BF16), tri))
        off = off + jnp.sum(xg, axis=1, keepdims=True)
    offs.append(off)
    return local, offs


def _split3_bf16(x):
    hi = x.astype(BF16)
    r1 = x - hi.astype(F32)
    mid = r1.astype(BF16)
    lo = (r1 - mid.astype(F32)).astype(BF16)
    return hi, mid, lo


def _plan_kernel(lg_ref, tri_ref, idx_ref, gc_ref, cnt_ref, affg_ref):
    lg = lg_ref[...]
    ex = jnp.exp(lg - jnp.max(lg, axis=0, keepdims=True))
    aff = ex / jnp.sum(ex, axis=0, keepdims=True)

    def count_ge(v):
        return jnp.sum(jnp.where(aff >= v, 1.0, 0.0), axis=1, keepdims=True)

    def search(i, thr):
        cand = thr | jnp.left_shift(jnp.int32(1), F32_VALUE_BITS - 1 - i)
        ok = (count_ge(lax.bitcast_convert_type(cand, F32)) >= CAP) & (cand >= F32_MIN_NORMAL_BITS)
        return jnp.where(ok, cand, thr)

    thr = lax.fori_loop(0, F32_VALUE_BITS, search, jnp.zeros((N_EXPERTS, 1), I32))
    lo = lax.bitcast_convert_type(thr, F32)
    hi = lax.bitcast_convert_type(jnp.maximum(thr + 1, F32_MIN_NORMAL_BITS), F32)

    def refine(i, bounds):
        lo, hi = bounds
        mid = lo + (hi - lo) * 0.5
        ok = count_ge(mid) >= CAP
        return jnp.where(ok, mid, lo), jnp.where(ok, hi, mid)

    lo, hi = lax.fori_loop(0, REFINE_STEPS, refine, (lo, hi))
    tri = tri_ref[...]
    gt = aff >= hi
    eq = jnp.where((aff >= lo) & (aff < hi), 1.0, 0.0)
    need = CAP - jnp.sum(jnp.where(gt, 1.0, 0.0), axis=1, keepdims=True)
    eq_local, eq_offs = _group_prefix(eq, tri)
    eq_rank = jnp.concatenate([eq_local[g] + eq_offs[g] for g in range(N_GROUPS)], axis=1)
    sel = jnp.where(gt | ((eq > 0.0) & (eq_rank < need)), 1.0, 0.0)
    sel_local, offs = _group_prefix(sel, tri)
    for g in range(N_GROUPS):
        sl = slice(g * GROUP, (g + 1) * GROUP)
        cnt_ref[g] = sel_local[g] + sel[:, sl]
        affg_ref[g] = aff[:, sl]

    lane = lax.broadcasted_iota(I32, (N_EXPERTS, LANES), 1)
    never = jnp.full((N_EXPERTS, LANES), 2.0 * N_TOK, F32)
    grp_lo, grp_hi = never, never
    for g in range(N_GROUPS):
        grp_lo = jnp.where(lane == g, offs[g], grp_lo)
        grp_hi = jnp.where(lane == g, offs[g + 1], grp_hi)
    row = lax.broadcasted_iota(I32, (CAP, LANES), 0).astype(F32)
    in_group_lane = lax.broadcasted_iota(I32, (CAP, GROUP), 1).astype(F32)
    zpad = jnp.zeros((LANES - N_GROUPS, GROUP), BF16)
    for e in range(N_EXPERTS):
        lo_row, hi_row = grp_lo[e:e + 1, :], grp_hi[e:e + 1, :]
        in_grp = (lo_row <= row) & (row < hi_row)
        onehot = jnp.where(in_grp, 1.0, 0.0).astype(BF16)
        counts = _dot(onehot, jnp.concatenate([cnt_ref[:, e, :].astype(BF16), zpad], axis=0))
        rank = row[:, 0:1] - jnp.sum(jnp.where(in_grp, lo_row, 0.0), axis=1, keepdims=True)
        local = jnp.sum(jnp.where(counts <= rank, 1.0, 0.0), axis=1, keepdims=True)
        grp = jnp.sum(jnp.where(hi_row <= row, 1.0, 0.0), axis=1, keepdims=True)
        tok = grp * GROUP + local
        aff_rows = sum(_dot(onehot, jnp.concatenate([part, zpad], axis=0))
                       for part in _split3_bf16(affg_ref[:, e, :]))
        gate = jnp.sum(jnp.where(in_group_lane == local, aff_rows, 0.0), axis=1, keepdims=True)
        gc_ref[0, e] = jnp.broadcast_to(gate, (CAP, LANES))
        tok_b = jnp.broadcast_to(tok, (CAP, LANES))
        tok_row = jnp.concatenate([tok_b[t * LANES:(t + 1) * LANES, :].T[0:1, :]
                                   for t in range(CAP // LANES)], axis=1)
        idx_ref[0, e:e + 1, :] = tok_row.astype(I32)


def _plan(logits_t, tri):
    ns = N_STREAMS
    return pl.pallas_call(
        _plan_kernel,
        grid=(ns,),
        in_specs=[pl.BlockSpec((N_EXPERTS, N_TOK), lambda s: (0, s)),
                  pl.BlockSpec((MXU_DIM, MXU_DIM), lambda s: (0, 0))],
        out_specs=[pl.BlockSpec((1, N_EXPERTS, CAP), lambda s: (s, 0, 0)),
                   pl.BlockSpec((1, N_EXPERTS, CAP, LANES), lambda s: (s, 0, 0, 0))],
        out_shape=[jax.ShapeDtypeStruct((ns, N_EXPERTS, CAP), I32),
                   jax.ShapeDtypeStruct((ns, N_EXPERTS, CAP, LANES), F32)],
        scratch_shapes=[pltpu.VMEM((N_GROUPS, N_EXPERTS, GROUP), F32),
                        pltpu.VMEM((N_GROUPS, N_EXPERTS, GROUP), F32)],
        compiler_params=_params("arbitrary"),
        name="plan",
    )(logits_t, tri)


def _tile_rows(r):
    return pl.ds(pl.multiple_of(r * ROW_TILE, ROW_TILE), ROW_TILE)


def _gather_kernel(idx_ref, h_ref, xe_ref, stage_ref):
    sub = 8
    groups = MOVE_UNROLL // sub

    def move(g, carry):
        for u in range(MOVE_UNROLL):
            tile = h_ref[0, _tile_rows(idx_ref[0, 0, 0, g * MOVE_UNROLL + u]), :]
            stage_ref[g * groups + u // sub, pl.ds(u % sub, ROW_TILE, stride=sub), :] = tile
        return carry

    lax.fori_loop(0, CAP // MOVE_UNROLL, move, 0)
    for k in range(ROW_TILE):
        rows = stage_ref[:, k * sub:(k + 1) * sub, :].reshape(CAP, LANES)
        xe_ref[0, 0, :, k * LANES:(k + 1) * LANES] = rows.astype(BF16)


def _gather(idx, h2_tiles):
    ns = h2_tiles.shape[0]
    return pl.pallas_call(
        _gather_kernel,
        grid=(ns, N_EXPERTS),
        in_specs=[pl.BlockSpec((1, 1, 1, CAP), lambda s, e: (s, e, 0, 0), memory_space=pltpu.SMEM),
                  pl.BlockSpec((1, N_TOK * ROW_TILE, LANES), lambda s, e: (s, 0, 0),
                               pipeline_mode=pl.Buffered(1))],
        out_specs=pl.BlockSpec((1, 1, CAP, D_MODEL), lambda s, e: (s, e, 0, 0)),
        scratch_shapes=[pltpu.VMEM((CAP // 8, ROW_TILE * 8, LANES), F32)],
        out_shape=jax.ShapeDtypeStruct((ns, N_EXPERTS, CAP, D_MODEL), BF16),
        compiler_params=_params("arbitrary", "arbitrary"),
        name="moe_gather",
    )(idx.reshape(ns, N_EXPERTS, 1, CAP), h2_tiles)


def _ffn_kernel(x_ref, wg_ref, wu_ref, wd_ref, gc_ref, ye_ref, acc_ref, *, row_tile):
    f = pl.program_id(1)
    last = pl.num_programs(1) - 1
    ns = x_ref.shape[0]
    wg = wg_ref[0, 0].astype(BF16)
    wu = wu_ref[0, 0].astype(BF16)
    wd = wd_ref[0, 0].astype(BF16)

    tiles = [(s, r0) for s in range(ns) for r0 in range(0, CAP, row_tile)]

    def gate_up(s, r0):
        x = x_ref[s, 0, r0:r0 + row_tile, :]
        return _dot(x, wg), _dot(x, wu)

    def for_each_tile(emit):
        nxt = gate_up(*tiles[0])
        for i, (s, r0) in enumerate(tiles):
            gate, up = nxt
            if i + 1 < len(tiles):
                nxt = gate_up(*tiles[i + 1])
            emit(s, r0, _dot((_silu(gate) * up).astype(BF16), wd))

    def first(s, r0, part):
        acc_ref[s, r0:r0 + row_tile, :] = part

    def middle(s, r0, part):
        acc_ref[s, r0:r0 + row_tile, :] += part

    def final(s, r0, part):
        ye = (acc_ref[s, r0:r0 + row_tile, :] + part) * gc_ref[s, 0, r0:r0 + row_tile, 0:1]
        for k in range(ROW_TILE):
            ye_ref[s, 0, pl.ds(r0 * ROW_TILE + k, row_tile, stride=ROW_TILE), :] = (
                ye[:, k * LANES:(k + 1) * LANES])

    pl.when(f == 0)(lambda: for_each_tile(first))
    pl.when((f > 0) & (f < last))(lambda: for_each_tile(middle))
    pl.when(f == last)(lambda: for_each_tile(final))


def _ffn(xe, gc, w_gate, w_up, w_down, layer):
    ns = xe.shape[0]
    return pl.pallas_call(
        functools.partial(_ffn_kernel, row_tile=512),
        grid=(N_EXPERTS, EXPERT_FF // FF_TILE),
        in_specs=[
            pl.BlockSpec((ns, 1, CAP, D_MODEL), lambda e, f: (0, e, 0, 0)),
            pl.BlockSpec((1, 1, D_MODEL, FF_TILE), lambda e, f: (layer, e, 0, f)),
            pl.BlockSpec((1, 1, D_MODEL, FF_TILE), lambda e, f: (layer, e, 0, f)),
            pl.BlockSpec((1, 1, FF_TILE, D_MODEL), lambda e, f: (layer, e, f, 0)),
            pl.BlockSpec((ns, 1, CAP, LANES), lambda e, f: (0, e, 0, 0)),
        ],
        out_specs=pl.BlockSpec((ns, 1, CAP * ROW_TILE, LANES), lambda e, f: (0, e, 0, 0)),
        out_shape=jax.ShapeDtypeStruct((ns, N_EXPERTS, CAP * ROW_TILE, LANES), F32),
        scratch_shapes=[pltpu.VMEM((ns, CAP, D_MODEL), F32)],
        compiler_params=_params("arbitrary", "arbitrary"),
        name="moe_ffn",
    )(xe, w_gate, w_up, w_down, gc)


def _scatter_kernel(idx_ref, ye_ref, y_ref, gate_ref, *refs, split):
    out_refs, acc_ref = refs[:-1], refs[-1]
    s, j = pl.program_id(0), pl.program_id(1)

    @pl.when(j == 0)
    def _():
        acc_ref[...] = jnp.zeros_like(acc_ref)

    @pl.when(j < N_EXPERTS)
    def _():
        def add_rows(g, carry):
            r0 = g * MOVE_UNROLL
            group = ye_ref.at[0, 0, pl.ds(pl.multiple_of(r0 * ROW_TILE, MOVE_UNROLL * ROW_TILE),
                                          MOVE_UNROLL * ROW_TILE), :]
            dst = [_tile_rows(idx_ref[0, 0, 0, r0 + u]) for u in range(MOVE_UNROLL)]
            val = [acc_ref[dst[u], :] + group[u * ROW_TILE:(u + 1) * ROW_TILE, :] for u in range(MOVE_UNROLL)]
            for u in range(MOVE_UNROLL):
                acc_ref[dst[u], :] = val[u]
            return carry

        lax.fori_loop(0, CAP // MOVE_UNROLL, add_rows, 0)

    @pl.when(j >= N_EXPERTS)
    def _():
        c = j - N_EXPERTS
        rows = FIN_TOK * ROW_TILE
        part = acc_ref.at[pl.ds(pl.multiple_of(c * rows, rows), rows), :]
        moe = jnp.concatenate([part[pl.ds(k, FIN_TOK, stride=ROW_TILE), :] for k in range(ROW_TILE)], axis=1)
        res = y_ref[...] + gate_ref[0] * moe
        if split:
            for si, out_ref in enumerate(out_refs):
                @pl.when(s == si)
                def _(out_ref=out_ref):
                    out_ref[...] = res
        else:
            out_refs[0][...] = res


def _scatter(idx, ye_tiles, y, mod3, layer, lat_seq, split):
    ns = ye_tiles.shape[0]
    fin = lambda j: jnp.maximum(j - N_EXPERTS, 0)

    def gate_index(s, j):
        row = _mod_row(s * N_FIN + fin(j), FIN_TOK, lat_seq)
        return ((layer * N_MOD_ROWS + row) * 6 + 5, 0, 0)

    chunk = (FIN_TOK, D_MODEL)
    if split:
        out_specs = [
            pl.BlockSpec(chunk, lambda s, j, si=si: (
                jnp.where(s == si, fin(j), jnp.where(s < si, 0, N_FIN - 1)), 0))
            for si in range(ns)]
        out_shape = [jax.ShapeDtypeStruct((N_TOK, D_MODEL), F32)] * ns
    else:
        out_specs = pl.BlockSpec(chunk, lambda s, j: (s * N_FIN + fin(j), 0))
        out_shape = jax.ShapeDtypeStruct(y.shape, F32)
    expert = lambda j: jnp.minimum(j, N_EXPERTS - 1)
    return pl.pallas_call(
        functools.partial(_scatter_kernel, split=split),
        grid=(ns, N_EXPERTS + N_FIN),
        in_specs=[
            pl.BlockSpec((1, 1, 1, CAP), lambda s, j: (s, expert(j), 0, 0), memory_space=pltpu.SMEM),
            pl.BlockSpec((1, 1, CAP * ROW_TILE, LANES), lambda s, j: (s, expert(j), 0, 0)),
            pl.BlockSpec(chunk, lambda s, j: (s * N_FIN + fin(j), 0)),
            pl.BlockSpec((1, 1, D_MODEL), gate_index),
        ],
        out_specs=out_specs,
        scratch_shapes=[pltpu.VMEM((N_TOK * ROW_TILE, LANES), F32)],
        out_shape=out_shape,
        compiler_params=_params("arbitrary", "arbitrary"),
        name="moe_scatter",
    )(idx.reshape(ns, N_EXPERTS, 1, CAP), ye_tiles, y, mod3)


def _rope_tables(n_tokens):
    t = np.arange(n_tokens)
    row = (t // GRID_W).astype(np.float32)
    col = (t % GRID_W).astype(np.float32)
    pairs = HEAD_DIM // 4
    inv_freq = ROPE_THETA ** (-jnp.arange(pairs, dtype=F32) / pairs)
    ang = jnp.concatenate([row[:, None] * inv_freq, col[:, None] * inv_freq], axis=-1)
    cos, sin = jnp.cos(ang), jnp.sin(ang)
    reps = MXU_DIM // HEAD_DIM
    return (jnp.tile(jnp.concatenate([cos, cos], axis=-1), (1, reps)),
            jnp.tile(jnp.concatenate([-sin, sin], axis=-1), (1, reps)))


def _dup_cache(cache):
    b, s, hk, hd = cache.shape
    return jnp.broadcast_to(cache[:, :, :, None, :], (b, s, hk, 2, hd)).reshape(b, s, 2 * hk * hd).astype(BF16)


def kernel(x_prompt, x_sample, cache_attn_k, cache_attn_v, cache_na_k, cache_na_v, c, c_ctx,
           norm1_g, norm2_g, w_ada, b_ada, attn_w_qkv, attn_q_gain, attn_k_gain, attn_w_o,
           na_w_qkv, na_q_gain, na_k_gain, na_rpb, na_w_o,
           moe_w_router, moe_w_gate, moe_w_up, moe_w_down):
    bc, tc, _ = x_prompt.shape
    bl, tl, _ = x_sample.shape
    depth = w_ada.shape[0]
    assert bc * tc == N_TOK and bl * tl == N_TOK and 1 + bl <= N_MOD_ROWS
    assert tl % TM == 0 and tl % FIN_TOK == 0
    n_all = N_STREAMS * N_TOK

    cond = jnp.zeros((N_MOD_ROWS, D_MODEL), F32).at[0].set(c_ctx).at[1:1 + bl].set(c)
    mod3 = _ada(cond, w_ada, b_ada).reshape(depth * N_MOD_ROWS * 6, 1, D_MODEL)

    seg = jnp.asarray(np.kron(np.eye(MXU_DIM // HEAD_DIM), np.ones((HEAD_DIM, HEAD_DIM))) / HEAD_DIM, BF16)
    tri = jnp.asarray(np.triu(np.ones((MXU_DIM, MXU_DIM)), k=1), BF16)
    rope_tables = _rope_tables(tl)

    x_ctx = (x_prompt.reshape(N_TOK, D_MODEL), 0)
    x_lat = (x_sample.reshape(N_TOK, D_MODEL), 0)
    new_k, new_v = [], []
    for i in range(depth):
        j = i // 2
        gqa = i % 2 == 0
        if gqa:
            w_qkv, q_gain, k_gain, w_o = attn_w_qkv[j], attn_q_gain[j], attn_k_gain[j], attn_w_o[j]
            kv_heads = N_KV_HEADS
        else:
            w_qkv, q_gain, k_gain, w_o = na_w_qkv[j], na_q_gain[j], na_k_gain[j], na_w_o[j]
            kv_heads = N_HEADS
        wr_hi, wr_lo = _split_bf16(moe_w_router[i].T)

        w_qkv_b = w_qkv.astype(BF16)
        qkv_args = (norm1_g[i], mod3, i, tl, w_qkv_b, q_gain, k_gain, seg, kv_heads * HEAD_DIM)
        q_c, kb_c, vb_c, kf, vf = _qkv(x_ctx, True, *qkv_args, None)
        q_l, kb_l, vb_l = _qkv(x_lat, False, *qkv_args, rope_tables if gqa else None)
        o_ctx = _attention(*(a.reshape(bc, tc, a.shape[1]) for a in (q_c, kb_c, vb_c)), bc, 0)
        q_l, kb_l, vb_l = (a.reshape(bl, tl, a.shape[1]) for a in (q_l, kb_l, vb_l))
        if gqa:
            o_lat = _attention(q_l, kb_l, vb_l, bl, 0,
                               _dup_cache(cache_attn_k[:, j]), _dup_cache(cache_attn_v[:, j]), tq=512)
        else:
            past = cache_na_k.shape[2]
            o_lat = _na_attention(q_l, kb_l, vb_l, bl, 0,
                                  cache_na_k[:, j].reshape(bl, past, D_MODEL).astype(BF16),
                                  cache_na_v[:, j].reshape(bl, past, D_MODEL).astype(BF16),
                                  _na_bias_table(na_rpb[j]))
        new_k.append(kf.reshape(bc, 1, tc, kv_heads, HEAD_DIM))
        new_v.append(vf.reshape(bc, 1, tc, kv_heads, HEAD_DIM))

        y, h2, logits_t = _post_attention(o_ctx.reshape(N_TOK, D_MODEL), o_lat.reshape(N_TOK, D_MODEL),
                                          x_ctx, x_lat, w_o.astype(BF16), norm2_g[i], mod3, i, tl,
                                          wr_hi, wr_lo)
        idx, gc = _plan(logits_t, tri)
        xe = _gather(idx, h2.reshape(N_STREAMS, N_TOK * ROW_TILE, LANES))
        ye = _ffn(xe, gc, moe_w_gate, moe_w_up, moe_w_down, i)
        y = _scatter(idx, ye, y, mod3, i, tl, split=(i == depth - 1))
        x_ctx, x_lat = (y, 0), (y, CTX_TILES)

    y_ctx, y_lat = y
    return (y_ctx.reshape(bc, tc, D_MODEL), y_lat.reshape(bl, tl, D_MODEL),
            jnp.concatenate(new_k[0::2], axis=1), jnp.concatenate(new_v[0::2], axis=1),
            jnp.concatenate(new_k[1::2], axis=1), jnp.concatenate(new_v[1::2], axis=1))
```

```python
import functools

import jax
import jax.numpy as jnp
import numpy as np
from jax import lax
from jax.experimental import pallas as pl
from jax.experimental.pallas import tpu as pltpu

F32 = jnp.float32
BF16 = jnp.bfloat16
I32 = jnp.int32

D_MODEL = 1024
N_HEADS = 16
N_KV_HEADS = 4
HEAD_DIM = 64
GRID_W = 64
WIN_R = 8
WIN_C = 16
N_EXPERTS = 16
EXPERT_FF = 2048
ROPE_THETA = 10000.0
RMS_EPS = 1e-6
NEG_INF = -1e30
F32_MIN_NORMAL_BITS = 0x00800000
F32_VALUE_BITS = 31
REFINE_STEPS = 32
LOG2_E = 1.4426950408889634
Q_SCALE = HEAD_DIM ** -0.5 * LOG2_E

LANES = 128
MXU_DIM = 256
VMEM_LIMIT = 56 * 1024 * 1024

N_STREAMS = 2
N_TOK = 8192
CAP = 2 * N_TOK // N_EXPERTS
TM = 512
N_MOD_ROWS = 16
ROW_TILE = D_MODEL // LANES
GROUP = MXU_DIM
N_GROUPS = N_TOK // GROUP
FIN_TOK = 512
N_FIN = N_TOK // FIN_TOK
MOVE_UNROLL = 16
FF_TILE = 512
NA_QBLK = 256
NA_QROWS = NA_QBLK // GRID_W
NA_BAND = WIN_R + NA_QROWS
NA_MAX_SHIFT = WIN_R - 2 + NA_QROWS
NA_TILES = NA_MAX_SHIFT + 1
N_REL_R = 2 * WIN_R - 1
N_REL_C = 2 * WIN_C - 1


def _params(*sem):
    return pltpu.CompilerParams(dimension_semantics=sem, vmem_limit_bytes=VMEM_LIMIT)


def _dot(a, b):
    return jnp.dot(a, b, preferred_element_type=F32)


def _dot_nt(a, b):
    return lax.dot_general(a, b, (((1,), (1,)), ((), ())), preferred_element_type=F32)


def _split_bf16(x):
    hi = x.astype(BF16)
    lo = (x - hi.astype(F32)).astype(BF16)
    return hi, lo


def _silu(x):
    return x * (1.0 / (1.0 + jnp.exp(-x)))


def _ada_kernel(cond_ref, w_ref, b_ref, out_ref):
    sx = _silu(cond_ref[...])
    xh, xl = _split_bf16(sx)
    wh, wl = _split_bf16(w_ref[0])
    out_ref[0] = _dot(xh, wh) + _dot(xl, wh) + _dot(xh, wl) + b_ref[0]


def _ada(cond, w_ada, b_ada):
    depth = w_ada.shape[0]
    tn = 1024
    n_out = w_ada.shape[2]
    return pl.pallas_call(
        _ada_kernel,
        grid=(depth, n_out // tn),
        in_specs=[
            pl.BlockSpec((N_MOD_ROWS, D_MODEL), lambda l, n: (0, 0)),
            pl.BlockSpec((1, D_MODEL, tn), lambda l, n: (l, 0, n)),
            pl.BlockSpec((1, 1, tn), lambda l, n: (l, 0, n)),
        ],
        out_specs=pl.BlockSpec((1, N_MOD_ROWS, tn), lambda l, n: (l, 0, n)),
        out_shape=jax.ShapeDtypeStruct((depth, N_MOD_ROWS, n_out), F32),
        compiler_params=_params("arbitrary", "arbitrary"),
        name="ada",
    )(cond, w_ada, b_ada.reshape(depth, 1, n_out))


def _mod_row(tile, tile_rows, lat_seq):
    ctx_tiles = N_TOK // tile_rows
    return jnp.where(tile < ctx_tiles, 0, 1 + (tile - ctx_tiles) // (lat_seq // tile_rows))


def _mod_spec(layer, which, lat_seq, tile_of=lambda i: i):
    def index(i):
        return ((layer * N_MOD_ROWS + _mod_row(tile_of(i), TM, lat_seq)) * 6 + which, 0, 0)
    return pl.BlockSpec((1, 1, D_MODEL), index)


CTX_TILES = N_TOK // TM


def _x_specs(x_ctx_block0, x_lat_block0, tile_of=lambda i: i):
    return [
        pl.BlockSpec((TM, D_MODEL), lambda i: (x_ctx_block0 + jnp.minimum(tile_of(i), CTX_TILES - 1), 0)),
        pl.BlockSpec((TM, D_MODEL), lambda i: (x_lat_block0 + jnp.maximum(tile_of(i) - CTX_TILES, 0), 0)),
    ]


def _head_norm(z, seg, gain):
    ms = _dot((z * z).astype(BF16), seg)
    return z * lax.rsqrt(ms + RMS_EPS) * gain


def _rope(z, cos_t, sin_t):
    lane = lax.broadcasted_iota(I32, z.shape, 1)
    first = (lane & 32) == 0
    n = z.shape[1]
    partner = jnp.where(first, pltpu.roll(z, n - 32, axis=1), pltpu.roll(z, 32, axis=1))
    return z * cos_t + partner * sin_t


def _dup_heads(z):
    outs = []
    for b in range(z.shape[1] // LANES):
        x = z[:, b * LANES:(b + 1) * LANES]
        xr = pltpu.roll(x, HEAD_DIM, axis=1)
        lo = lax.broadcasted_iota(I32, x.shape, 1) < HEAD_DIM
        outs.append(jnp.where(lo, x, xr))
        outs.append(jnp.where(lo, xr, x))
    return jnp.concatenate(outs, axis=1)


def _store_head_rows(dst_ref, c0, z, n_heads):
    for b in range(z.shape[1] // LANES):
        pair = z[:, b * LANES:(b + 1) * LANES]
        swapped = pltpu.roll(pair, HEAD_DIM, axis=1)
        for hh, val in enumerate((pair, swapped)):
            head = c0 // HEAD_DIM + 2 * b + hh
            dst_ref[pl.ds(head, TM, stride=n_heads), :] = val[:, :HEAD_DIM]


def _qkv_kernel(*refs, kv_width, rope, emit_heads):
    it = iter(refs)
    x_ref, g_ref, shift_ref, scale_ref, w_ref, qg_ref, kg_ref, seg_ref = (next(it) for _ in range(8))
    cos_ref = sin_ref = None
    if rope:
        cos_ref, sin_ref = next(it), next(it)
    q_ref, kb_ref, vb_ref = (next(it) for _ in range(3))
    kh_ref = vh_ref = None
    if emit_heads:
        kh_ref, vh_ref = next(it), next(it)
    dup = kv_width == N_KV_HEADS * HEAD_DIM
    n_heads = kv_width // HEAD_DIM

    x = x_ref[...]
    ms = jnp.mean(x * x, axis=-1, keepdims=True)
    h = x * lax.rsqrt(ms + RMS_EPS) * (g_ref[...] * (1.0 + scale_ref[0])) + shift_ref[0]
    hb = h.astype(BF16)
    seg = seg_ref[...]
    if rope:
        cos_t, sin_t = cos_ref[...], sin_ref[...]

    q_width = N_HEADS * HEAD_DIM

    def finish_q(c0, raw):
        z = _head_norm(raw, seg, qg_ref[...])
        if rope:
            z = _rope(z, cos_t, sin_t)
        q_ref[:, c0:c0 + MXU_DIM] = z.astype(BF16)

    def finish_k(c0, raw):
        z = _head_norm(raw, seg, kg_ref[...])
        if emit_heads:
            _store_head_rows(kh_ref, c0, z, n_heads)
        if rope:
            z = _rope(z, cos_t, sin_t)
        if dup:
            kb_ref[:, 2 * c0:2 * c0 + 2 * MXU_DIM] = _dup_heads(z).astype(BF16)
        else:
            kb_ref[:, c0:c0 + MXU_DIM] = z.astype(BF16)

    def finish_v(c0, raw):
        if emit_heads:
            _store_head_rows(vh_ref, c0, raw, n_heads)
        if dup:
            vb_ref[:, 2 * c0:2 * c0 + 2 * MXU_DIM] = _dup_heads(raw).astype(BF16)
        else:
            vb_ref[:, c0:c0 + MXU_DIM] = raw.astype(BF16)

    q_chunks = [(finish_q, c0, c0) for c0 in range(0, q_width, MXU_DIM)]
    k_chunks = [(finish_k, c0, q_width + c0) for c0 in range(0, kv_width, MXU_DIM)]
    v_chunks = [(finish_v, c0, q_width + kv_width + c0) for c0 in range(0, kv_width, MXU_DIM)]
    normed = q_chunks + k_chunks
    chunks = []
    for i, chunk in enumerate(normed):
        chunks.append(chunk)
        if i * len(v_chunks) // len(normed) != (i + 1) * len(v_chunks) // len(normed):
            chunks.append(v_chunks[i * len(v_chunks) // len(normed)])

    project = lambda col: _dot(hb, w_ref[:, col:col + MXU_DIM])
    raw = project(chunks[0][2])
    for i, (finish, c0, _) in enumerate(chunks):
        nxt = project(chunks[i + 1][2]) if i + 1 < len(chunks) else None
        finish(c0, raw)
        raw = nxt

def _qkv(x_src, is_ctx, norm_g, mod3, layer, lat_seq, w_bf16, q_gain, k_gain, seg, kv_width, rope_tables):
    n_w = w_bf16.shape[1]
    kvb_width = 2 * kv_width if kv_width == N_KV_HEADS * HEAD_DIM else kv_width
    n_heads = kv_width // HEAD_DIM
    tile_of = (lambda i: i) if is_ctx else (lambda i: i + CTX_TILES)
    tile = lambda w: pl.BlockSpec((TM, w), lambda i: (i, 0))
    const = lambda shape: pl.BlockSpec(shape, lambda i: (0,) * len(shape))
    in_specs = [
        pl.BlockSpec((TM, D_MODEL), lambda i: (x_src[1] + i, 0)),
        const((1, D_MODEL)),
        _mod_spec(layer, 0, lat_seq, tile_of),
        _mod_spec(layer, 1, lat_seq, tile_of),
        const((D_MODEL, n_w)),
        const((1, MXU_DIM)),
        const((1, MXU_DIM)),
        const((MXU_DIM, MXU_DIM)),
    ]
    args = [x_src[0], norm_g.reshape(1, D_MODEL), mod3, mod3, w_bf16,
            jnp.tile(q_gain * Q_SCALE, MXU_DIM // HEAD_DIM).reshape(1, MXU_DIM),
            jnp.tile(k_gain, MXU_DIM // HEAD_DIM).reshape(1, MXU_DIM), seg]
    if rope_tables is not None:
        seq_tiles = lat_seq // TM
        in_specs += [pl.BlockSpec((TM, MXU_DIM), lambda i: (i % seq_tiles, 0))] * 2
        args += list(rope_tables)
    out_specs = [tile(D_MODEL), tile(kvb_width), tile(kvb_width)]
    out_shape = [jax.ShapeDtypeStruct((N_TOK, D_MODEL), BF16),
                 jax.ShapeDtypeStruct((N_TOK, kvb_width), BF16),
                 jax.ShapeDtypeStruct((N_TOK, kvb_width), BF16)]
    if is_ctx:
        out_specs += [pl.BlockSpec((TM * n_heads, HEAD_DIM), lambda i: (i, 0))] * 2
        out_shape += [jax.ShapeDtypeStruct((N_TOK * n_heads, HEAD_DIM), F32)] * 2
    return pl.pallas_call(
        functools.partial(_qkv_kernel, kv_width=kv_width, rope=rope_tables is not None, emit_heads=is_ctx),
        grid=(N_TOK // TM,),
        in_specs=in_specs,
        out_specs=out_specs,
        out_shape=out_shape,
        compiler_params=_params("arbitrary"),
        name="qkv",
    )(*args)


def _two_head_rows(qj):
    lo = lax.broadcasted_iota(I32, qj.shape, 1) < HEAD_DIM
    zero = jnp.zeros_like(qj)
    return jnp.concatenate([jnp.where(lo, qj, zero), jnp.where(lo, zero, qj)], axis=0)


def _merge_two_heads(r):
    tq = r.shape[0] // 2
    lo = lax.broadcasted_iota(I32, (tq, LANES), 1) < HEAD_DIM
    return jnp.where(lo, r[:tq], r[tq:])


def _attn_kernel(*refs, n_kv_blocks, has_ctx):
    if has_ctx:
        q_ref, k_ref, v_ref, ck_ref, cv_ref, o_ref = refs
    else:
        q_ref, k_ref, v_ref, o_ref = refs
    n_q_blocks = D_MODEL // LANES
    key_lanes = lambda j: slice(((j * n_kv_blocks) // n_q_blocks) * LANES,
                                ((j * n_kv_blocks) // n_q_blocks + 1) * LANES)

    items = [(b, j) for b in range(q_ref.shape[0]) for j in range(n_q_blocks)]

    def scores(b, j):
        q2 = _two_head_rows(q_ref[b, :, j * LANES:(j + 1) * LANES])
        s = _dot_nt(q2, k_ref[b, :, key_lanes(j)])
        sc = _dot_nt(q2, ck_ref[b, :, key_lanes(j)]) if has_ctx else None
        return s, sc

    nxt = scores(*items[0])
    for i, (b, j) in enumerate(items):
        (s, sc), ksl = nxt, key_lanes(j)
        if i + 1 < len(items):
            nxt = scores(*items[i + 1])
        m = jnp.max(s, axis=-1, keepdims=True)
        if has_ctx:
            m = jnp.maximum(m, jnp.max(sc, axis=-1, keepdims=True))
        p = jnp.exp2(s - m)
        l = jnp.sum(p, axis=-1, keepdims=True)
        r = _dot(p.astype(BF16), v_ref[b, :, ksl])
        if has_ctx:
            pc = jnp.exp2(sc - m)
            l = l + jnp.sum(pc, axis=-1, keepdims=True)
            r = r + _dot(pc.astype(BF16), cv_ref[b, :, ksl])
        r = r / l
        o_ref[b, :, j * LANES:(j + 1) * LANES] = _merge_two_heads(r).astype(BF16)


def _attention(q, k, v, ctx_k=None, ctx_v=None, tq=256, nb=1):
    n_batch, t, _ = q.shape
    s, w = k.shape[1], k.shape[2]
    has_ctx = ctx_k is not None
    in_specs = [
        pl.BlockSpec((nb, tq, D_MODEL), lambda bi, qi: (bi, qi, 0)),
        pl.BlockSpec((nb, s, w), lambda bi, qi: (bi, 0, 0)),
        pl.BlockSpec((nb, s, w), lambda bi, qi: (bi, 0, 0)),
    ]
    args = [q, k, v]
    if has_ctx:
        sc = ctx_k.shape[1]
        in_specs += [pl.BlockSpec((nb, sc, w), lambda bi, qi: (bi, 0, 0))] * 2
        args += [ctx_k, ctx_v]
    return pl.pallas_call(
        functools.partial(_attn_kernel, n_kv_blocks=w // LANES, has_ctx=has_ctx),
        grid=(n_batch // nb, t // tq),
        in_specs=in_specs,
        out_specs=pl.BlockSpec((nb, tq, D_MODEL), lambda bi, qi: (bi, qi, 0)),
        out_shape=jax.ShapeDtypeStruct((n_batch, t, D_MODEL), BF16),
        compiler_params=_params("arbitrary", "arbitrary"),
        name="attn",
    )(*args)


def _na_kernel(q_ref, k_ref, v_ref, ck_ref, cv_ref, bias_ref, o_ref, *, rows):
    blk = pl.program_id(1)
    q_row0 = NA_QROWS * blk
    band_start = jnp.minimum(jnp.clip(q_row0 - WIN_R // 2, 0, rows - WIN_R), rows - NA_BAND)
    n_band = NA_BAND * GRID_W
    band = pl.ds(pl.multiple_of(band_start * GRID_W, LANES), n_band)
    shape = (NA_QBLK, n_band)
    q_r = q_row0 + lax.broadcasted_iota(I32, shape, 0) // GRID_W
    k_r = band_start + lax.broadcasted_iota(I32, shape, 1) // GRID_W
    q_rs = jnp.clip(q_r - WIN_R // 2, 0, rows - WIN_R)
    row_ok1 = (k_r >= q_rs) & (k_r < q_rs + WIN_R)
    row_ok = jnp.concatenate([row_ok1, row_ok1], axis=0)
    shift0 = (band_start - q_row0 + NA_MAX_SHIFT) // 2
    n_q_blocks = D_MODEL // LANES

    def scores(j):
        sl = slice(j * LANES, (j + 1) * LANES)
        q2 = _two_head_rows(q_ref[0, :, sl])
        return _dot_nt(q2, k_ref[0, band, sl]), _dot_nt(q2, ck_ref[0, :, sl])

    nxt = scores(0)
    for j in range(n_q_blocks):
        sl = slice(j * LANES, (j + 1) * LANES)
        s, sc = nxt
        if j + 1 < n_q_blocks:
            nxt = scores(j + 1)
        bias = jnp.concatenate(
            [jnp.concatenate([bias_ref[2 * j + hh, shift0 + m - u] for m in range(NA_BAND // 2)], axis=1)
             for hh in range(2) for u in range(NA_QROWS // 2)], axis=0)
        s = jnp.where(row_ok, s + bias, NEG_INF)
        m = jnp.maximum(jnp.max(s, axis=-1, keepdims=True), jnp.max(sc, axis=-1, keepdims=True))
        p = jnp.exp2(s - m)
        pc = jnp.exp2(sc - m)
        l = jnp.sum(p, axis=-1, keepdims=True) + jnp.sum(pc, axis=-1, keepdims=True)
        r = _dot(p.astype(BF16), v_ref[0, band, sl]) + _dot(pc.astype(BF16), cv_ref[0, :, sl])
        r = r / l
        o_ref[0, :, sl] = _merge_two_heads(r).astype(BF16)


def _na_bias_kernel(rpb_ref, out_ref, tz_ref):
    h = pl.program_id(0)
    shape = (GRID_W, LANES)
    q_col = lax.broadcasted_iota(I32, shape, 0)
    lane = lax.broadcasted_iota(I32, shape, 1)
    k_col = lane & (GRID_W - 1)
    col_start = jnp.clip(q_col - WIN_C // 2, 0, GRID_W - WIN_C)
    col_ok = (k_col >= col_start) & (k_col < col_start + WIN_C)
    rel_c = jnp.clip(k_col - q_col + WIN_C - 1, 0, N_REL_C - 1)
    for a in range(N_REL_R):
        acc = jnp.zeros(shape, F32)
        for b in range(N_REL_C):
            acc = jnp.where(rel_c == b, rpb_ref[(h * N_REL_R + a) * N_REL_C + b], acc)
        tz_ref[a] = jnp.where(col_ok, acc * LOG2_E, NEG_INF)
    left = lane < GRID_W
    for dd in range(NA_TILES):
        for qr in range(2):
            rel = [min(max(2 * dd - NA_MAX_SHIFT + kr - qr + WIN_R - 1, 0), N_REL_R - 1) for kr in range(2)]
            out_ref[0, dd, qr * GRID_W:(qr + 1) * GRID_W, :] = jnp.where(left, tz_ref[rel[0]], tz_ref[rel[1]])


def _na_bias_table(rpb):
    return pl.pallas_call(
        _na_bias_kernel,
        grid=(N_HEADS,),
        in_specs=[pl.BlockSpec(memory_space=pltpu.SMEM)],
        out_specs=pl.BlockSpec((1, NA_TILES, LANES, LANES), lambda h: (h, 0, 0, 0)),
        out_shape=jax.ShapeDtypeStruct((N_HEADS, NA_TILES, LANES, LANES), F32),
        scratch_shapes=[pltpu.VMEM((N_REL_R, GRID_W, LANES), F32)],
        compiler_params=_params("arbitrary"),
        name="na_bias",
    )(rpb.reshape(-1))


def _na_attention(q, k, v, ctx_k, ctx_v, bias_tbl):
    n_batch, t, _ = q.shape
    sc = ctx_k.shape[1]
    rows = t // GRID_W
    full = lambda n: pl.BlockSpec((1, n, D_MODEL), lambda bi, qi: (bi, 0, 0))
    return pl.pallas_call(
        functools.partial(_na_kernel, rows=rows),
        grid=(n_batch, t // NA_QBLK),
        in_specs=[
            pl.BlockSpec((1, NA_QBLK, D_MODEL), lambda bi, qi: (bi, qi, 0)),
            full(t), full(t), full(sc), full(sc),
            pl.BlockSpec(bias_tbl.shape, lambda bi, qi: (0, 0, 0, 0)),
        ],
        out_specs=pl.BlockSpec((1, NA_QBLK, D_MODEL), lambda bi, qi: (bi, qi, 0)),
        out_shape=jax.ShapeDtypeStruct((n_batch, t, D_MODEL), BF16),
        compiler_params=_params("arbitrary", "arbitrary"),
        name="na_attn",
    )(q, k, v, ctx_k, ctx_v, bias_tbl)


def _post_kernel(oc_ref, ol_ref, xc_ref, xl_ref, wo_ref, gate_ref, g_ref, shift_ref, scale_ref,
                 wrh_ref, wrl_ref, y_ref, h_ref, lg_ref):
    is_ctx = pl.program_id(0) < CTX_TILES
    o = jnp.where(is_ctx, oc_ref[...], ol_ref[...])
    x = jnp.where(is_ctx, xc_ref[...], xl_ref[...])
    half = TM // 2
    proj = [_dot(o[a * half:(a + 1) * half], wo_ref[...]) for a in range(2)]
    gain = g_ref[...] * (1.0 + scale_ref[0])
    for a in range(2):
        rows = slice(a * half, (a + 1) * half)
        y = x[rows] + gate_ref[0] * proj[a]
        y_ref[rows, :] = y
        ms = jnp.mean(y * y, axis=-1, keepdims=True)
        h = y * lax.rsqrt(ms + RMS_EPS) * gain + shift_ref[0]
        for k in range(ROW_TILE):
            h_ref[pl.ds(a * half * ROW_TILE + k, half, stride=ROW_TILE), :] = h[:, k * LANES:(k + 1) * LANES]
        hh, hl = _split_bf16(h)
        lg_ref[:, rows] = _dot_nt(wrh_ref[...], hh) + _dot_nt(wrh_ref[...], hl) + _dot_nt(wrl_ref[...], hh)


def _post_attention(o_ctx, o_lat, x_ctx, x_lat, wo_bf16, norm_g, mod3, layer, lat_seq, wr_hi, wr_lo):
    n_tok = N_STREAMS * N_TOK
    tile = lambda w: pl.BlockSpec((TM, w), lambda i: (i, 0))
    const = lambda shape: pl.BlockSpec(shape, lambda i: (0,) * len(shape))
    return pl.pallas_call(
        _post_kernel,
        grid=(n_tok // TM,),
        in_specs=_x_specs(0, 0) + _x_specs(x_ctx[1], x_lat[1]) + [
            const((D_MODEL, D_MODEL)),
            _mod_spec(layer, 2, lat_seq),
            const((1, D_MODEL)),
            _mod_spec(layer, 3, lat_seq),
            _mod_spec(layer, 4, lat_seq),
            const((N_EXPERTS, D_MODEL)), const((N_EXPERTS, D_MODEL)),
        ],
        out_specs=[tile(D_MODEL), pl.BlockSpec((TM * ROW_TILE, LANES), lambda i: (i, 0)),
                   pl.BlockSpec((N_EXPERTS, TM), lambda i: (0, i))],
        out_shape=[jax.ShapeDtypeStruct((n_tok, D_MODEL), F32),
                   jax.ShapeDtypeStruct((n_tok * ROW_TILE, LANES), F32),
                   jax.ShapeDtypeStruct((N_EXPERTS, n_tok), F32)],
        compiler_params=_params("arbitrary"),
        name="post_attn",
    )(o_ctx, o_lat, x_ctx[0], x_lat[0], wo_bf16, mod3, norm_g.reshape(1, D_MODEL), mod3, mod3, wr_hi, wr_lo)


def _group_prefix(mask, tri):
    local, offs = [], []
    off = jnp.zeros((mask.shape[0], 1), F32)
    for g in range(mask.shape[1] // GROUP):
        xg = mask[:, g * GROUP:(g + 1) * GROUP]
        offs.append(off)
        local.append(_dot(xg.astype(BF16), tri))
        off = off + jnp.sum(xg, axis=1, keepdims=True)
    offs.append(off)
    return local, offs


def _split3_bf16(x):
    hi = x.astype(BF16)
    r1 = x - hi.astype(F32)
    mid = r1.astype(BF16)
    lo = (r1 - mid.astype(F32)).astype(BF16)
    return hi, mid, lo


def _plan_kernel(lg_ref, tri_ref, idx_ref, gc_ref, cnt_ref, affg_ref):
    lg = lg_ref[...]
    ex = jnp.exp(lg - jnp.max(lg, axis=0, keepdims=True))
    aff = ex / jnp.sum(ex, axis=0, keepdims=True)

    def count_ge(v):
        return jnp.sum(jnp.where(aff >= v, 1.0, 0.0), axis=1, keepdims=True)

    def search(i, thr):
        cand = thr | jnp.left_shift(jnp.int32(1), F32_VALUE_BITS - 1 - i)
        ok = (count_ge(lax.bitcast_convert_type(cand, F32)) >= CAP) & (cand >= F32_MIN_NORMAL_BITS)
        return jnp.where(ok, cand, thr)

    thr = lax.fori_loop(0, F32_VALUE_BITS, search, jnp.zeros((N_EXPERTS, 1), I32))
    lo = lax.bitcast_convert_type(thr, F32)
    hi = lax.bitcast_convert_type(jnp.maximum(thr + 1, F32_MIN_NORMAL_BITS), F32)

    def refine(i, bounds):
        lo, hi = bounds
        mid = lo + (hi - lo) * 0.5
        ok = count_ge(mid) >= CAP
        return jnp.where(ok, mid, lo), jnp.where(ok, hi, mid)

    lo, hi = lax.fori_loop(0, REFINE_STEPS, refine, (lo, hi))
    tri = tri_ref[...]
    gt = aff >= hi
    eq = jnp.where((aff >= lo) & (aff < hi), 1.0, 0.0)
    need = CAP - jnp.sum(jnp.where(gt, 1.0, 0.0), axis=1, keepdims=True)
    eq_local, eq_offs = _group_prefix(eq, tri)
    eq_rank = jnp.concatenate([eq_local[g] + eq_offs[g] for g in range(N_GROUPS)], axis=1)
    sel = jnp.where(gt | ((eq > 0.0) & (eq_rank < need)), 1.0, 0.0)
    sel_local, offs = _group_prefix(sel, tri)
    for g in range(N_GROUPS):
        sl = slice(g * GROUP, (g + 1) * GROUP)
        cnt_ref[g] = sel_local[g] + sel[:, sl]
        affg_ref[g] = aff[:, sl]

    lane = lax.broadcasted_iota(I32, (N_EXPERTS, LANES), 1)
    never = jnp.full((N_EXPERTS, LANES), 2.0 * N_TOK, F32)
    grp_lo, grp_hi = never, never
    for g in range(N_GROUPS):
        grp_lo = jnp.where(lane == g, offs[g], grp_lo)
        grp_hi = jnp.where(lane == g, offs[g + 1], grp_hi)
    row = lax.broadcasted_iota(I32, (CAP, LANES), 0).astype(F32)
    in_group_lane = lax.broadcasted_iota(I32, (CAP, GROUP), 1).astype(F32)
    zpad = jnp.zeros((LANES - N_GROUPS, GROUP), BF16)
    for e in range(N_EXPERTS):
        lo_row, hi_row = grp_lo[e:e + 1, :], grp_hi[e:e + 1, :]
        in_grp = (lo_row <= row) & (row < hi_row)
        onehot = jnp.where(in_grp, 1.0, 0.0).astype(BF16)
        counts = _dot(onehot, jnp.concatenate([cnt_ref[:, e, :].astype(BF16), zpad], axis=0))
        rank = row[:, 0:1] - jnp.sum(jnp.where(in_grp, lo_row, 0.0), axis=1, keepdims=True)
        local = jnp.sum(jnp.where(counts <= rank, 1.0, 0.0), axis=1, keepdims=True)
        grp = jnp.sum(jnp.where(hi_row <= row, 1.0, 0.0), axis=1, keepdims=True)
        tok = grp * GROUP + local
        aff_rows = sum(_dot(onehot, jnp.concatenate([part, zpad], axis=0))
                       for part in _split3_bf16(affg_ref[:, e, :]))
        gate = jnp.sum(jnp.where(in_group_lane == local, aff_rows, 0.0), axis=1, keepdims=True)
        gc_ref[0, e] = jnp.broadcast_to(gate, (CAP, LANES))
        tok_b = jnp.broadcast_to(tok, (CAP, LANES))
        tok_row = jnp.concatenate([tok_b[t * LANES:(t + 1) * LANES, :].T[0:1, :]
                                   for t in range(CAP // LANES)], axis=1)
        idx_ref[0, e:e + 1, :] = tok_row.astype(I32)


def _plan(logits_t, tri):
    ns = N_STREAMS
    return pl.pallas_call(
        _plan_kernel,
        grid=(ns,),
        in_specs=[pl.BlockSpec((N_EXPERTS, N_TOK), lambda s: (0, s)),
                  pl.BlockSpec((MXU_DIM, MXU_DIM), lambda s: (0, 0))],
        out_specs=[pl.BlockSpec((1, N_EXPERTS, CAP), lambda s: (s, 0, 0)),
                   pl.BlockSpec((1, N_EXPERTS, CAP, LANES), lambda s: (s, 0, 0, 0))],
        out_shape=[jax.ShapeDtypeStruct((ns, N_EXPERTS, CAP), I32),
                   jax.ShapeDtypeStruct((ns, N_EXPERTS, CAP, LANES), F32)],
        scratch_shapes=[pltpu.VMEM((N_GROUPS, N_EXPERTS, GROUP), F32),
                        pltpu.VMEM((N_GROUPS, N_EXPERTS, GROUP), F32)],
        compiler_params=_params("arbitrary"),
        name="plan",
    )(logits_t, tri)


def _tile_rows(r):
    return pl.ds(pl.multiple_of(r * ROW_TILE, ROW_TILE), ROW_TILE)


def _gather_kernel(idx_ref, h_ref, xe_ref, stage_ref):
    sub = 8
    groups = MOVE_UNROLL // sub

    def move(g, carry):
        for u in range(MOVE_UNROLL):
            tile = h_ref[0, _tile_rows(idx_ref[0, 0, 0, g * MOVE_UNROLL + u]), :]
            stage_ref[g * groups + u // sub, pl.ds(u % sub, ROW_TILE, stride=sub), :] = tile
        return carry

    lax.fori_loop(0, CAP // MOVE_UNROLL, move, 0)
    for k in range(ROW_TILE):
        rows = stage_ref[:, k * sub:(k + 1) * sub, :].reshape(CAP, LANES)
        xe_ref[0, 0, :, k * LANES:(k + 1) * LANES] = rows.astype(BF16)


def _gather(idx, h2_tiles):
    ns = h2_tiles.shape[0]
    return pl.pallas_call(
        _gather_kernel,
        grid=(ns, N_EXPERTS),
        in_specs=[pl.BlockSpec((1, 1, 1, CAP), lambda s, e: (s, e, 0, 0), memory_space=pltpu.SMEM),
                  pl.BlockSpec((1, N_TOK * ROW_TILE, LANES), lambda s, e: (s, 0, 0),
                               pipeline_mode=pl.Buffered(1))],
        out_specs=pl.BlockSpec((1, 1, CAP, D_MODEL), lambda s, e: (s, e, 0, 0)),
        scratch_shapes=[pltpu.VMEM((CAP // 8, ROW_TILE * 8, LANES), F32)],
        out_shape=jax.ShapeDtypeStruct((ns, N_EXPERTS, CAP, D_MODEL), BF16),
        compiler_params=_params("arbitrary", "arbitrary"),
        name="moe_gather",
    )(idx.reshape(ns, N_EXPERTS, 1, CAP), h2_tiles)


def _ffn_kernel(x_ref, wg_ref, wu_ref, wd_ref, gc_ref, ye_ref, acc_ref, *, row_tile):
    f = pl.program_id(1)
    last = pl.num_programs(1) - 1
    ns = x_ref.shape[0]
    tiles = [(s, r0) for s in range(ns) for r0 in range(0, CAP, row_tile)]

    def for_each_tile(emit):
        wg = wg_ref[0, 0].astype(BF16)
        wu = wu_ref[0, 0].astype(BF16)
        wd = wd_ref[0, 0].astype(BF16)

        def gate_up(s, r0):
            x = x_ref[s, 0, r0:r0 + row_tile, :]
            return _dot(x, wg), _dot(x, wu)

        nxt = gate_up(*tiles[0])
        for i, (s, r0) in enumerate(tiles):
            gate, up = nxt
            if i + 1 < len(tiles):
                nxt = gate_up(*tiles[i + 1])
            emit(s, r0, _dot((_silu(gate) * up).astype(BF16), wd))

    def first(s, r0, part):
        acc_ref[s, r0:r0 + row_tile, :] = part

    def middle(s, r0, part):
        acc_ref[s, r0:r0 + row_tile, :] += part

    def final(s, r0, part):
        ye = (acc_ref[s, r0:r0 + row_tile, :] + part) * gc_ref[s, 0, r0:r0 + row_tile, 0:1]
        for k in range(ROW_TILE):
            ye_ref[s, 0, pl.ds(r0 * ROW_TILE + k, row_tile, stride=ROW_TILE), :] = (
                ye[:, k * LANES:(k + 1) * LANES])

    pl.when(f == 0)(lambda: for_each_tile(first))
    pl.when((f > 0) & (f < last))(lambda: for_each_tile(middle))
    pl.when(f == last)(lambda: for_each_tile(final))


def _ffn(xe, gc, w_gate, w_up, w_down, layer):
    ns = xe.shape[0]
    return pl.pallas_call(
        functools.partial(_ffn_kernel, row_tile=512),
        grid=(N_EXPERTS, EXPERT_FF // FF_TILE),
        in_specs=[
            pl.BlockSpec((ns, 1, CAP, D_MODEL), lambda e, f: (0, e, 0, 0)),
            pl.BlockSpec((1, 1, D_MODEL, FF_TILE), lambda e, f: (layer, e, 0, f)),
            pl.BlockSpec((1, 1, D_MODEL, FF_TILE), lambda e, f: (layer, e, 0, f)),
            pl.BlockSpec((1, 1, FF_TILE, D_MODEL), lambda e, f: (layer, e, f, 0)),
            pl.BlockSpec((ns, 1, CAP, LANES), lambda e, f: (0, e, 0, 0)),
        ],
        out_specs=pl.BlockSpec((ns, 1, CAP * ROW_TILE, LANES), lambda e, f: (0, e, 0, 0)),
        out_shape=jax.ShapeDtypeStruct((ns, N_EXPERTS, CAP * ROW_TILE, LANES), F32),
        scratch_shapes=[pltpu.VMEM((ns, CAP, D_MODEL), F32)],
        compiler_params=_params("arbitrary", "arbitrary"),
        name="moe_ffn",
    )(xe, w_gate, w_up, w_down, gc)


def _scatter_kernel(idx_ref, ye_ref, y_ref, gate_ref, *refs, split):
    out_refs, acc_ref = refs[:-1], refs[-1]
    s, j = pl.program_id(0), pl.program_id(1)

    @pl.when(j == 0)
    def _():
        acc_ref[...] = jnp.zeros_like(acc_ref)

    @pl.when(j < N_EXPERTS)
    def _():
        def add_rows(g, carry):
            r0 = g * MOVE_UNROLL
            group = ye_ref.at[0, 0, pl.ds(pl.multiple_of(r0 * ROW_TILE, MOVE_UNROLL * ROW_TILE),
                                          MOVE_UNROLL * ROW_TILE), :]
            dst = [_tile_rows(idx_ref[0, 0, 0, r0 + u]) for u in range(MOVE_UNROLL)]
            val = [acc_ref[dst[u], :] + group[u * ROW_TILE:(u + 1) * ROW_TILE, :] for u in range(MOVE_UNROLL)]
            for u in range(MOVE_UNROLL):
                acc_ref[dst[u], :] = val[u]
            return carry

        lax.fori_loop(0, CAP // MOVE_UNROLL, add_rows, 0)

    @pl.when(j >= N_EXPERTS)
    def _():
        c = j - N_EXPERTS
        rows = FIN_TOK * ROW_TILE
        part = acc_ref.at[pl.ds(pl.multiple_of(c * rows, rows), rows), :]
        moe = jnp.concatenate([part[pl.ds(k, FIN_TOK, stride=ROW_TILE), :] for k in range(ROW_TILE)], axis=1)
        res = y_ref[...] + gate_ref[0] * moe
        if split:
            for si, out_ref in enumerate(out_refs):
                @pl.when(s == si)
                def _(out_ref=out_ref):
                    out_ref[...] = res
        else:
            out_refs[0][...] = res


def _scatter(idx, ye_tiles, y, mod3, layer, lat_seq, split):
    ns = ye_tiles.shape[0]
    fin = lambda j: jnp.maximum(j - N_EXPERTS, 0)

    def gate_index(s, j):
        row = _mod_row(s * N_FIN + fin(j), FIN_TOK, lat_seq)
        return ((layer * N_MOD_ROWS + row) * 6 + 5, 0, 0)

    chunk = (FIN_TOK, D_MODEL)
    if split:
        out_specs = [
            pl.BlockSpec(chunk, lambda s, j, si=si: (
                jnp.where(s == si, fin(j), jnp.where(s < si, 0, N_FIN - 1)), 0))
            for si in range(ns)]
        out_shape = [jax.ShapeDtypeStruct((N_TOK, D_MODEL), F32)] * ns
    else:
        out_specs = pl.BlockSpec(chunk, lambda s, j: (s * N_FIN + fin(j), 0))
        out_shape = jax.ShapeDtypeStruct(y.shape, F32)
    expert = lambda j: jnp.minimum(j, N_EXPERTS - 1)
    return pl.pallas_call(
        functools.partial(_scatter_kernel, split=split),
        grid=(ns, N_EXPERTS + N_FIN),
        in_specs=[
            pl.BlockSpec((1, 1, 1, CAP), lambda s, j: (s, expert(j), 0, 0), memory_space=pltpu.SMEM),
            pl.BlockSpec((1, 1, CAP * ROW_TILE, LANES), lambda s, j: (s, expert(j), 0, 0)),
            pl.BlockSpec(chunk, lambda s, j: (s * N_FIN + fin(j), 0)),
            pl.BlockSpec((1, 1, D_MODEL), gate_index),
        ],
        out_specs=out_specs,
        scratch_shapes=[pltpu.VMEM((N_TOK * ROW_TILE, LANES), F32)],
        out_shape=out_shape,
        compiler_params=_params("arbitrary", "arbitrary"),
        name="moe_scatter",
    )(idx.reshape(ns, N_EXPERTS, 1, CAP), ye_tiles, y, mod3)


def _rope_tables(n_tokens):
    t = np.arange(n_tokens)
    row = (t // GRID_W).astype(np.float32)
    col = (t % GRID_W).astype(np.float32)
    pairs = HEAD_DIM // 4
    inv_freq = ROPE_THETA ** (-jnp.arange(pairs, dtype=F32) / pairs)
    ang = jnp.concatenate([row[:, None] * inv_freq, col[:, None] * inv_freq], axis=-1)
    cos, sin = jnp.cos(ang), jnp.sin(ang)
    reps = MXU_DIM // HEAD_DIM
    return (jnp.tile(jnp.concatenate([cos, cos], axis=-1), (1, reps)),
            jnp.tile(jnp.concatenate([-sin, sin], axis=-1), (1, reps)))


def _dup_cache(cache):
    b, s, hk, hd = cache.shape
    return jnp.broadcast_to(cache[:, :, :, None, :], (b, s, hk, 2, hd)).reshape(b, s, 2 * hk * hd).astype(BF16)


def kernel(x_prompt, x_sample, cache_attn_k, cache_attn_v, cache_na_k, cache_na_v, c, c_ctx,
           norm1_g, norm2_g, w_ada, b_ada, attn_w_qkv, attn_q_gain, attn_k_gain, attn_w_o,
           na_w_qkv, na_q_gain, na_k_gain, na_rpb, na_w_o,
           moe_w_router, moe_w_gate, moe_w_up, moe_w_down):
    bc, tc, _ = x_prompt.shape
    bl, tl, _ = x_sample.shape
    depth = w_ada.shape[0]
    assert bc * tc == N_TOK and bl * tl == N_TOK and 1 + bl <= N_MOD_ROWS
    assert tl % TM == 0 and tl % FIN_TOK == 0
    n_all = N_STREAMS * N_TOK

    cond = jnp.zeros((N_MOD_ROWS, D_MODEL), F32).at[0].set(c_ctx).at[1:1 + bl].set(c)
    mod3 = _ada(cond, w_ada, b_ada).reshape(depth * N_MOD_ROWS * 6, 1, D_MODEL)

    seg = jnp.asarray(np.kron(np.eye(MXU_DIM // HEAD_DIM), np.ones((HEAD_DIM, HEAD_DIM))) / HEAD_DIM, BF16)
    tri = jnp.asarray(np.triu(np.ones((MXU_DIM, MXU_DIM)), k=1), BF16)
    rope_tables = _rope_tables(tl)

    x_ctx = (x_prompt.reshape(N_TOK, D_MODEL), 0)
    x_lat = (x_sample.reshape(N_TOK, D_MODEL), 0)
    new_k, new_v = [], []
    for i in range(depth):
        j = i // 2
        gqa = i % 2 == 0
        if gqa:
            w_qkv, q_gain, k_gain, w_o = attn_w_qkv[j], attn_q_gain[j], attn_k_gain[j], attn_w_o[j]
            kv_heads = N_KV_HEADS
        else:
            w_qkv, q_gain, k_gain, w_o = na_w_qkv[j], na_q_gain[j], na_k_gain[j], na_w_o[j]
            kv_heads = N_HEADS
        wr_hi, wr_lo = _split_bf16(moe_w_router[i].T)

        w_qkv_b = w_qkv.astype(BF16)
        qkv_args = (norm1_g[i], mod3, i, tl, w_qkv_b, q_gain, k_gain, seg, kv_heads * HEAD_DIM)
        q_c, kb_c, vb_c, kf, vf = _qkv(x_ctx, True, *qkv_args, None)
        q_l, kb_l, vb_l = _qkv(x_lat, False, *qkv_args, rope_tables if gqa else None)
        o_ctx = _attention(*(a.reshape(bc, tc, a.shape[1]) for a in (q_c, kb_c, vb_c)), tq=tc, nb=2)
        q_l, kb_l, vb_l = (a.reshape(bl, tl, a.shape[1]) for a in (q_l, kb_l, vb_l))
        if gqa:
            o_lat = _attention(q_l, kb_l, vb_l,
                               _dup_cache(cache_attn_k[:, j]), _dup_cache(cache_attn_v[:, j]), tq=512)
        else:
            past = cache_na_k.shape[2]
            o_lat = _na_attention(q_l, kb_l, vb_l,
                                  cache_na_k[:, j].reshape(bl, past, D_MODEL).astype(BF16),
                                  cache_na_v[:, j].reshape(bl, past, D_MODEL).astype(BF16),
                                  _na_bias_table(na_rpb[j]))
        new_k.append(kf.reshape(bc, 1, tc, kv_heads, HEAD_DIM))
        new_v.append(vf.reshape(bc, 1, tc, kv_heads, HEAD_DIM))

        y, h2, logits_t = _post_attention(o_ctx.reshape(N_TOK, D_MODEL), o_lat.reshape(N_TOK, D_MODEL),
                                          x_ctx, x_lat, w_o.astype(BF16), norm2_g[i], mod3, i, tl,
                                          wr_hi, wr_lo)
        idx, gc = _plan(logits_t, tri)
        xe = _gather(idx, h2.reshape(N_STREAMS, N_TOK * ROW_TILE, LANES))
        ye = _ffn(xe, gc, moe_w_gate, moe_w_up, moe_w_down, i)
        y = _scatter(idx, ye, y, mod3, i, tl, split=(i == depth - 1))
        x_ctx, x_lat = (y, 0), (y, CTX_TILES)

    y_ctx, y_lat = y
    return (y_ctx.reshape(bc, tc, D_MODEL), y_lat.reshape(bl, tl, D_MODEL),
            jnp.concatenate(new_k[0::2], axis=1), jnp.concatenate(new_v[0::2], axis=1),
            jnp.concatenate(new_k[1::2], axis=1), jnp.concatenate(new_v[1::2], axis=1))
```

```python
import functools

import jax
import jax.numpy as jnp
import numpy as np
from jax import lax
from jax.experimental import pallas as pl
from jax.experimental.pallas import tpu as pltpu

F32 = jnp.float32
BF16 = jnp.bfloat16
I32 = jnp.int32

D_MODEL = 1024
N_HEADS = 16
N_KV_HEADS = 4
HEAD_DIM = 64
GRID_W = 64
WIN_R = 8
WIN_C = 16
N_EXPERTS = 16
EXPERT_FF = 2048
ROPE_THETA = 10000.0
RMS_EPS = 1e-6
NEG_INF = -1e30
F32_MIN_NORMAL_BITS = 0x00800000
F32_VALUE_BITS = 31
REFINE_STEPS = 32
LOG2_E = 1.4426950408889634
Q_SCALE = HEAD_DIM ** -0.5 * LOG2_E

LANES = 128
SUBLANES = 8
MXU_DIM = 256
VMEM_LIMIT = 56 * 1024 * 1024

N_STREAMS = 2
N_TOK = 8192
CAP = 2 * N_TOK // N_EXPERTS
TM = 512
N_MOD_ROWS = 16
ROW_TILE = D_MODEL // LANES
GROUP = MXU_DIM
N_GROUPS = N_TOK // GROUP
FIN_TOK = 512
N_FIN = N_TOK // FIN_TOK
GATHER_UNROLL = 32
SCATTER_UNROLL = 16
FF_TILE = 512
NA_QBLK = 256
NA_QROWS = NA_QBLK // GRID_W
NA_BAND = WIN_R + NA_QROWS
NA_MAX_SHIFT = WIN_R - 2 + NA_QROWS
NA_TILES = NA_MAX_SHIFT + 1
N_REL_R = 2 * WIN_R - 1
N_REL_C = 2 * WIN_C - 1


def _params(*sem):
    return pltpu.CompilerParams(dimension_semantics=sem, vmem_limit_bytes=VMEM_LIMIT)


def _dot(a, b):
    return jnp.dot(a, b, preferred_element_type=F32)


def _dot_nt(a, b):
    return lax.dot_general(a, b, (((1,), (1,)), ((), ())), preferred_element_type=F32)


def _split_bf16(x):
    hi = x.astype(BF16)
    lo = (x - hi.astype(F32)).astype(BF16)
    return hi, lo


def _silu(x):
    return x * (1.0 / (1.0 + jnp.exp(-x)))


def _ada_kernel(cond_ref, w_ref, b_ref, out_ref):
    sx = _silu(cond_ref[...])
    xh, xl = _split_bf16(sx)
    wh, wl = _split_bf16(w_ref[0])
    out_ref[0] = _dot(xh, wh) + _dot(xl, wh) + _dot(xh, wl) + b_ref[0]


def _ada(cond, w_ada, b_ada):
    depth = w_ada.shape[0]
    tn = 1024
    n_out = w_ada.shape[2]
    return pl.pallas_call(
        _ada_kernel,
        grid=(depth, n_out // tn),
        in_specs=[
            pl.BlockSpec((N_MOD_ROWS, D_MODEL), lambda l, n: (0, 0)),
            pl.BlockSpec((1, D_MODEL, tn), lambda l, n: (l, 0, n)),
            pl.BlockSpec((1, 1, tn), lambda l, n: (l, 0, n)),
        ],
        out_specs=pl.BlockSpec((1, N_MOD_ROWS, tn), lambda l, n: (l, 0, n)),
        out_shape=jax.ShapeDtypeStruct((depth, N_MOD_ROWS, n_out), F32),
        compiler_params=_params("arbitrary", "arbitrary"),
        name="ada",
    )(cond, w_ada, b_ada.reshape(depth, 1, n_out))


def _mod_row(tile, tile_rows, lat_seq):
    ctx_tiles = N_TOK // tile_rows
    return jnp.where(tile < ctx_tiles, 0, 1 + (tile - ctx_tiles) // (lat_seq // tile_rows))


def _mod_spec(layer, which, lat_seq, tile_of=lambda i: i):
    def index(i):
        return ((layer * N_MOD_ROWS + _mod_row(tile_of(i), TM, lat_seq)) * 6 + which, 0, 0)
    return pl.BlockSpec((1, 1, D_MODEL), index)


CTX_TILES = N_TOK // TM


def _x_specs(x_ctx_block0, x_lat_block0, tile_of=lambda i: i):
    return [
        pl.BlockSpec((TM, D_MODEL), lambda i: (x_ctx_block0 + jnp.minimum(tile_of(i), CTX_TILES - 1), 0)),
        pl.BlockSpec((TM, D_MODEL), lambda i: (x_lat_block0 + jnp.maximum(tile_of(i) - CTX_TILES, 0), 0)),
    ]


def _head_norm(z, seg, gain):
    ms = _dot((z * z).astype(BF16), seg)
    return z * lax.rsqrt(ms + RMS_EPS) * gain


def _rope(z, cos_t, sin_t):
    lane = lax.broadcasted_iota(I32, z.shape, 1)
    first = (lane & 32) == 0
    n = z.shape[1]
    partner = jnp.where(first, pltpu.roll(z, n - 32, axis=1), pltpu.roll(z, 32, axis=1))
    return z * cos_t + partner * sin_t


def _dup_heads(z):
    outs = []
    for b in range(z.shape[1] // LANES):
        x = z[:, b * LANES:(b + 1) * LANES]
        xr = pltpu.roll(x, HEAD_DIM, axis=1)
        lo = lax.broadcasted_iota(I32, x.shape, 1) < HEAD_DIM
        outs.append(jnp.where(lo, x, xr))
        outs.append(jnp.where(lo, xr, x))
    return jnp.concatenate(outs, axis=1)


def _store_head_rows(dst_ref, c0, z, n_heads):
    for b in range(z.shape[1] // LANES):
        pair = z[:, b * LANES:(b + 1) * LANES]
        swapped = pltpu.roll(pair, HEAD_DIM, axis=1)
        for hh, val in enumerate((pair, swapped)):
            head = c0 // HEAD_DIM + 2 * b + hh
            dst_ref[pl.ds(head, TM, stride=n_heads), :] = val[:, :HEAD_DIM]


def _qkv_kernel(*refs, kv_width, rope, emit_heads):
    it = iter(refs)
    x_ref, g_ref, shift_ref, scale_ref, w_ref, qg_ref, kg_ref, seg_ref = (next(it) for _ in range(8))
    cos_ref = sin_ref = None
    if rope:
        cos_ref, sin_ref = next(it), next(it)
    q_ref, kb_ref, vb_ref = (next(it) for _ in range(3))
    kh_ref = vh_ref = None
    if emit_heads:
        kh_ref, vh_ref = next(it), next(it)
    dup = kv_width == N_KV_HEADS * HEAD_DIM
    n_heads = kv_width // HEAD_DIM

    x = x_ref[...]
    ms = jnp.mean(x * x, axis=-1, keepdims=True)
    h = x * lax.rsqrt(ms + RMS_EPS) * (g_ref[...] * (1.0 + scale_ref[0])) + shift_ref[0]
    hb = h.astype(BF16)
    seg = seg_ref[...]
    if rope:
        cos_t, sin_t = cos_ref[...], sin_ref[...]

    q_width = N_HEADS * HEAD_DIM

    def finish_q(c0, raw):
        z = _head_norm(raw, seg, qg_ref[...])
        if rope:
            z = _rope(z, cos_t, sin_t)
        q_ref[:, c0:c0 + MXU_DIM] = z.astype(BF16)

    def finish_k(c0, raw):
        z = _head_norm(raw, seg, kg_ref[...])
        if emit_heads:
            _store_head_rows(kh_ref, c0, z, n_heads)
        if rope:
            z = _rope(z, cos_t, sin_t)
        if dup:
            kb_ref[:, 2 * c0:2 * c0 + 2 * MXU_DIM] = _dup_heads(z).astype(BF16)
        else:
            kb_ref[:, c0:c0 + MXU_DIM] = z.astype(BF16)

    def finish_v(c0, raw):
        if emit_heads:
            _store_head_rows(vh_ref, c0, raw, n_heads)
        if dup:
            vb_ref[:, 2 * c0:2 * c0 + 2 * MXU_DIM] = _dup_heads(raw).astype(BF16)
        else:
            vb_ref[:, c0:c0 + MXU_DIM] = raw.astype(BF16)

    q_chunks = [(finish_q, c0, c0) for c0 in range(0, q_width, MXU_DIM)]
    k_chunks = [(finish_k, c0, q_width + c0) for c0 in range(0, kv_width, MXU_DIM)]
    v_chunks = [(finish_v, c0, q_width + kv_width + c0) for c0 in range(0, kv_width, MXU_DIM)]
    normed = q_chunks + k_chunks
    chunks = []
    for i, chunk in enumerate(normed):
        chunks.append(chunk)
        if i * len(v_chunks) // len(normed) != (i + 1) * len(v_chunks) // len(normed):
            chunks.append(v_chunks[i * len(v_chunks) // len(normed)])

    project = lambda col: _dot(hb, w_ref[:, col:col + MXU_DIM])
    raw = project(chunks[0][2])
    for i, (finish, c0, _) in enumerate(chunks):
        nxt = project(chunks[i + 1][2]) if i + 1 < len(chunks) else None
        finish(c0, raw)
        raw = nxt

def _qkv(x_src, is_ctx, norm_g, mod3, layer, lat_seq, w_bf16, q_gain, k_gain, seg, kv_width, rope_tables):
    n_w = w_bf16.shape[1]
    kvb_width = 2 * kv_width if kv_width == N_KV_HEADS * HEAD_DIM else kv_width
    n_heads = kv_width // HEAD_DIM
    tile_of = (lambda i: i) if is_ctx else (lambda i: i + CTX_TILES)
    tile = lambda w: pl.BlockSpec((TM, w), lambda i: (i, 0))
    const = lambda shape: pl.BlockSpec(shape, lambda i: (0,) * len(shape))
    in_specs = [
        pl.BlockSpec((TM, D_MODEL), lambda i: (x_src[1] + i, 0)),
        const((1, D_MODEL)),
        _mod_spec(layer, 0, lat_seq, tile_of),
        _mod_spec(layer, 1, lat_seq, tile_of),
        const((D_MODEL, n_w)),
        const((1, MXU_DIM)),
        const((1, MXU_DIM)),
        const((MXU_DIM, MXU_DIM)),
    ]
    args = [x_src[0], norm_g.reshape(1, D_MODEL), mod3, mod3, w_bf16,
            jnp.tile(q_gain * Q_SCALE, MXU_DIM // HEAD_DIM).reshape(1, MXU_DIM),
            jnp.tile(k_gain, MXU_DIM // HEAD_DIM).reshape(1, MXU_DIM), seg]
    if rope_tables is not None:
        seq_tiles = lat_seq // TM
        in_specs += [pl.BlockSpec((TM, MXU_DIM), lambda i: (i % seq_tiles, 0))] * 2
        args += list(rope_tables)
    out_specs = [tile(D_MODEL), tile(kvb_width), tile(kvb_width)]
    out_shape = [jax.ShapeDtypeStruct((N_TOK, D_MODEL), BF16),
                 jax.ShapeDtypeStruct((N_TOK, kvb_width), BF16),
                 jax.ShapeDtypeStruct((N_TOK, kvb_width), BF16)]
    if is_ctx:
        out_specs += [pl.BlockSpec((TM * n_heads, HEAD_DIM), lambda i: (i, 0))] * 2
        out_shape += [jax.ShapeDtypeStruct((N_TOK * n_heads, HEAD_DIM), F32)] * 2
    return pl.pallas_call(
        functools.partial(_qkv_kernel, kv_width=kv_width, rope=rope_tables is not None, emit_heads=is_ctx),
        grid=(N_TOK // TM,),
        in_specs=in_specs,
        out_specs=out_specs,
        out_shape=out_shape,
        compiler_params=_params("arbitrary"),
        name="qkv",
    )(*args)


def _two_head_rows(qj):
    lo = lax.broadcasted_iota(I32, qj.shape, 1) < HEAD_DIM
    zero = jnp.zeros_like(qj)
    return jnp.concatenate([jnp.where(lo, qj, zero), jnp.where(lo, zero, qj)], axis=0)


def _merge_two_heads(r):
    tq = r.shape[0] // 2
    lo = lax.broadcasted_iota(I32, (tq, LANES), 1) < HEAD_DIM
    return jnp.where(lo, r[:tq], r[tq:])


def _attn_kernel(*refs, n_kv_blocks, has_ctx):
    if has_ctx:
        q_ref, k_ref, v_ref, ck_ref, cv_ref, o_ref = refs
    else:
        q_ref, k_ref, v_ref, o_ref = refs
    n_q_blocks = D_MODEL // LANES
    key_lanes = lambda j: slice(((j * n_kv_blocks) // n_q_blocks) * LANES,
                                ((j * n_kv_blocks) // n_q_blocks + 1) * LANES)

    items = [(b, j) for b in range(q_ref.shape[0]) for j in range(n_q_blocks)]

    def scores(b, j):
        q2 = _two_head_rows(q_ref[b, :, j * LANES:(j + 1) * LANES])
        s = _dot_nt(q2, k_ref[b, :, key_lanes(j)])
        sc = _dot_nt(q2, ck_ref[b, :, key_lanes(j)]) if has_ctx else None
        return s, sc

    nxt = scores(*items[0])
    for i, (b, j) in enumerate(items):
        (s, sc), ksl = nxt, key_lanes(j)
        if i + 1 < len(items):
            nxt = scores(*items[i + 1])
        m = jnp.max(s, axis=-1, keepdims=True)
        if has_ctx:
            m = jnp.maximum(m, jnp.max(sc, axis=-1, keepdims=True))
        p = jnp.exp2(s - m)
        l = jnp.sum(p, axis=-1, keepdims=True)
        r = _dot(p.astype(BF16), v_ref[b, :, ksl])
        if has_ctx:
            pc = jnp.exp2(sc - m)
            l = l + jnp.sum(pc, axis=-1, keepdims=True)
            r = r + _dot(pc.astype(BF16), cv_ref[b, :, ksl])
        r = r / l
        o_ref[b, :, j * LANES:(j + 1) * LANES] = _merge_two_heads(r).astype(BF16)


def _attention(q, k, v, ctx_k=None, ctx_v=None, tq=256, nb=1):
    n_batch, t, _ = q.shape
    s, w = k.shape[1], k.shape[2]
    has_ctx = ctx_k is not None
    in_specs = [
        pl.BlockSpec((nb, tq, D_MODEL), lambda bi, qi: (bi, qi, 0)),
        pl.BlockSpec((nb, s, w), lambda bi, qi: (bi, 0, 0)),
        pl.BlockSpec((nb, s, w), lambda bi, qi: (bi, 0, 0)),
    ]
    args = [q, k, v]
    if has_ctx:
        sc = ctx_k.shape[1]
        in_specs += [pl.BlockSpec((nb, sc, w), lambda bi, qi: (bi, 0, 0))] * 2
        args += [ctx_k, ctx_v]
    return pl.pallas_call(
        functools.partial(_attn_kernel, n_kv_blocks=w // LANES, has_ctx=has_ctx),
        grid=(n_batch // nb, t // tq),
        in_specs=in_specs,
        out_specs=pl.BlockSpec((nb, tq, D_MODEL), lambda bi, qi: (bi, qi, 0)),
        out_shape=jax.ShapeDtypeStruct((n_batch, t, D_MODEL), BF16),
        compiler_params=_params("arbitrary", "arbitrary"),
        name="attn",
    )(*args)


def _na_kernel(q_ref, k_ref, v_ref, ck_ref, cv_ref, bias_ref, o_ref, *, rows):
    blk = pl.program_id(1)
    q_row0 = NA_QROWS * blk
    band_start = jnp.minimum(jnp.clip(q_row0 - WIN_R // 2, 0, rows - WIN_R), rows - NA_BAND)
    n_band = NA_BAND * GRID_W
    band = pl.ds(pl.multiple_of(band_start * GRID_W, LANES), n_band)
    shape = (NA_QBLK, n_band)
    q_r = q_row0 + lax.broadcasted_iota(I32, shape, 0) // GRID_W
    k_r = band_start + lax.broadcasted_iota(I32, shape, 1) // GRID_W
    q_rs = jnp.clip(q_r - WIN_R // 2, 0, rows - WIN_R)
    row_ok1 = (k_r >= q_rs) & (k_r < q_rs + WIN_R)
    row_ok = jnp.concatenate([row_ok1, row_ok1], axis=0)
    shift0 = (band_start - q_row0 + NA_MAX_SHIFT) // 2
    n_q_blocks = D_MODEL // LANES

    def scores(j):
        sl = slice(j * LANES, (j + 1) * LANES)
        q2 = _two_head_rows(q_ref[0, :, sl])
        return _dot_nt(q2, k_ref[0, band, sl]), _dot_nt(q2, ck_ref[0, :, sl])

    nxt = scores(0)
    for j in range(n_q_blocks):
        sl = slice(j * LANES, (j + 1) * LANES)
        s, sc = nxt
        if j + 1 < n_q_blocks:
            nxt = scores(j + 1)
        bias = jnp.concatenate(
            [jnp.concatenate([bias_ref[2 * j + hh, shift0 + m - u] for m in range(NA_BAND // 2)], axis=1)
             for hh in range(2) for u in range(NA_QROWS // 2)], axis=0)
        s = jnp.where(row_ok, s + bias, NEG_INF)
        m = jnp.maximum(jnp.max(s, axis=-1, keepdims=True), jnp.max(sc, axis=-1, keepdims=True))
        p = jnp.exp2(s - m)
        pc = jnp.exp2(sc - m)
        l = jnp.sum(p, axis=-1, keepdims=True) + jnp.sum(pc, axis=-1, keepdims=True)
        r = _dot(p.astype(BF16), v_ref[0, band, sl]) + _dot(pc.astype(BF16), cv_ref[0, :, sl])
        r = r / l
        o_ref[0, :, sl] = _merge_two_heads(r).astype(BF16)


def _na_bias_kernel(rpb_ref, out_ref, tz_ref):
    h = pl.program_id(0)
    shape = (GRID_W, LANES)
    q_col = lax.broadcasted_iota(I32, shape, 0)
    lane = lax.broadcasted_iota(I32, shape, 1)
    k_col = lane & (GRID_W - 1)
    col_start = jnp.clip(q_col - WIN_C // 2, 0, GRID_W - WIN_C)
    col_ok = (k_col >= col_start) & (k_col < col_start + WIN_C)
    rel_c = jnp.clip(k_col - q_col + WIN_C - 1, 0, N_REL_C - 1)
    for a in range(N_REL_R):
        acc = jnp.zeros(shape, F32)
        for b in range(N_REL_C):
            acc = jnp.where(rel_c == b, rpb_ref[(h * N_REL_R + a) * N_REL_C + b], acc)
        tz_ref[a] = jnp.where(col_ok, acc * LOG2_E, NEG_INF)
    left = lane < GRID_W
    for dd in range(NA_TILES):
        for qr in range(2):
            rel = [min(max(2 * dd - NA_MAX_SHIFT + kr - qr + WIN_R - 1, 0), N_REL_R - 1) for kr in range(2)]
            out_ref[0, dd, qr * GRID_W:(qr + 1) * GRID_W, :] = jnp.where(left, tz_ref[rel[0]], tz_ref[rel[1]])


def _na_bias_table(rpb):
    return pl.pallas_call(
        _na_bias_kernel,
        grid=(N_HEADS,),
        in_specs=[pl.BlockSpec(memory_space=pltpu.SMEM)],
        out_specs=pl.BlockSpec((1, NA_TILES, LANES, LANES), lambda h: (h, 0, 0, 0)),
        out_shape=jax.ShapeDtypeStruct((N_HEADS, NA_TILES, LANES, LANES), F32),
        scratch_shapes=[pltpu.VMEM((N_REL_R, GRID_W, LANES), F32)],
        compiler_params=_params("arbitrary"),
        name="na_bias",
    )(rpb.reshape(-1))


def _na_attention(q, k, v, ctx_k, ctx_v, bias_tbl):
    n_batch, t, _ = q.shape
    sc = ctx_k.shape[1]
    rows = t // GRID_W
    full = lambda n: pl.BlockSpec((1, n, D_MODEL), lambda bi, qi: (bi, 0, 0))
    return pl.pallas_call(
        functools.partial(_na_kernel, rows=rows),
        grid=(n_batch, t // NA_QBLK),
        in_specs=[
            pl.BlockSpec((1, NA_QBLK, D_MODEL), lambda bi, qi: (bi, qi, 0)),
            full(t), full(t), full(sc), full(sc),
            pl.BlockSpec(bias_tbl.shape, lambda bi, qi: (0, 0, 0, 0)),
        ],
        out_specs=pl.BlockSpec((1, NA_QBLK, D_MODEL), lambda bi, qi: (bi, qi, 0)),
        out_shape=jax.ShapeDtypeStruct((n_batch, t, D_MODEL), BF16),
        compiler_params=_params("arbitrary", "arbitrary"),
        name="na_attn",
    )(q, k, v, ctx_k, ctx_v, bias_tbl)


def _post_kernel(oc_ref, ol_ref, xc_ref, xl_ref, wo_ref, gate_ref, g_ref, shift_ref, scale_ref,
                 wrh_ref, wrl_ref, y_ref, h_ref, lg_ref):
    is_ctx = pl.program_id(0) < CTX_TILES
    o = jnp.where(is_ctx, oc_ref[...], ol_ref[...])
    x = jnp.where(is_ctx, xc_ref[...], xl_ref[...])
    half = TM // 2
    proj = [_dot(o[a * half:(a + 1) * half], wo_ref[...]) for a in range(2)]
    gain = g_ref[...] * (1.0 + scale_ref[0])
    for a in range(2):
        rows = slice(a * half, (a + 1) * half)
        y = x[rows] + gate_ref[0] * proj[a]
        y_ref[rows, :] = y
        ms = jnp.mean(y * y, axis=-1, keepdims=True)
        h = y * lax.rsqrt(ms + RMS_EPS) * gain + shift_ref[0]
        for k in range(ROW_TILE):
            h_ref[pl.ds(a * half * ROW_TILE + k, half, stride=ROW_TILE), :] = h[:, k * LANES:(k + 1) * LANES]
        hh, hl = _split_bf16(h)
        lg_ref[:, rows] = _dot_nt(wrh_ref[...], hh) + _dot_nt(wrh_ref[...], hl) + _dot_nt(wrl_ref[...], hh)


def _post_attention(o_ctx, o_lat, x_ctx, x_lat, wo_bf16, norm_g, mod3, layer, lat_seq, wr_hi, wr_lo):
    n_tok = N_STREAMS * N_TOK
    tile = lambda w: pl.BlockSpec((TM, w), lambda i: (i, 0))
    const = lambda shape: pl.BlockSpec(shape, lambda i: (0,) * len(shape))
    return pl.pallas_call(
        _post_kernel,
        grid=(n_tok // TM,),
        in_specs=_x_specs(0, 0) + _x_specs(x_ctx[1], x_lat[1]) + [
            const((D_MODEL, D_MODEL)),
            _mod_spec(layer, 2, lat_seq),
            const((1, D_MODEL)),
            _mod_spec(layer, 3, lat_seq),
            _mod_spec(layer, 4, lat_seq),
            const((N_EXPERTS, D_MODEL)), const((N_EXPERTS, D_MODEL)),
        ],
        out_specs=[tile(D_MODEL), pl.BlockSpec((TM * ROW_TILE, LANES), lambda i: (i, 0)),
                   pl.BlockSpec((N_EXPERTS, TM), lambda i: (0, i))],
        out_shape=[jax.ShapeDtypeStruct((n_tok, D_MODEL), F32),
                   jax.ShapeDtypeStruct((n_tok * ROW_TILE, LANES), F32),
                   jax.ShapeDtypeStruct((N_EXPERTS, n_tok), F32)],
        compiler_params=_params("arbitrary"),
        name="post_attn",
    )(o_ctx, o_lat, x_ctx[0], x_lat[0], wo_bf16, mod3, norm_g.reshape(1, D_MODEL), mod3, mod3, wr_hi, wr_lo)


def _group_prefix(mask, tri):
    local, offs = [], []
    off = jnp.zeros((mask.shape[0], 1), F32)
    for g in range(mask.shape[1] // GROUP):
        xg = mask[:, g * GROUP:(g + 1) * GROUP]
        offs.append(off)
        local.append(_dot(xg.astype(BF16), tri))
        off = off + jnp.sum(xg, axis=1, keepdims=True)
    offs.append(off)
    return local, offs


def _split3_bf16(x):
    hi = x.astype(BF16)
    r1 = x - hi.astype(F32)
    mid = r1.astype(BF16)
    lo = (r1 - mid.astype(F32)).astype(BF16)
    return hi, mid, lo


def _plan_kernel(lg_ref, tri_ref, idx_ref, gc_ref, cnt_ref, affg_ref):
    lg = lg_ref[...]
    ex = jnp.exp(lg - jnp.max(lg, axis=0, keepdims=True))
    aff = ex / jnp.sum(ex, axis=0, keepdims=True)

    def count_ge(v):
        return jnp.sum(jnp.where(aff >= v, 1.0, 0.0), axis=1, keepdims=True)

    def search(i, thr):
        cand = thr | jnp.left_shift(jnp.int32(1), F32_VALUE_BITS - 1 - i)
        ok = (count_ge(lax.bitcast_convert_type(cand, F32)) >= CAP) & (cand >= F32_MIN_NORMAL_BITS)
        return jnp.where(ok, cand, thr)

    thr = lax.fori_loop(0, F32_VALUE_BITS, search, jnp.zeros((N_EXPERTS, 1), I32))
    lo = lax.bitcast_convert_type(thr, F32)
    hi = lax.bitcast_convert_type(jnp.maximum(thr + 1, F32_MIN_NORMAL_BITS), F32)

    def refine(i, bounds):
        lo, hi = bounds
        mid = lo + (hi - lo) * 0.5
        ok = count_ge(mid) >= CAP
        return jnp.where(ok, mid, lo), jnp.where(ok, hi, mid)

    lo, hi = lax.fori_loop(0, REFINE_STEPS, refine, (lo, hi))
    tri = tri_ref[...]
    gt = aff >= hi
    eq = jnp.where((aff >= lo) & (aff < hi), 1.0, 0.0)
    need = CAP - jnp.sum(jnp.where(gt, 1.0, 0.0), axis=1, keepdims=True)
    eq_local, eq_offs = _group_prefix(eq, tri)
    eq_rank = jnp.concatenate([eq_local[g] + eq_offs[g] for g in range(N_GROUPS)], axis=1)
    sel = jnp.where(gt | ((eq > 0.0) & (eq_rank < need)), 1.0, 0.0)
    sel_local, offs = _group_prefix(sel, tri)
    for g in range(N_GROUPS):
        sl = slice(g * GROUP, (g + 1) * GROUP)
        cnt_ref[g] = sel_local[g] + sel[:, sl]
        affg_ref[g] = aff[:, sl]

    lane = lax.broadcasted_iota(I32, (N_EXPERTS, LANES), 1)
    never = jnp.full((N_EXPERTS, LANES), 2.0 * N_TOK, F32)
    grp_lo, grp_hi = never, never
    for g in range(N_GROUPS):
        grp_lo = jnp.where(lane == g, offs[g], grp_lo)
        grp_hi = jnp.where(lane == g, offs[g + 1], grp_hi)
    row = lax.broadcasted_iota(I32, (CAP, LANES), 0).astype(F32)
    in_group_lane = lax.broadcasted_iota(I32, (CAP, GROUP), 1).astype(F32)
    zpad = jnp.zeros((LANES - N_GROUPS, GROUP), BF16)
    for e in range(N_EXPERTS):
        lo_row, hi_row = grp_lo[e:e + 1, :], grp_hi[e:e + 1, :]
        in_grp = (lo_row <= row) & (row < hi_row)
        onehot = jnp.where(in_grp, 1.0, 0.0).astype(BF16)
        counts = _dot(onehot, jnp.concatenate([cnt_ref[:, e, :].astype(BF16), zpad], axis=0))
        rank = row[:, 0:1] - jnp.sum(jnp.where(in_grp, lo_row, 0.0), axis=1, keepdims=True)
        local = jnp.sum(jnp.where(counts <= rank, 1.0, 0.0), axis=1, keepdims=True)
        grp = jnp.sum(jnp.where(hi_row <= row, 1.0, 0.0), axis=1, keepdims=True)
        tok = grp * GROUP + local
        aff_rows = sum(_dot(onehot, jnp.concatenate([part, zpad], axis=0))
                       for part in _split3_bf16(affg_ref[:, e, :]))
        gate = jnp.sum(jnp.where(in_group_lane == local, aff_rows, 0.0), axis=1, keepdims=True)
        gc_ref[0, e] = jnp.broadcast_to(gate, (CAP, LANES))
        tok_b = jnp.broadcast_to(tok, (CAP, LANES))
        tok_row = jnp.concatenate([tok_b[t * LANES:(t + 1) * LANES, :].T[0:1, :]
                                   for t in range(CAP // LANES)], axis=1)
        idx_ref[0, e:e + 1, :] = tok_row.astype(I32)


def _plan(logits_t, tri):
    ns = N_STREAMS
    return pl.pallas_call(
        _plan_kernel,
        grid=(ns,),
        in_specs=[pl.BlockSpec((N_EXPERTS, N_TOK), lambda s: (0, s)),
                  pl.BlockSpec((MXU_DIM, MXU_DIM), lambda s: (0, 0))],
        out_specs=[pl.BlockSpec((1, N_EXPERTS, CAP), lambda s: (s, 0, 0)),
                   pl.BlockSpec((1, N_EXPERTS, CAP, LANES), lambda s: (s, 0, 0, 0))],
        out_shape=[jax.ShapeDtypeStruct((ns, N_EXPERTS, CAP), I32),
                   jax.ShapeDtypeStruct((ns, N_EXPERTS, CAP, LANES), F32)],
        scratch_shapes=[pltpu.VMEM((N_GROUPS, N_EXPERTS, GROUP), F32),
                        pltpu.VMEM((N_GROUPS, N_EXPERTS, GROUP), F32)],
        compiler_params=_params("arbitrary"),
        name="plan",
    )(logits_t, tri)


def _tile_rows(r):
    return pl.ds(pl.multiple_of(r * ROW_TILE, ROW_TILE), ROW_TILE)


def _gather_kernel(idx_ref, h_ref, xe_ref, stage_ref):
    sub = SUBLANES
    groups = GATHER_UNROLL // sub

    def move(g, carry):
        for u in range(GATHER_UNROLL):
            tile = h_ref[0, _tile_rows(idx_ref[0, 0, 0, g * GATHER_UNROLL + u]), :]
            stage_ref[g * groups + u // sub, pl.ds(u % sub, ROW_TILE, stride=sub), :] = tile
        return carry

    lax.fori_loop(0, CAP // GATHER_UNROLL, move, 0)
    for k in range(ROW_TILE):
        rows = stage_ref[:, k * sub:(k + 1) * sub, :].reshape(CAP, LANES)
        xe_ref[0, 0, :, k * LANES:(k + 1) * LANES] = rows.astype(BF16)


def _gather(idx, h2_tiles):
    ns = h2_tiles.shape[0]
    return pl.pallas_call(
        _gather_kernel,
        grid=(ns, N_EXPERTS),
        in_specs=[pl.BlockSpec((1, 1, 1, CAP), lambda s, e: (s, e, 0, 0), memory_space=pltpu.SMEM),
                  pl.BlockSpec((1, N_TOK * ROW_TILE, LANES), lambda s, e: (s, 0, 0),
                               pipeline_mode=pl.Buffered(1))],
        out_specs=pl.BlockSpec((1, 1, CAP, D_MODEL), lambda s, e: (s, e, 0, 0)),
        scratch_shapes=[pltpu.VMEM((CAP // SUBLANES, ROW_TILE * SUBLANES, LANES), F32)],
        out_shape=jax.ShapeDtypeStruct((ns, N_EXPERTS, CAP, D_MODEL), BF16),
        compiler_params=_params("arbitrary", "arbitrary"),
        name="moe_gather",
    )(idx.reshape(ns, N_EXPERTS, 1, CAP), h2_tiles)


def _ffn_kernel(x_ref, wg_ref, wu_ref, wd_ref, gc_ref, ye_ref, acc_ref, *, row_tile):
    f = pl.program_id(1)
    last = pl.num_programs(1) - 1
    ns = x_ref.shape[0]
    tiles = [(s, r0) for s in range(ns) for r0 in range(0, CAP, row_tile)]

    def for_each_tile(emit):
        wg = wg_ref[0, 0].astype(BF16)
        wu = wu_ref[0, 0].astype(BF16)
        wd = wd_ref[0, 0].astype(BF16)

        def gate_up(s, r0):
            x = x_ref[s, 0, r0:r0 + row_tile, :]
            return _dot(x, wg), _dot(x, wu)

        nxt = gate_up(*tiles[0])
        for i, (s, r0) in enumerate(tiles):
            gate, up = nxt
            if i + 1 < len(tiles):
                nxt = gate_up(*tiles[i + 1])
            emit(s, r0, _dot((_silu(gate) * up).astype(BF16), wd))

    def first(s, r0, part):
        acc_ref[s, r0:r0 + row_tile, :] = part

    def middle(s, r0, part):
        acc_ref[s, r0:r0 + row_tile, :] += part

    def final(s, r0, part):
        ye = (acc_ref[s, r0:r0 + row_tile, :] + part) * gc_ref[s, 0, r0:r0 + row_tile, 0:1]
        for k in range(ROW_TILE):
            ye_ref[s, 0, pl.ds(r0 * ROW_TILE + k, row_tile, stride=ROW_TILE), :] = (
                ye[:, k * LANES:(k + 1) * LANES])

    pl.when(f == 0)(lambda: for_each_tile(first))
    pl.when((f > 0) & (f < last))(lambda: for_each_tile(middle))
    pl.when(f == last)(lambda: for_each_tile(final))


def _ffn(xe, gc, w_gate, w_up, w_down, layer):
    ns = xe.shape[0]
    return pl.pallas_call(
        functools.partial(_ffn_kernel, row_tile=512),
        grid=(N_EXPERTS, EXPERT_FF // FF_TILE),
        in_specs=[
            pl.BlockSpec((ns, 1, CAP, D_MODEL), lambda e, f: (0, e, 0, 0)),
            pl.BlockSpec((1, 1, D_MODEL, FF_TILE), lambda e, f: (layer, e, 0, f)),
            pl.BlockSpec((1, 1, D_MODEL, FF_TILE), lambda e, f: (layer, e, 0, f)),
            pl.BlockSpec((1, 1, FF_TILE, D_MODEL), lambda e, f: (layer, e, f, 0)),
            pl.BlockSpec((ns, 1, CAP, LANES), lambda e, f: (0, e, 0, 0)),
        ],
        out_specs=pl.BlockSpec((ns, 1, CAP * ROW_TILE, LANES), lambda e, f: (0, e, 0, 0)),
        out_shape=jax.ShapeDtypeStruct((ns, N_EXPERTS, CAP * ROW_TILE, LANES), F32),
        scratch_shapes=[pltpu.VMEM((ns, CAP, D_MODEL), F32)],
        compiler_params=_params("arbitrary", "arbitrary"),
        name="moe_ffn",
    )(xe, w_gate, w_up, w_down, gc)


def _scatter_kernel(idx_ref, ye_ref, y_ref, gate_ref, *refs, split):
    out_refs, acc_ref = refs[:-1], refs[-1]
    s, j = pl.program_id(0), pl.program_id(1)

    @pl.when(j == 0)
    def _():
        acc_ref[...] = jnp.zeros_like(acc_ref)

    @pl.when(j < N_EXPERTS)
    def _():
        def add_rows(g, carry):
            r0 = g * SCATTER_UNROLL
            group = ye_ref.at[0, 0, pl.ds(pl.multiple_of(r0 * ROW_TILE, SCATTER_UNROLL * ROW_TILE),
                                          SCATTER_UNROLL * ROW_TILE), :]
            dst = [_tile_rows(idx_ref[0, 0, 0, r0 + u]) for u in range(SCATTER_UNROLL)]
            val = [acc_ref[dst[u], :] + group[u * ROW_TILE:(u + 1) * ROW_TILE, :] for u in range(SCATTER_UNROLL)]
            for u in range(SCATTER_UNROLL):
                acc_ref[dst[u], :] = val[u]
            return carry

        lax.fori_loop(0, CAP // SCATTER_UNROLL, add_rows, 0)

    @pl.when(j >= N_EXPERTS)
    def _():
        c = j - N_EXPERTS
        rows = FIN_TOK * ROW_TILE
        part = acc_ref.at[pl.ds(pl.multiple_of(c * rows, rows), rows), :]
        moe = jnp.concatenate([part[pl.ds(k, FIN_TOK, stride=ROW_TILE), :] for k in range(ROW_TILE)], axis=1)
        res = y_ref[...] + gate_ref[0] * moe
        if split:
            for si, out_ref in enumerate(out_refs):
                @pl.when(s == si)
                def _(out_ref=out_ref):
                    out_ref[...] = res
        else:
            out_refs[0][...] = res


def _scatter(idx, ye_tiles, y, mod3, layer, lat_seq, split):
    ns = ye_tiles.shape[0]
    fin = lambda j: jnp.maximum(j - N_EXPERTS, 0)

    def gate_index(s, j):
        row = _mod_row(s * N_FIN + fin(j), FIN_TOK, lat_seq)
        return ((layer * N_MOD_ROWS + row) * 6 + 5, 0, 0)

    chunk = (FIN_TOK, D_MODEL)
    if split:
        out_specs = [
            pl.BlockSpec(chunk, lambda s, j, si=si: (
                jnp.where(s == si, fin(j), jnp.where(s < si, 0, N_FIN - 1)), 0))
            for si in range(ns)]
        out_shape = [jax.ShapeDtypeStruct((N_TOK, D_MODEL), F32)] * ns
    else:
        out_specs = pl.BlockSpec(chunk, lambda s, j: (s * N_FIN + fin(j), 0))
        out_shape = jax.ShapeDtypeStruct(y.shape, F32)
    expert = lambda j: jnp.minimum(j, N_EXPERTS - 1)
    return pl.pallas_call(
        functools.partial(_scatter_kernel, split=split),
        grid=(ns, N_EXPERTS + N_FIN),
        in_specs=[
            pl.BlockSpec((1, 1, 1, CAP), lambda s, j: (s, expert(j), 0, 0), memory_space=pltpu.SMEM),
            pl.BlockSpec((1, 1, CAP * ROW_TILE, LANES), lambda s, j: (s, expert(j), 0, 0)),
            pl.BlockSpec(chunk, lambda s, j: (s * N_FIN + fin(j), 0)),
            pl.BlockSpec((1, 1, D_MODEL), gate_index),
        ],
        out_specs=out_specs,
        scratch_shapes=[pltpu.VMEM((N_TOK * ROW_TILE, LANES), F32)],
        out_shape=out_shape,
        compiler_params=_params("arbitrary", "arbitrary"),
        name="moe_scatter",
    )(idx.reshape(ns, N_EXPERTS, 1, CAP), ye_tiles, y, mod3)


def _rope_tables(n_tokens):
    t = np.arange(n_tokens)
    row = (t // GRID_W).astype(np.float32)
    col = (t % GRID_W).astype(np.float32)
    pairs = HEAD_DIM // 4
    inv_freq = ROPE_THETA ** (-jnp.arange(pairs, dtype=F32) / pairs)
    ang = jnp.concatenate([row[:, None] * inv_freq, col[:, None] * inv_freq], axis=-1)
    cos, sin = jnp.cos(ang), jnp.sin(ang)
    reps = MXU_DIM // HEAD_DIM
    return (jnp.tile(jnp.concatenate([cos, cos], axis=-1), (1, reps)),
            jnp.tile(jnp.concatenate([-sin, sin], axis=-1), (1, reps)))


def _dup_cache(cache):
    b, s, hk, hd = cache.shape
    return jnp.broadcast_to(cache[:, :, :, None, :], (b, s, hk, 2, hd)).reshape(b, s, 2 * hk * hd).astype(BF16)


def kernel(x_prompt, x_sample, cache_attn_k, cache_attn_v, cache_na_k, cache_na_v, c, c_ctx,
           norm1_g, norm2_g, w_ada, b_ada, attn_w_qkv, attn_q_gain, attn_k_gain, attn_w_o,
           na_w_qkv, na_q_gain, na_k_gain, na_rpb, na_w_o,
           moe_w_router, moe_w_gate, moe_w_up, moe_w_down):
    bc, tc, _ = x_prompt.shape
    bl, tl, _ = x_sample.shape
    depth = w_ada.shape[0]
    assert bc * tc == N_TOK and bl * tl == N_TOK and 1 + bl <= N_MOD_ROWS
    assert tl % TM == 0 and tl % FIN_TOK == 0

    cond = jnp.zeros((N_MOD_ROWS, D_MODEL), F32).at[0].set(c_ctx).at[1:1 + bl].set(c)
    mod3 = _ada(cond, w_ada, b_ada).reshape(depth * N_MOD_ROWS * 6, 1, D_MODEL)

    seg = jnp.asarray(np.kron(np.eye(MXU_DIM // HEAD_DIM), np.ones((HEAD_DIM, HEAD_DIM))) / HEAD_DIM, BF16)
    tri = jnp.asarray(np.triu(np.ones((MXU_DIM, MXU_DIM)), k=1), BF16)
    rope_tables = _rope_tables(tl)

    x_ctx = (x_prompt.reshape(N_TOK, D_MODEL), 0)
    x_lat = (x_sample.reshape(N_TOK, D_MODEL), 0)
    new_k, new_v = [], []
    for i in range(depth):
        j = i // 2
        gqa = i % 2 == 0
        if gqa:
            w_qkv, q_gain, k_gain, w_o = attn_w_qkv[j], attn_q_gain[j], attn_k_gain[j], attn_w_o[j]
            kv_heads = N_KV_HEADS
        else:
            w_qkv, q_gain, k_gain, w_o = na_w_qkv[j], na_q_gain[j], na_k_gain[j], na_w_o[j]
            kv_heads = N_HEADS
        wr_hi, wr_lo = _split_bf16(moe_w_router[i].T)

        w_qkv_b = w_qkv.astype(BF16)
        qkv_args = (norm1_g[i], mod3, i, tl, w_qkv_b, q_gain, k_gain, seg, kv_heads * HEAD_DIM)
        q_c, kb_c, vb_c, kf, vf = _qkv(x_ctx, True, *qkv_args, None)
        q_l, kb_l, vb_l = _qkv(x_lat, False, *qkv_args, rope_tables if gqa else None)
        o_ctx = _attention(*(a.reshape(bc, tc, a.shape[1]) for a in (q_c, kb_c, vb_c)), tq=tc, nb=2)
        q_l, kb_l, vb_l = (a.reshape(bl, tl, a.shape[1]) for a in (q_l, kb_l, vb_l))
        if gqa:
            o_lat = _attention(q_l, kb_l, vb_l,
                               _dup_cache(cache_attn_k[:, j]), _dup_cache(cache_attn_v[:, j]), tq=512)
        else:
            past = cache_na_k.shape[2]
            o_lat = _na_attention(q_l, kb_l, vb_l,
                                  cache_na_k[:, j].reshape(bl, past, D_MODEL).astype(BF16),
                                  cache_na_v[:, j].reshape(bl, past, D_MODEL).astype(BF16),
                                  _na_bias_table(na_rpb[j]))
        new_k.append(kf.reshape(bc, 1, tc, kv_heads, HEAD_DIM))
        new_v.append(vf.reshape(bc, 1, tc, kv_heads, HEAD_DIM))

        y, h2, logits_t = _post_attention(o_ctx.reshape(N_TOK, D_MODEL), o_lat.reshape(N_TOK, D_MODEL),
                                          x_ctx, x_lat, w_o.astype(BF16), norm2_g[i], mod3, i, tl,
                                          wr_hi, wr_lo)
        idx, gc = _plan(logits_t, tri)
        xe = _gather(idx, h2.reshape(N_STREAMS, N_TOK * ROW_TILE, LANES))
        ye = _ffn(xe, gc, moe_w_gate, moe_w_up, moe_w_down, i)
        y = _scatter(idx, ye, y, mod3, i, tl, split=(i == depth - 1))
        x_ctx, x_lat = (y, 0), (y, CTX_TILES)

    y_ctx, y_lat = y
    return (y_ctx.reshape(bc, tc, D_MODEL), y_lat.reshape(bl, tl, D_MODEL),
            jnp.concatenate(new_k[0::2], axis=1), jnp.concatenate(new_v[0::2], axis=1),
            jnp.concatenate(new_k[1::2], axis=1), jnp.concatenate(new_v[1::2], axis=1))
```

```python
import functools

import jax
import jax.numpy as jnp
import numpy as np
from jax import lax
from jax.experimental import pallas as pl
from jax.experimental.pallas import tpu as pltpu

F32 = jnp.float32
BF16 = jnp.bfloat16
I32 = jnp.int32

D_MODEL = 1024
N_HEADS = 16
N_KV_HEADS = 4
HEAD_DIM = 64
GRID_W = 64
WIN_R = 8
WIN_C = 16
N_EXPERTS = 16
EXPERT_FF = 2048
ROPE_THETA = 10000.0
RMS_EPS = 1e-6
NEG_INF = -1e30
F32_MIN_NORMAL_BITS = 0x00800000
F32_VALUE_BITS = 31
REFINE_STEPS = 64
LOG2_E = 1.4426950408889634
Q_SCALE = HEAD_DIM ** -0.5 * LOG2_E

LANES = 128
SUBLANES = 8
MXU_DIM = 256
VMEM_LIMIT = 56 * 1024 * 1024

N_STREAMS = 2
N_TOK = 8192
CAP = 2 * N_TOK // N_EXPERTS
TM = 512
N_MOD_ROWS = 16
ROW_TILE = D_MODEL // LANES
GROUP = MXU_DIM
N_GROUPS = N_TOK // GROUP
FIN_TOK = 512
N_FIN = N_TOK // FIN_TOK
GATHER_UNROLL = 32
SCATTER_UNROLL = 16
FF_TILE = 512
NA_QBLK = 256
NA_QROWS = NA_QBLK // GRID_W
NA_BAND = WIN_R + NA_QROWS
NA_MAX_SHIFT = WIN_R - 2 + NA_QROWS
NA_TILES = NA_MAX_SHIFT + 1
N_REL_R = 2 * WIN_R - 1
N_REL_C = 2 * WIN_C - 1


def _params(*sem):
    return pltpu.CompilerParams(dimension_semantics=sem, vmem_limit_bytes=VMEM_LIMIT)


def _dot(a, b):
    return jnp.dot(a, b, preferred_element_type=F32)


def _dot_nt(a, b):
    return lax.dot_general(a, b, (((1,), (1,)), ((), ())), preferred_element_type=F32)


def _split_bf16(x):
    hi = x.astype(BF16)
    lo = (x - hi.astype(F32)).astype(BF16)
    return hi, lo


def _silu(x):
    return x * (1.0 / (1.0 + jnp.exp(-x)))


def _ada_kernel(cond_ref, w_ref, b_ref, out_ref):
    sx = _silu(cond_ref[...])
    xh, xl = _split_bf16(sx)
    wh, wl = _split_bf16(w_ref[0])
    out_ref[0] = _dot(xh, wh) + _dot(xl, wh) + _dot(xh, wl) + b_ref[0]


def _ada(cond, w_ada, b_ada):
    depth = w_ada.shape[0]
    tn = 1024
    n_out = w_ada.shape[2]
    return pl.pallas_call(
        _ada_kernel,
        grid=(depth, n_out // tn),
        in_specs=[
            pl.BlockSpec((N_MOD_ROWS, D_MODEL), lambda l, n: (0, 0)),
            pl.BlockSpec((1, D_MODEL, tn), lambda l, n: (l, 0, n)),
            pl.BlockSpec((1, 1, tn), lambda l, n: (l, 0, n)),
        ],
        out_specs=pl.BlockSpec((1, N_MOD_ROWS, tn), lambda l, n: (l, 0, n)),
        out_shape=jax.ShapeDtypeStruct((depth, N_MOD_ROWS, n_out), F32),
        compiler_params=_params("arbitrary", "arbitrary"),
        name="ada",
    )(cond, w_ada, b_ada.reshape(depth, 1, n_out))


def _mod_row(tile, tile_rows, lat_seq):
    ctx_tiles = N_TOK // tile_rows
    return jnp.where(tile < ctx_tiles, 0, 1 + (tile - ctx_tiles) // (lat_seq // tile_rows))


def _mod_spec(layer, which, lat_seq, tile_of=lambda i: i):
    def index(i):
        return ((layer * N_MOD_ROWS + _mod_row(tile_of(i), TM, lat_seq)) * 6 + which, 0, 0)
    return pl.BlockSpec((1, 1, D_MODEL), index)


CTX_TILES = N_TOK // TM


def _x_specs(x_ctx_block0, x_lat_block0, tile_of=lambda i: i):
    return [
        pl.BlockSpec((TM, D_MODEL), lambda i: (x_ctx_block0 + jnp.minimum(tile_of(i), CTX_TILES - 1), 0)),
        pl.BlockSpec((TM, D_MODEL), lambda i: (x_lat_block0 + jnp.maximum(tile_of(i) - CTX_TILES, 0), 0)),
    ]


def _head_norm(z, seg, gain):
    ms = _dot((z * z).astype(BF16), seg)
    return z * lax.rsqrt(ms + RMS_EPS) * gain


def _rope(z, cos_t, sin_t):
    lane = lax.broadcasted_iota(I32, z.shape, 1)
    first = (lane & 32) == 0
    n = z.shape[1]
    partner = jnp.where(first, pltpu.roll(z, n - 32, axis=1), pltpu.roll(z, 32, axis=1))
    return z * cos_t + partner * sin_t


def _dup_heads(z):
    outs = []
    for b in range(z.shape[1] // LANES):
        x = z[:, b * LANES:(b + 1) * LANES]
        xr = pltpu.roll(x, HEAD_DIM, axis=1)
        lo = lax.broadcasted_iota(I32, x.shape, 1) < HEAD_DIM
        outs.append(jnp.where(lo, x, xr))
        outs.append(jnp.where(lo, xr, x))
    return jnp.concatenate(outs, axis=1)


def _store_head_rows(dst_ref, c0, z, n_heads):
    for b in range(z.shape[1] // LANES):
        pair = z[:, b * LANES:(b + 1) * LANES]
        swapped = pltpu.roll(pair, HEAD_DIM, axis=1)
        for hh, val in enumerate((pair, swapped)):
            head = c0 // HEAD_DIM + 2 * b + hh
            dst_ref[pl.ds(head, TM, stride=n_heads), :] = val[:, :HEAD_DIM]


def _qkv_kernel(*refs, kv_width, rope, emit_heads):
    it = iter(refs)
    x_ref, g_ref, shift_ref, scale_ref, w_ref, qg_ref, kg_ref, seg_ref = (next(it) for _ in range(8))
    cos_ref = sin_ref = None
    if rope:
        cos_ref, sin_ref = next(it), next(it)
    q_ref, kb_ref, vb_ref = (next(it) for _ in range(3))
    kh_ref = vh_ref = None
    if emit_heads:
        kh_ref, vh_ref = next(it), next(it)
    dup = kv_width == N_KV_HEADS * HEAD_DIM
    n_heads = kv_width // HEAD_DIM

    x = x_ref[...]
    ms = jnp.mean(x * x, axis=-1, keepdims=True)
    h = x * lax.rsqrt(ms + RMS_EPS) * (g_ref[...] * (1.0 + scale_ref[0])) + shift_ref[0]
    hb = h.astype(BF16)
    seg = seg_ref[...]
    if rope:
        cos_t, sin_t = cos_ref[...], sin_ref[...]

    q_width = N_HEADS * HEAD_DIM

    def finish_q(c0, raw):
        z = _head_norm(raw, seg, qg_ref[...])
        if rope:
            z = _rope(z, cos_t, sin_t)
        q_ref[:, c0:c0 + MXU_DIM] = z.astype(BF16)

    def finish_k(c0, raw):
        z = _head_norm(raw, seg, kg_ref[...])
        if emit_heads:
            _store_head_rows(kh_ref, c0, z, n_heads)
        if rope:
            z = _rope(z, cos_t, sin_t)
        if dup:
            kb_ref[:, 2 * c0:2 * c0 + 2 * MXU_DIM] = _dup_heads(z).astype(BF16)
        else:
            kb_ref[:, c0:c0 + MXU_DIM] = z.astype(BF16)

    def finish_v(c0, raw):
        if emit_heads:
            _store_head_rows(vh_ref, c0, raw, n_heads)
        if dup:
            vb_ref[:, 2 * c0:2 * c0 + 2 * MXU_DIM] = _dup_heads(raw).astype(BF16)
        else:
            vb_ref[:, c0:c0 + MXU_DIM] = raw.astype(BF16)

    q_chunks = [(finish_q, c0, c0) for c0 in range(0, q_width, MXU_DIM)]
    k_chunks = [(finish_k, c0, q_width + c0) for c0 in range(0, kv_width, MXU_DIM)]
    v_chunks = [(finish_v, c0, q_width + kv_width + c0) for c0 in range(0, kv_width, MXU_DIM)]
    normed = q_chunks + k_chunks
    chunks = []
    for i, chunk in enumerate(normed):
        chunks.append(chunk)
        if i * len(v_chunks) // len(normed) != (i + 1) * len(v_chunks) // len(normed):
            chunks.append(v_chunks[i * len(v_chunks) // len(normed)])

    project = lambda col: _dot(hb, w_ref[:, col:col + MXU_DIM])
    raw = project(chunks[0][2])
    for i, (finish, c0, _) in enumerate(chunks):
        nxt = project(chunks[i + 1][2]) if i + 1 < len(chunks) else None
        finish(c0, raw)
        raw = nxt

def _qkv(x_src, is_ctx, norm_g, mod3, layer, lat_seq, w_bf16, q_gain, k_gain, seg, kv_width, rope_tables):
    n_w = w_bf16.shape[1]
    kvb_width = 2 * kv_width if kv_width == N_KV_HEADS * HEAD_DIM else kv_width
    n_heads = kv_width // HEAD_DIM
    tile_of = (lambda i: i) if is_ctx else (lambda i: i + CTX_TILES)
    tile = lambda w: pl.BlockSpec((TM, w), lambda i: (i, 0))
    const = lambda shape: pl.BlockSpec(shape, lambda i: (0,) * len(shape))
    in_specs = [
        pl.BlockSpec((TM, D_MODEL), lambda i: (x_src[1] + i, 0)),
        const((1, D_MODEL)),
        _mod_spec(layer, 0, lat_seq, tile_of),
        _mod_spec(layer, 1, lat_seq, tile_of),
        const((D_MODEL, n_w)),
        const((1, MXU_DIM)),
        const((1, MXU_DIM)),
        const((MXU_DIM, MXU_DIM)),
    ]
    args = [x_src[0], norm_g.reshape(1, D_MODEL), mod3, mod3, w_bf16,
            jnp.tile(q_gain * Q_SCALE, MXU_DIM // HEAD_DIM).reshape(1, MXU_DIM),
            jnp.tile(k_gain, MXU_DIM // HEAD_DIM).reshape(1, MXU_DIM), seg]
    if rope_tables is not None:
        seq_tiles = lat_seq // TM
        in_specs += [pl.BlockSpec((TM, MXU_DIM), lambda i: (i % seq_tiles, 0))] * 2
        args += list(rope_tables)
    out_specs = [tile(D_MODEL), tile(kvb_width), tile(kvb_width)]
    out_shape = [jax.ShapeDtypeStruct((N_TOK, D_MODEL), BF16),
                 jax.ShapeDtypeStruct((N_TOK, kvb_width), BF16),
                 jax.ShapeDtypeStruct((N_TOK, kvb_width), BF16)]
    if is_ctx:
        out_specs += [pl.BlockSpec((TM * n_heads, HEAD_DIM), lambda i: (i, 0))] * 2
        out_shape += [jax.ShapeDtypeStruct((N_TOK * n_heads, HEAD_DIM), F32)] * 2
    return pl.pallas_call(
        functools.partial(_qkv_kernel, kv_width=kv_width, rope=rope_tables is not None, emit_heads=is_ctx),
        grid=(N_TOK // TM,),
        in_specs=in_specs,
        out_specs=out_specs,
        out_shape=out_shape,
        compiler_params=_params("arbitrary"),
        name="qkv",
    )(*args)


def _two_head_rows(qj):
    lo = lax.broadcasted_iota(I32, qj.shape, 1) < HEAD_DIM
    zero = jnp.zeros_like(qj)
    return jnp.concatenate([jnp.where(lo, qj, zero), jnp.where(lo, zero, qj)], axis=0)


def _merge_two_heads(r):
    tq = r.shape[0] // 2
    lo = lax.broadcasted_iota(I32, (tq, LANES), 1) < HEAD_DIM
    return jnp.where(lo, r[:tq], r[tq:])


def _attn_kernel(*refs, n_kv_blocks, has_ctx):
    if has_ctx:
        q_ref, k_ref, v_ref, ck_ref, cv_ref, o_ref = refs
    else:
        q_ref, k_ref, v_ref, o_ref = refs
    n_q_blocks = D_MODEL // LANES
    key_lanes = lambda j: slice(((j * n_kv_blocks) // n_q_blocks) * LANES,
                                ((j * n_kv_blocks) // n_q_blocks + 1) * LANES)

    items = [(b, j) for b in range(q_ref.shape[0]) for j in range(n_q_blocks)]

    def scores(b, j):
        q2 = _two_head_rows(q_ref[b, :, j * LANES:(j + 1) * LANES])
        s = _dot_nt(q2, k_ref[b, :, key_lanes(j)])
        sc = _dot_nt(q2, ck_ref[b, :, key_lanes(j)]) if has_ctx else None
        return s, sc

    nxt = scores(*items[0])
    for i, (b, j) in enumerate(items):
        (s, sc), ksl = nxt, key_lanes(j)
        if i + 1 < len(items):
            nxt = scores(*items[i + 1])
        m = jnp.max(s, axis=-1, keepdims=True)
        if has_ctx:
            m = jnp.maximum(m, jnp.max(sc, axis=-1, keepdims=True))
        p = jnp.exp2(s - m)
        l = jnp.sum(p, axis=-1, keepdims=True)
        r = _dot(p.astype(BF16), v_ref[b, :, ksl])
        if has_ctx:
            pc = jnp.exp2(sc - m)
            l = l + jnp.sum(pc, axis=-1, keepdims=True)
            r = r + _dot(pc.astype(BF16), cv_ref[b, :, ksl])
        r = r / l
        o_ref[b, :, j * LANES:(j + 1) * LANES] = _merge_two_heads(r).astype(BF16)


def _attention(q, k, v, ctx_k=None, ctx_v=None, tq=256, nb=1):
    n_batch, t, _ = q.shape
    s, w = k.shape[1], k.shape[2]
    has_ctx = ctx_k is not None
    in_specs = [
        pl.BlockSpec((nb, tq, D_MODEL), lambda bi, qi: (bi, qi, 0)),
        pl.BlockSpec((nb, s, w), lambda bi, qi: (bi, 0, 0)),
        pl.BlockSpec((nb, s, w), lambda bi, qi: (bi, 0, 0)),
    ]
    args = [q, k, v]
    if has_ctx:
        sc = ctx_k.shape[1]
        in_specs += [pl.BlockSpec((nb, sc, w), lambda bi, qi: (bi, 0, 0))] * 2
        args += [ctx_k, ctx_v]
    return pl.pallas_call(
        functools.partial(_attn_kernel, n_kv_blocks=w // LANES, has_ctx=has_ctx),
        grid=(n_batch // nb, t // tq),
        in_specs=in_specs,
        out_specs=pl.BlockSpec((nb, tq, D_MODEL), lambda bi, qi: (bi, qi, 0)),
        out_shape=jax.ShapeDtypeStruct((n_batch, t, D_MODEL), BF16),
        compiler_params=_params("arbitrary", "arbitrary"),
        name="attn",
    )(*args)


def _na_kernel(q_ref, k_ref, v_ref, ck_ref, cv_ref, bias_ref, o_ref, *, rows):
    blk = pl.program_id(1)
    q_row0 = NA_QROWS * blk
    band_start = jnp.minimum(jnp.clip(q_row0 - WIN_R // 2, 0, rows - WIN_R), rows - NA_BAND)
    n_band = NA_BAND * GRID_W
    band = pl.ds(pl.multiple_of(band_start * GRID_W, LANES), n_band)
    shape = (NA_QBLK, n_band)
    q_r = q_row0 + lax.broadcasted_iota(I32, shape, 0) // GRID_W
    k_r = band_start + lax.broadcasted_iota(I32, shape, 1) // GRID_W
    q_rs = jnp.clip(q_r - WIN_R // 2, 0, rows - WIN_R)
    row_ok1 = (k_r >= q_rs) & (k_r < q_rs + WIN_R)
    row_ok = jnp.concatenate([row_ok1, row_ok1], axis=0)
    shift0 = (band_start - q_row0 + NA_MAX_SHIFT) // 2
    n_q_blocks = D_MODEL // LANES

    def scores(j):
        sl = slice(j * LANES, (j + 1) * LANES)
        q2 = _two_head_rows(q_ref[0, :, sl])
        return _dot_nt(q2, k_ref[0, band, sl]), _dot_nt(q2, ck_ref[0, :, sl])

    nxt = scores(0)
    for j in range(n_q_blocks):
        sl = slice(j * LANES, (j + 1) * LANES)
        s, sc = nxt
        if j + 1 < n_q_blocks:
            nxt = scores(j + 1)
        bias = jnp.concatenate(
            [jnp.concatenate([bias_ref[2 * j + hh, shift0 + m - u] for m in range(NA_BAND // 2)], axis=1)
             for hh in range(2) for u in range(NA_QROWS // 2)], axis=0)
        s = jnp.where(row_ok, s + bias, NEG_INF)
        m = jnp.maximum(jnp.max(s, axis=-1, keepdims=True), jnp.max(sc, axis=-1, keepdims=True))
        p = jnp.exp2(s - m)
        pc = jnp.exp2(sc - m)
        l = jnp.sum(p, axis=-1, keepdims=True) + jnp.sum(pc, axis=-1, keepdims=True)
        r = _dot(p.astype(BF16), v_ref[0, band, sl]) + _dot(pc.astype(BF16), cv_ref[0, :, sl])
        r = r / l
        o_ref[0, :, sl] = _merge_two_heads(r).astype(BF16)


def _na_bias_kernel(rpb_ref, out_ref, tz_ref):
    h = pl.program_id(0)
    shape = (GRID_W, LANES)
    q_col = lax.broadcasted_iota(I32, shape, 0)
    lane = lax.broadcasted_iota(I32, shape, 1)
    k_col = lane & (GRID_W - 1)
    col_start = jnp.clip(q_col - WIN_C // 2, 0, GRID_W - WIN_C)
    col_ok = (k_col >= col_start) & (k_col < col_start + WIN_C)
    rel_c = jnp.clip(k_col - q_col + WIN_C - 1, 0, N_REL_C - 1)
    for a in range(N_REL_R):
        acc = jnp.zeros(shape, F32)
        for b in range(N_REL_C):
            acc = jnp.where(rel_c == b, rpb_ref[(h * N_REL_R + a) * N_REL_C + b], acc)
        tz_ref[a] = jnp.where(col_ok, acc * LOG2_E, NEG_INF)
    left = lane < GRID_W
    for dd in range(NA_TILES):
        for qr in range(2):
            rel = [min(max(2 * dd - NA_MAX_SHIFT + kr - qr + WIN_R - 1, 0), N_REL_R - 1) for kr in range(2)]
            out_ref[0, dd, qr * GRID_W:(qr + 1) * GRID_W, :] = jnp.where(left, tz_ref[rel[0]], tz_ref[rel[1]])


def _na_bias_table(rpb):
    return pl.pallas_call(
        _na_bias_kernel,
        grid=(N_HEADS,),
        in_specs=[pl.BlockSpec(memory_space=pltpu.SMEM)],
        out_specs=pl.BlockSpec((1, NA_TILES, LANES, LANES), lambda h: (h, 0, 0, 0)),
        out_shape=jax.ShapeDtypeStruct((N_HEADS, NA_TILES, LANES, LANES), F32),
        scratch_shapes=[pltpu.VMEM((N_REL_R, GRID_W, LANES), F32)],
        compiler_params=_params("arbitrary"),
        name="na_bias",
    )(rpb.reshape(-1))


def _na_attention(q, k, v, ctx_k, ctx_v, bias_tbl):
    n_batch, t, _ = q.shape
    sc = ctx_k.shape[1]
    rows = t // GRID_W
    full = lambda n: pl.BlockSpec((1, n, D_MODEL), lambda bi, qi: (bi, 0, 0))
    return pl.pallas_call(
        functools.partial(_na_kernel, rows=rows),
        grid=(n_batch, t // NA_QBLK),
        in_specs=[
            pl.BlockSpec((1, NA_QBLK, D_MODEL), lambda bi, qi: (bi, qi, 0)),
            full(t), full(t), full(sc), full(sc),
            pl.BlockSpec(bias_tbl.shape, lambda bi, qi: (0, 0, 0, 0)),
        ],
        out_specs=pl.BlockSpec((1, NA_QBLK, D_MODEL), lambda bi, qi: (bi, qi, 0)),
        out_shape=jax.ShapeDtypeStruct((n_batch, t, D_MODEL), BF16),
        compiler_params=_params("arbitrary", "arbitrary"),
        name="na_attn",
    )(q, k, v, ctx_k, ctx_v, bias_tbl)


def _post_kernel(oc_ref, ol_ref, xc_ref, xl_ref, wo_ref, gate_ref, g_ref, shift_ref, scale_ref,
                 wrh_ref, wrl_ref, y_ref, h_ref, lg_ref):
    is_ctx = pl.program_id(0) < CTX_TILES
    o = jnp.where(is_ctx, oc_ref[...], ol_ref[...])
    x = jnp.where(is_ctx, xc_ref[...], xl_ref[...])
    half = TM // 2
    proj = [_dot(o[a * half:(a + 1) * half], wo_ref[...]) for a in range(2)]
    gain = g_ref[...] * (1.0 + scale_ref[0])
    for a in range(2):
        rows = slice(a * half, (a + 1) * half)
        y = x[rows] + gate_ref[0] * proj[a]
        y_ref[rows, :] = y
        ms = jnp.mean(y * y, axis=-1, keepdims=True)
        h = y * lax.rsqrt(ms + RMS_EPS) * gain + shift_ref[0]
        for k in range(ROW_TILE):
            h_ref[pl.ds(a * half * ROW_TILE + k, half, stride=ROW_TILE), :] = h[:, k * LANES:(k + 1) * LANES]
        hh, hl = _split_bf16(h)
        lg_ref[:, rows] = _dot_nt(wrh_ref[...], hh) + _dot_nt(wrh_ref[...], hl) + _dot_nt(wrl_ref[...], hh)


def _post_attention(o_ctx, o_lat, x_ctx, x_lat, wo_bf16, norm_g, mod3, layer, lat_seq, wr_hi, wr_lo):
    n_tok = N_STREAMS * N_TOK
    tile = lambda w: pl.BlockSpec((TM, w), lambda i: (i, 0))
    const = lambda shape: pl.BlockSpec(shape, lambda i: (0,) * len(shape))
    return pl.pallas_call(
        _post_kernel,
        grid=(n_tok // TM,),
        in_specs=_x_specs(0, 0) + _x_specs(x_ctx[1], x_lat[1]) + [
            const((D_MODEL, D_MODEL)),
            _mod_spec(layer, 2, lat_seq),
            const((1, D_MODEL)),
            _mod_spec(layer, 3, lat_seq),
            _mod_spec(layer, 4, lat_seq),
            const((N_EXPERTS, D_MODEL)), const((N_EXPERTS, D_MODEL)),
        ],
        out_specs=[tile(D_MODEL), pl.BlockSpec((TM * ROW_TILE, LANES), lambda i: (i, 0)),
                   pl.BlockSpec((N_EXPERTS, TM), lambda i: (0, i))],
        out_shape=[jax.ShapeDtypeStruct((n_tok, D_MODEL), F32),
                   jax.ShapeDtypeStruct((n_tok * ROW_TILE, LANES), F32),
                   jax.ShapeDtypeStruct((N_EXPERTS, n_tok), F32)],
        compiler_params=_params("arbitrary"),
        name="post_attn",
    )(o_ctx, o_lat, x_ctx[0], x_lat[0], wo_bf16, mod3, norm_g.reshape(1, D_MODEL), mod3, mod3, wr_hi, wr_lo)


def _group_prefix(mask, tri):
    local, offs = [], []
    off = jnp.zeros((mask.shape[0], 1), F32)
    for g in range(mask.shape[1] // GROUP):
        xg = mask[:, g * GROUP:(g + 1) * GROUP]
        offs.append(off)
        local.append(_dot(xg.astype(BF16), tri))
        off = off + jnp.sum(xg, axis=1, keepdims=True)
    offs.append(off)
    return local, offs


def _split3_bf16(x):
    hi = x.astype(BF16)
    r1 = x - hi.astype(F32)
    mid = r1.astype(BF16)
    lo = (r1 - mid.astype(F32)).astype(BF16)
    return hi, mid, lo


def _plan_kernel(lg_ref, tri_ref, idx_ref, gc_ref, cnt_ref, affg_ref):
    lg = lg_ref[...]
    ex = jnp.exp(lg - jnp.max(lg, axis=0, keepdims=True))
    aff = ex / jnp.sum(ex, axis=0, keepdims=True)

    def count_ge(v):
        return jnp.sum(jnp.where(aff >= v, 1.0, 0.0), axis=1, keepdims=True)

    def search(i, thr):
        cand = thr | jnp.left_shift(jnp.int32(1), F32_VALUE_BITS - 1 - i)
        ok = (count_ge(lax.bitcast_convert_type(cand, F32)) >= CAP) & (cand >= F32_MIN_NORMAL_BITS)
        return jnp.where(ok, cand, thr)

    thr = lax.fori_loop(0, F32_VALUE_BITS, search, jnp.zeros((N_EXPERTS, 1), I32))
    lo = lax.bitcast_convert_type(thr, F32)
    hi = lax.bitcast_convert_type(jnp.maximum(thr + 1, F32_MIN_NORMAL_BITS), F32)

    midpoint = lambda lo, hi: lo + (hi - lo) * 0.5

    def splittable(state):
        i, lo, hi = state
        mid = midpoint(lo, hi)
        inside = jnp.max(jnp.where((mid > lo) & (mid < hi), 1.0, 0.0))
        return (i < REFINE_STEPS) & (inside > 0.0)

    def refine(state):
        i, lo, hi = state
        mid = midpoint(lo, hi)
        ok = count_ge(mid) >= CAP
        return i + 1, jnp.where(ok, mid, lo), jnp.where(ok, hi, mid)

    _, lo, hi = lax.while_loop(splittable, refine, (jnp.int32(0), lo, hi))
    tri = tri_ref[...]
    gt = aff >= hi
    eq = jnp.where((aff >= lo) & (aff < hi), 1.0, 0.0)
    need = CAP - jnp.sum(jnp.where(gt, 1.0, 0.0), axis=1, keepdims=True)
    eq_local, eq_offs = _group_prefix(eq, tri)
    eq_rank = jnp.concatenate([eq_local[g] + eq_offs[g] for g in range(N_GROUPS)], axis=1)
    sel = jnp.where(gt | ((eq > 0.0) & (eq_rank < need)), 1.0, 0.0)
    sel_local, offs = _group_prefix(sel, tri)
    for g in range(N_GROUPS):
        sl = slice(g * GROUP, (g + 1) * GROUP)
        cnt_ref[g] = sel_local[g] + sel[:, sl]
        affg_ref[g] = aff[:, sl]

    lane = lax.broadcasted_iota(I32, (N_EXPERTS, LANES), 1)
    never = jnp.full((N_EXPERTS, LANES), 2.0 * N_TOK, F32)
    grp_lo, grp_hi = never, never
    for g in range(N_GROUPS):
        grp_lo = jnp.where(lane == g, offs[g], grp_lo)
        grp_hi = jnp.where(lane == g, offs[g + 1], grp_hi)
    row = lax.broadcasted_iota(I32, (CAP, LANES), 0).astype(F32)
    in_group_lane = lax.broadcasted_iota(I32, (CAP, GROUP), 1).astype(F32)
    zpad = jnp.zeros((LANES - N_GROUPS, GROUP), BF16)
    for e in range(N_EXPERTS):
        lo_row, hi_row = grp_lo[e:e + 1, :], grp_hi[e:e + 1, :]
        in_grp = (lo_row <= row) & (row < hi_row)
        onehot = jnp.where(in_grp, 1.0, 0.0).astype(BF16)
        counts = _dot(onehot, jnp.concatenate([cnt_ref[:, e, :].astype(BF16), zpad], axis=0))
        rank = row[:, 0:1] - jnp.sum(jnp.where(in_grp, lo_row, 0.0), axis=1, keepdims=True)
        local = jnp.sum(jnp.where(counts <= rank, 1.0, 0.0), axis=1, keepdims=True)
        grp = jnp.sum(jnp.where(hi_row <= row, 1.0, 0.0), axis=1, keepdims=True)
        tok = grp * GROUP + local
        aff_rows = sum(_dot(onehot, jnp.concatenate([part, zpad], axis=0))
                       for part in _split3_bf16(affg_ref[:, e, :]))
        gate = jnp.sum(jnp.where(in_group_lane == local, aff_rows, 0.0), axis=1, keepdims=True)
        gc_ref[0, e] = jnp.broadcast_to(gate, (CAP, LANES))
        tok_b = jnp.broadcast_to(tok, (CAP, LANES))
        tok_row = jnp.concatenate([tok_b[t * LANES:(t + 1) * LANES, :].T[0:1, :]
                                   for t in range(CAP // LANES)], axis=1)
        idx_ref[0, e:e + 1, :] = tok_row.astype(I32)


def _plan(logits_t, tri):
    ns = N_STREAMS
    return pl.pallas_call(
        _plan_kernel,
        grid=(ns,),
        in_specs=[pl.BlockSpec((N_EXPERTS, N_TOK), lambda s: (0, s)),
                  pl.BlockSpec((MXU_DIM, MXU_DIM), lambda s: (0, 0))],
        out_specs=[pl.BlockSpec((1, N_EXPERTS, CAP), lambda s: (s, 0, 0)),
                   pl.BlockSpec((1, N_EXPERTS, CAP, LANES), lambda s: (s, 0, 0, 0))],
        out_shape=[jax.ShapeDtypeStruct((ns, N_EXPERTS, CAP), I32),
                   jax.ShapeDtypeStruct((ns, N_EXPERTS, CAP, LANES), F32)],
        scratch_shapes=[pltpu.VMEM((N_GROUPS, N_EXPERTS, GROUP), F32),
                        pltpu.VMEM((N_GROUPS, N_EXPERTS, GROUP), F32)],
        compiler_params=_params("arbitrary"),
        name="plan",
    )(logits_t, tri)


def _tile_rows(r):
    return pl.ds(pl.multiple_of(r * ROW_TILE, ROW_TILE), ROW_TILE)


def _gather_kernel(idx_ref, h_ref, xe_ref, stage_ref):
    sub = SUBLANES
    groups = GATHER_UNROLL // sub

    def move(g, carry):
        for u in range(GATHER_UNROLL):
            tile = h_ref[0, _tile_rows(idx_ref[0, 0, 0, g * GATHER_UNROLL + u]), :]
            stage_ref[g * groups + u // sub, pl.ds(u % sub, ROW_TILE, stride=sub), :] = tile
        return carry

    lax.fori_loop(0, CAP // GATHER_UNROLL, move, 0)
    for k in range(ROW_TILE):
        rows = stage_ref[:, k * sub:(k + 1) * sub, :].reshape(CAP, LANES)
        xe_ref[0, 0, :, k * LANES:(k + 1) * LANES] = rows.astype(BF16)


def _gather(idx, h2_tiles):
    ns = h2_tiles.shape[0]
    return pl.pallas_call(
        _gather_kernel,
        grid=(ns, N_EXPERTS),
        in_specs=[pl.BlockSpec((1, 1, 1, CAP), lambda s, e: (s, e, 0, 0), memory_space=pltpu.SMEM),
                  pl.BlockSpec((1, N_TOK * ROW_TILE, LANES), lambda s, e: (s, 0, 0),
                               pipeline_mode=pl.Buffered(1))],
        out_specs=pl.BlockSpec((1, 1, CAP, D_MODEL), lambda s, e: (s, e, 0, 0)),
        scratch_shapes=[pltpu.VMEM((CAP // SUBLANES, ROW_TILE * SUBLANES, LANES), F32)],
        out_shape=jax.ShapeDtypeStruct((ns, N_EXPERTS, CAP, D_MODEL), BF16),
        compiler_params=_params("arbitrary", "arbitrary"),
        name="moe_gather",
    )(idx.reshape(ns, N_EXPERTS, 1, CAP), h2_tiles)


def _ffn_kernel(x_ref, wg_ref, wu_ref, wd_ref, gc_ref, ye_ref, acc_ref, *, row_tile):
    f = pl.program_id(1)
    last = pl.num_programs(1) - 1
    ns = x_ref.shape[0]
    tiles = [(s, r0) for s in range(ns) for r0 in range(0, CAP, row_tile)]

    def for_each_tile(emit):
        wg = wg_ref[0, 0].astype(BF16)
        wu = wu_ref[0, 0].astype(BF16)
        wd = wd_ref[0, 0].astype(BF16)

        def gate_up(s, r0):
            x = x_ref[s, 0, r0:r0 + row_tile, :]
            return _dot(x, wg), _dot(x, wu)

        nxt = gate_up(*tiles[0])
        for i, (s, r0) in enumerate(tiles):
            gate, up = nxt
            if i + 1 < len(tiles):
                nxt = gate_up(*tiles[i + 1])
            emit(s, r0, _dot((_silu(gate) * up).astype(BF16), wd))

    def first(s, r0, part):
        acc_ref[s, r0:r0 + row_tile, :] = part

    def middle(s, r0, part):
        acc_ref[s, r0:r0 + row_tile, :] += part

    def final(s, r0, part):
        ye = (acc_ref[s, r0:r0 + row_tile, :] + part) * gc_ref[s, 0, r0:r0 + row_tile, 0:1]
        for k in range(ROW_TILE):
            ye_ref[s, 0, pl.ds(r0 * ROW_TILE + k, row_tile, stride=ROW_TILE), :] = (
                ye[:, k * LANES:(k + 1) * LANES])

    pl.when(f == 0)(lambda: for_each_tile(first))
    pl.when((f > 0) & (f < last))(lambda: for_each_tile(middle))
    pl.when(f == last)(lambda: for_each_tile(final))


def _ffn(xe, gc, w_gate, w_up, w_down, layer):
    ns = xe.shape[0]
    return pl.pallas_call(
        functools.partial(_ffn_kernel, row_tile=512),
        grid=(N_EXPERTS, EXPERT_FF // FF_TILE),
        in_specs=[
            pl.BlockSpec((ns, 1, CAP, D_MODEL), lambda e, f: (0, e, 0, 0)),
            pl.BlockSpec((1, 1, D_MODEL, FF_TILE), lambda e, f: (layer, e, 0, f)),
            pl.BlockSpec((1, 1, D_MODEL, FF_TILE), lambda e, f: (layer, e, 0, f)),
            pl.BlockSpec((1, 1, FF_TILE, D_MODEL), lambda e, f: (layer, e, f, 0)),
            pl.BlockSpec((ns, 1, CAP, LANES), lambda e, f: (0, e, 0, 0)),
        ],
        out_specs=pl.BlockSpec((ns, 1, CAP * ROW_TILE, LANES), lambda e, f: (0, e, 0, 0)),
        out_shape=jax.ShapeDtypeStruct((ns, N_EXPERTS, CAP * ROW_TILE, LANES), F32),
        scratch_shapes=[pltpu.VMEM((ns, CAP, D_MODEL), F32)],
        compiler_params=_params("arbitrary", "arbitrary"),
        name="moe_ffn",
    )(xe, w_gate, w_up, w_down, gc)


def _scatter_kernel(idx_ref, ye_ref, y_ref, gate_ref, *refs, split):
    out_refs, acc_ref = refs[:-1], refs[-1]
    s, j = pl.program_id(0), pl.program_id(1)

    @pl.when(j == 0)
    def _():
        acc_ref[...] = jnp.zeros_like(acc_ref)

    @pl.when(j < N_EXPERTS)
    def _():
        def add_rows(g, carry):
            r0 = g * SCATTER_UNROLL
            group = ye_ref.at[0, 0, pl.ds(pl.multiple_of(r0 * ROW_TILE, SCATTER_UNROLL * ROW_TILE),
                                          SCATTER_UNROLL * ROW_TILE), :]
            dst = [_tile_rows(idx_ref[0, 0, 0, r0 + u]) for u in range(SCATTER_UNROLL)]
            val = [acc_ref[dst[u], :] + group[u * ROW_TILE:(u + 1) * ROW_TILE, :] for u in range(SCATTER_UNROLL)]
            for u in range(SCATTER_UNROLL):
                acc_ref[dst[u], :] = val[u]
            return carry

        lax.fori_loop(0, CAP // SCATTER_UNROLL, add_rows, 0)

    @pl.when(j >= N_EXPERTS)
    def _():
        c = j - N_EXPERTS
        rows = FIN_TOK * ROW_TILE
        part = acc_ref.at[pl.ds(pl.multiple_of(c * rows, rows), rows), :]
        moe = jnp.concatenate([part[pl.ds(k, FIN_TOK, stride=ROW_TILE), :] for k in range(ROW_TILE)], axis=1)
        res = y_ref[...] + gate_ref[0] * moe
        if split:
            for si, out_ref in enumerate(out_refs):
                @pl.when(s == si)
                def _(out_ref=out_ref):
                    out_ref[...] = res
        else:
            out_refs[0][...] = res


def _scatter(idx, ye_tiles, y, mod3, layer, lat_seq, split):
    ns = ye_tiles.shape[0]
    fin = lambda j: jnp.maximum(j - N_EXPERTS, 0)

    def gate_index(s, j):
        row = _mod_row(s * N_FIN + fin(j), FIN_TOK, lat_seq)
        return ((layer * N_MOD_ROWS + row) * 6 + 5, 0, 0)

    chunk = (FIN_TOK, D_MODEL)
    if split:
        out_specs = [
            pl.BlockSpec(chunk, lambda s, j, si=si: (
                jnp.where(s == si, fin(j), jnp.where(s < si, 0, N_FIN - 1)), 0))
            for si in range(ns)]
        out_shape = [jax.ShapeDtypeStruct((N_TOK, D_MODEL), F32)] * ns
    else:
        out_specs = pl.BlockSpec(chunk, lambda s, j: (s * N_FIN + fin(j), 0))
        out_shape = jax.ShapeDtypeStruct(y.shape, F32)
    expert = lambda j: jnp.minimum(j, N_EXPERTS - 1)
    return pl.pallas_call(
        functools.partial(_scatter_kernel, split=split),
        grid=(ns, N_EXPERTS + N_FIN),
        in_specs=[
            pl.BlockSpec((1, 1, 1, CAP), lambda s, j: (s, expert(j), 0, 0), memory_space=pltpu.SMEM),
            pl.BlockSpec((1, 1, CAP * ROW_TILE, LANES), lambda s, j: (s, expert(j), 0, 0)),
            pl.BlockSpec(chunk, lambda s, j: (s * N_FIN + fin(j), 0)),
            pl.BlockSpec((1, 1, D_MODEL), gate_index),
        ],
        out_specs=out_specs,
        scratch_shapes=[pltpu.VMEM((N_TOK * ROW_TILE, LANES), F32)],
        out_shape=out_shape,
        compiler_params=_params("arbitrary", "arbitrary"),
        name="moe_scatter",
    )(idx.reshape(ns, N_EXPERTS, 1, CAP), ye_tiles, y, mod3)


def _rope_tables(n_tokens):
    t = np.arange(n_tokens)
    row = (t // GRID_W).astype(np.float32)
    col = (t % GRID_W).astype(np.float32)
    pairs = HEAD_DIM // 4
    inv_freq = ROPE_THETA ** (-jnp.arange(pairs, dtype=F32) / pairs)
    ang = jnp.concatenate([row[:, None] * inv_freq, col[:, None] * inv_freq], axis=-1)
    cos, sin = jnp.cos(ang), jnp.sin(ang)
    reps = MXU_DIM // HEAD_DIM
    return (jnp.tile(jnp.concatenate([cos, cos], axis=-1), (1, reps)),
            jnp.tile(jnp.concatenate([-sin, sin], axis=-1), (1, reps)))


def _dup_cache(cache):
    b, s, hk, hd = cache.shape
    return jnp.broadcast_to(cache[:, :, :, None, :], (b, s, hk, 2, hd)).reshape(b, s, 2 * hk * hd).astype(BF16)


def kernel(x_prompt, x_sample, cache_attn_k, cache_attn_v, cache_na_k, cache_na_v, c, c_ctx,
           norm1_g, norm2_g, w_ada, b_ada, attn_w_qkv, attn_q_gain, attn_k_gain, attn_w_o,
           na_w_qkv, na_q_gain, na_k_gain, na_rpb, na_w_o,
           moe_w_router, moe_w_gate, moe_w_up, moe_w_down):
    bc, tc, _ = x_prompt.shape
    bl, tl, _ = x_sample.shape
    depth = w_ada.shape[0]
    assert bc * tc == N_TOK and bl * tl == N_TOK and 1 + bl <= N_MOD_ROWS
    assert tl % TM == 0 and tl % FIN_TOK == 0

    cond = jnp.zeros((N_MOD_ROWS, D_MODEL), F32).at[0].set(c_ctx).at[1:1 + bl].set(c)
    mod3 = _ada(cond, w_ada, b_ada).reshape(depth * N_MOD_ROWS * 6, 1, D_MODEL)

    seg = jnp.asarray(np.kron(np.eye(MXU_DIM // HEAD_DIM), np.ones((HEAD_DIM, HEAD_DIM))) / HEAD_DIM, BF16)
    tri = jnp.asarray(np.triu(np.ones((MXU_DIM, MXU_DIM)), k=1), BF16)
    rope_tables = _rope_tables(tl)

    x_ctx = (x_prompt.reshape(N_TOK, D_MODEL), 0)
    x_lat = (x_sample.reshape(N_TOK, D_MODEL), 0)
    new_k, new_v = [], []
    for i in range(depth):
        j = i // 2
        gqa = i % 2 == 0
        if gqa:
            w_qkv, q_gain, k_gain, w_o = attn_w_qkv[j], attn_q_gain[j], attn_k_gain[j], attn_w_o[j]
            kv_heads = N_KV_HEADS
        else:
            w_qkv, q_gain, k_gain, w_o = na_w_qkv[j], na_q_gain[j], na_k_gain[j], na_w_o[j]
            kv_heads = N_HEADS
        wr_hi, wr_lo = _split_bf16(moe_w_router[i].T)

        w_qkv_b = w_qkv.astype(BF16)
        qkv_args = (norm1_g[i], mod3, i, tl, w_qkv_b, q_gain, k_gain, seg, kv_heads * HEAD_DIM)
        q_c, kb_c, vb_c, kf, vf = _qkv(x_ctx, True, *qkv_args, None)
        q_l, kb_l, vb_l = _qkv(x_lat, False, *qkv_args, rope_tables if gqa else None)
        o_ctx = _attention(*(a.reshape(bc, tc, a.shape[1]) for a in (q_c, kb_c, vb_c)), tq=tc, nb=2)
        q_l, kb_l, vb_l = (a.reshape(bl, tl, a.shape[1]) for a in (q_l, kb_l, vb_l))
        if gqa:
            o_lat = _attention(q_l, kb_l, vb_l,
                               _dup_cache(cache_attn_k[:, j]), _dup_cache(cache_attn_v[:, j]), tq=512)
        else:
            past = cache_na_k.shape[2]
            o_lat = _na_attention(q_l, kb_l, vb_l,
                                  cache_na_k[:, j].reshape(bl, past, D_MODEL).astype(BF16),
                                  cache_na_v[:, j].reshape(bl, past, D_MODEL).astype(BF16),
                                  _na_bias_table(na_rpb[j]))
        new_k.append(kf.reshape(bc, 1, tc, kv_heads, HEAD_DIM))
        new_v.append(vf.reshape(bc, 1, tc, kv_heads, HEAD_DIM))

        y, h2, logits_t = _post_attention(o_ctx.reshape(N_TOK, D_MODEL), o_lat.reshape(N_TOK, D_MODEL),
                                          x_ctx, x_lat, w_o.astype(BF16), norm2_g[i], mod3, i, tl,
                                          wr_hi, wr_lo)
        idx, gc = _plan(logits_t, tri)
        xe = _gather(idx, h2.reshape(N_STREAMS, N_TOK * ROW_TILE, LANES))
        ye = _ffn(xe, gc, moe_w_gate, moe_w_up, moe_w_down, i)
        y = _scatter(idx, ye, y, mod3, i, tl, split=(i == depth - 1))
        x_ctx, x_lat = (y, 0), (y, CTX_TILES)

    y_ctx, y_lat = y
    return (y_ctx.reshape(bc, tc, D_MODEL), y_lat.reshape(bl, tl, D_MODEL),
            jnp.concatenate(new_k[0::2], axis=1), jnp.concatenate(new_v[0::2], axis=1),
            jnp.concatenate(new_k[1::2], axis=1), jnp.concatenate(new_v[1::2], axis=1))
```

```python
import functools

import jax
import jax.numpy as jnp
import numpy as np
from jax import lax
from jax.experimental import pallas as pl
from jax.experimental.pallas import tpu as pltpu

F32 = jnp.float32
BF16 = jnp.bfloat16
I32 = jnp.int32

D_MODEL = 1024
N_HEADS = 16
N_KV_HEADS = 4
HEAD_DIM = 64
GRID_W = 64
WIN_R = 8
WIN_C = 16
N_EXPERTS = 16
EXPERT_FF = 2048
ROPE_THETA = 10000.0
RMS_EPS = 1e-6
NEG_INF = -1e30
F32_MIN_NORMAL_BITS = 0x00800000
F32_VALUE_BITS = 31
REFINE_STEPS = 64
LOG2_E = 1.4426950408889634
Q_SCALE = HEAD_DIM ** -0.5 * LOG2_E

LANES = 128
SUBLANES = 8
MXU_DIM = 256
VMEM_LIMIT = 56 * 1024 * 1024

N_STREAMS = 2
N_TOK = 8192
CAP = 2 * N_TOK // N_EXPERTS
TM = 512
N_MOD_ROWS = 16
ROW_TILE = D_MODEL // LANES
GROUP = MXU_DIM
N_GROUPS = N_TOK // GROUP
FIN_TOK = 512
N_FIN = N_TOK // FIN_TOK
GATHER_UNROLL = 32
SCATTER_UNROLL = 16
FF_TILE = 512
NA_QBLK = 256
NA_QROWS = NA_QBLK // GRID_W
NA_BAND = WIN_R + NA_QROWS
NA_MAX_SHIFT = WIN_R - 2 + NA_QROWS
NA_TILES = NA_MAX_SHIFT + 1
N_REL_R = 2 * WIN_R - 1
N_REL_C = 2 * WIN_C - 1


def _params(*sem):
    return pltpu.CompilerParams(dimension_semantics=sem, vmem_limit_bytes=VMEM_LIMIT)


def _dot(a, b):
    return jnp.dot(a, b, preferred_element_type=F32)


def _dot_nt(a, b):
    return lax.dot_general(a, b, (((1,), (1,)), ((), ())), preferred_element_type=F32)


def _split_bf16(x):
    hi = x.astype(BF16)
    lo = (x - hi.astype(F32)).astype(BF16)
    return hi, lo


def _silu(x):
    return x * (1.0 / (1.0 + jnp.exp(-x)))


def _ada_kernel(cond_ref, w_ref, b_ref, out_ref):
    sx = _silu(cond_ref[...])
    xh, xl = _split_bf16(sx)
    wh, wl = _split_bf16(w_ref[0])
    out_ref[0] = _dot(xh, wh) + _dot(xl, wh) + _dot(xh, wl) + b_ref[0]


def _ada(cond, w_ada, b_ada):
    depth = w_ada.shape[0]
    tn = 1024
    n_out = w_ada.shape[2]
    return pl.pallas_call(
        _ada_kernel,
        grid=(depth, n_out // tn),
        in_specs=[
            pl.BlockSpec((N_MOD_ROWS, D_MODEL), lambda l, n: (0, 0)),
            pl.BlockSpec((1, D_MODEL, tn), lambda l, n: (l, 0, n)),
            pl.BlockSpec((1, 1, tn), lambda l, n: (l, 0, n)),
        ],
        out_specs=pl.BlockSpec((1, N_MOD_ROWS, tn), lambda l, n: (l, 0, n)),
        out_shape=jax.ShapeDtypeStruct((depth, N_MOD_ROWS, n_out), F32),
        compiler_params=_params("arbitrary", "arbitrary"),
        name="ada",
    )(cond, w_ada, b_ada.reshape(depth, 1, n_out))


def _mod_row(tile, tile_rows, lat_seq):
    ctx_tiles = N_TOK // tile_rows
    return jnp.where(tile < ctx_tiles, 0, 1 + (tile - ctx_tiles) // (lat_seq // tile_rows))


def _mod_spec(layer, which, lat_seq, tile_of=lambda i: i):
    def index(i):
        return ((layer * N_MOD_ROWS + _mod_row(tile_of(i), TM, lat_seq)) * 6 + which, 0, 0)
    return pl.BlockSpec((1, 1, D_MODEL), index)


CTX_TILES = N_TOK // TM


def _x_specs(x_ctx_block0, x_lat_block0, tile_of=lambda i: i):
    return [
        pl.BlockSpec((TM, D_MODEL), lambda i: (x_ctx_block0 + jnp.minimum(tile_of(i), CTX_TILES - 1), 0)),
        pl.BlockSpec((TM, D_MODEL), lambda i: (x_lat_block0 + jnp.maximum(tile_of(i) - CTX_TILES, 0), 0)),
    ]


def _head_norm(z, seg, gain):
    ms = _dot((z * z).astype(BF16), seg)
    return z * lax.rsqrt(ms + RMS_EPS) * gain


def _rope(z, cos_t, sin_t):
    lane = lax.broadcasted_iota(I32, z.shape, 1)
    first = (lane & 32) == 0
    n = z.shape[1]
    partner = jnp.where(first, pltpu.roll(z, n - 32, axis=1), pltpu.roll(z, 32, axis=1))
    return z * cos_t + partner * sin_t


def _dup_heads(z):
    outs = []
    for b in range(z.shape[1] // LANES):
        x = z[:, b * LANES:(b + 1) * LANES]
        xr = pltpu.roll(x, HEAD_DIM, axis=1)
        lo = lax.broadcasted_iota(I32, x.shape, 1) < HEAD_DIM
        outs.append(jnp.where(lo, x, xr))
        outs.append(jnp.where(lo, xr, x))
    return jnp.concatenate(outs, axis=1)


def _store_head_rows(dst_ref, c0, z, n_heads):
    for b in range(z.shape[1] // LANES):
        pair = z[:, b * LANES:(b + 1) * LANES]
        swapped = pltpu.roll(pair, HEAD_DIM, axis=1)
        for hh, val in enumerate((pair, swapped)):
            head = c0 // HEAD_DIM + 2 * b + hh
            dst_ref[pl.ds(head, TM, stride=n_heads), :] = val[:, :HEAD_DIM]


def _qkv_kernel(*refs, kv_width, rope, emit_heads):
    it = iter(refs)
    x_ref, g_ref, shift_ref, scale_ref, w_ref, qg_ref, kg_ref, seg_ref = (next(it) for _ in range(8))
    cos_ref = sin_ref = None
    if rope:
        cos_ref, sin_ref = next(it), next(it)
    q_ref, kb_ref, vb_ref = (next(it) for _ in range(3))
    kh_ref = vh_ref = None
    if emit_heads:
        kh_ref, vh_ref = next(it), next(it)
    dup = kv_width == N_KV_HEADS * HEAD_DIM
    n_heads = kv_width // HEAD_DIM

    x = x_ref[...]
    ms = jnp.mean(x * x, axis=-1, keepdims=True)
    h = x * lax.rsqrt(ms + RMS_EPS) * (g_ref[...] * (1.0 + scale_ref[0])) + shift_ref[0]
    hb = h.astype(BF16)
    seg = seg_ref[...]
    if rope:
        cos_t, sin_t = cos_ref[...], sin_ref[...]

    q_width = N_HEADS * HEAD_DIM

    def finish_q(c0, raw):
        z = _head_norm(raw, seg, qg_ref[...])
        if rope:
            z = _rope(z, cos_t, sin_t)
        q_ref[:, c0:c0 + MXU_DIM] = z.astype(BF16)

    def finish_k(c0, raw):
        z = _head_norm(raw, seg, kg_ref[...])
        if emit_heads:
            _store_head_rows(kh_ref, c0, z, n_heads)
        if rope:
            z = _rope(z, cos_t, sin_t)
        if dup:
            kb_ref[:, 2 * c0:2 * c0 + 2 * MXU_DIM] = _dup_heads(z).astype(BF16)
        else:
            kb_ref[:, c0:c0 + MXU_DIM] = z.astype(BF16)

    def finish_v(c0, raw):
        if emit_heads:
            _store_head_rows(vh_ref, c0, raw, n_heads)
        if dup:
            vb_ref[:, 2 * c0:2 * c0 + 2 * MXU_DIM] = _dup_heads(raw).astype(BF16)
        else:
            vb_ref[:, c0:c0 + MXU_DIM] = raw.astype(BF16)

    q_chunks = [(finish_q, c0, c0) for c0 in range(0, q_width, MXU_DIM)]
    k_chunks = [(finish_k, c0, q_width + c0) for c0 in range(0, kv_width, MXU_DIM)]
    v_chunks = [(finish_v, c0, q_width + kv_width + c0) for c0 in range(0, kv_width, MXU_DIM)]
    normed = q_chunks + k_chunks
    chunks = []
    for i, chunk in enumerate(normed):
        chunks.append(chunk)
        if i * len(v_chunks) // len(normed) != (i + 1) * len(v_chunks) // len(normed):
            chunks.append(v_chunks[i * len(v_chunks) // len(normed)])

    project = lambda col: _dot(hb, w_ref[:, col:col + MXU_DIM])
    raw = project(chunks[0][2])
    for i, (finish, c0, _) in enumerate(chunks):
        nxt = project(chunks[i + 1][2]) if i + 1 < len(chunks) else None
        finish(c0, raw)
        raw = nxt

def _qkv(x_src, is_ctx, norm_g, mod3, layer, lat_seq, w_bf16, q_gain, k_gain, seg, kv_width, rope_tables):
    n_w = w_bf16.shape[1]
    kvb_width = 2 * kv_width if kv_width == N_KV_HEADS * HEAD_DIM else kv_width
    n_heads = kv_width // HEAD_DIM
    tile_of = (lambda i: i) if is_ctx else (lambda i: i + CTX_TILES)
    tile = lambda w: pl.BlockSpec((TM, w), lambda i: (i, 0))
    const = lambda shape: pl.BlockSpec(shape, lambda i: (0,) * len(shape))
    in_specs = [
        pl.BlockSpec((TM, D_MODEL), lambda i: (x_src[1] + i, 0)),
        const((1, D_MODEL)),
        _mod_spec(layer, 0, lat_seq, tile_of),
        _mod_spec(layer, 1, lat_seq, tile_of),
        const((D_MODEL, n_w)),
        const((1, MXU_DIM)),
        const((1, MXU_DIM)),
        const((MXU_DIM, MXU_DIM)),
    ]
    args = [x_src[0], norm_g.reshape(1, D_MODEL), mod3, mod3, w_bf16,
            jnp.tile(q_gain * Q_SCALE, MXU_DIM // HEAD_DIM).reshape(1, MXU_DIM),
            jnp.tile(k_gain, MXU_DIM // HEAD_DIM).reshape(1, MXU_DIM), seg]
    if rope_tables is not None:
        seq_tiles = lat_seq // TM
        in_specs += [pl.BlockSpec((TM, MXU_DIM), lambda i: (i % seq_tiles, 0))] * 2
        args += list(rope_tables)
    out_specs = [tile(D_MODEL), tile(kvb_width), tile(kvb_width)]
    out_shape = [jax.ShapeDtypeStruct((N_TOK, D_MODEL), BF16),
                 jax.ShapeDtypeStruct((N_TOK, kvb_width), BF16),
                 jax.ShapeDtypeStruct((N_TOK, kvb_width), BF16)]
    if is_ctx:
        out_specs += [pl.BlockSpec((TM * n_heads, HEAD_DIM), lambda i: (i, 0))] * 2
        out_shape += [jax.ShapeDtypeStruct((N_TOK * n_heads, HEAD_DIM), F32)] * 2
    return pl.pallas_call(
        functools.partial(_qkv_kernel, kv_width=kv_width, rope=rope_tables is not None, emit_heads=is_ctx),
        grid=(N_TOK // TM,),
        in_specs=in_specs,
        out_specs=out_specs,
        out_shape=out_shape,
        compiler_params=_params("arbitrary"),
        name="qkv",
    )(*args)


def _two_head_rows(qj):
    lo = lax.broadcasted_iota(I32, qj.shape, 1) < HEAD_DIM
    zero = jnp.zeros_like(qj)
    return jnp.concatenate([jnp.where(lo, qj, zero), jnp.where(lo, zero, qj)], axis=0)


def _merge_two_heads(r):
    tq = r.shape[0] // 2
    lo = lax.broadcasted_iota(I32, (tq, LANES), 1) < HEAD_DIM
    return jnp.where(lo, r[:tq], r[tq:])


def _attn_kernel(*refs, n_kv_blocks, has_ctx):
    if has_ctx:
        q_ref, k_ref, v_ref, ck_ref, cv_ref, o_ref = refs
    else:
        q_ref, k_ref, v_ref, o_ref = refs
    n_q_blocks = D_MODEL // LANES
    key_lanes = lambda j: slice(((j * n_kv_blocks) // n_q_blocks) * LANES,
                                ((j * n_kv_blocks) // n_q_blocks + 1) * LANES)

    items = [(b, j) for b in range(q_ref.shape[0]) for j in range(n_q_blocks)]

    def scores(b, j):
        q2 = _two_head_rows(q_ref[b, :, j * LANES:(j + 1) * LANES])
        s = _dot_nt(q2, k_ref[b, :, key_lanes(j)])
        sc = _dot_nt(q2, ck_ref[b, :, key_lanes(j)]) if has_ctx else None
        return s, sc

    nxt = scores(*items[0])
    for i, (b, j) in enumerate(items):
        (s, sc), ksl = nxt, key_lanes(j)
        if i + 1 < len(items):
            nxt = scores(*items[i + 1])
        m = jnp.max(s, axis=-1, keepdims=True)
        if has_ctx:
            m = jnp.maximum(m, jnp.max(sc, axis=-1, keepdims=True))
        p = jnp.exp2(s - m)
        l = jnp.sum(p, axis=-1, keepdims=True)
        r = _dot(p.astype(BF16), v_ref[b, :, ksl])
        if has_ctx:
            pc = jnp.exp2(sc - m)
            l = l + jnp.sum(pc, axis=-1, keepdims=True)
            r = r + _dot(pc.astype(BF16), cv_ref[b, :, ksl])
        r = r / l
        o_ref[b, :, j * LANES:(j + 1) * LANES] = _merge_two_heads(r).astype(BF16)


def _attention(q, k, v, ctx_k=None, ctx_v=None, tq=256, nb=1):
    n_batch, t, _ = q.shape
    s, w = k.shape[1], k.shape[2]
    has_ctx = ctx_k is not None
    in_specs = [
        pl.BlockSpec((nb, tq, D_MODEL), lambda bi, qi: (bi, qi, 0)),
        pl.BlockSpec((nb, s, w), lambda bi, qi: (bi, 0, 0)),
        pl.BlockSpec((nb, s, w), lambda bi, qi: (bi, 0, 0)),
    ]
    args = [q, k, v]
    if has_ctx:
        sc = ctx_k.shape[1]
        in_specs += [pl.BlockSpec((nb, sc, w), lambda bi, qi: (bi, 0, 0))] * 2
        args += [ctx_k, ctx_v]
    return pl.pallas_call(
        functools.partial(_attn_kernel, n_kv_blocks=w // LANES, has_ctx=has_ctx),
        grid=(n_batch // nb, t // tq),
        in_specs=in_specs,
        out_specs=pl.BlockSpec((nb, tq, D_MODEL), lambda bi, qi: (bi, qi, 0)),
        out_shape=jax.ShapeDtypeStruct((n_batch, t, D_MODEL), BF16),
        compiler_params=_params("arbitrary", "arbitrary"),
        name="attn",
    )(*args)


def _na_kernel(q_ref, k_ref, v_ref, ck_ref, cv_ref, bias_ref, o_ref, *, rows):
    blk = pl.program_id(1)
    q_row0 = NA_QROWS * blk
    band_start = jnp.minimum(jnp.clip(q_row0 - WIN_R // 2, 0, rows - WIN_R), rows - NA_BAND)
    n_band = NA_BAND * GRID_W
    band = pl.ds(pl.multiple_of(band_start * GRID_W, LANES), n_band)
    shape = (NA_QBLK, n_band)
    q_r = q_row0 + lax.broadcasted_iota(I32, shape, 0) // GRID_W
    k_r = band_start + lax.broadcasted_iota(I32, shape, 1) // GRID_W
    q_rs = jnp.clip(q_r - WIN_R // 2, 0, rows - WIN_R)
    row_ok1 = (k_r >= q_rs) & (k_r < q_rs + WIN_R)
    row_ok = jnp.concatenate([row_ok1, row_ok1], axis=0)
    shift0 = (band_start - q_row0 + NA_MAX_SHIFT) // 2
    n_q_blocks = D_MODEL // LANES

    def scores(j):
        sl = slice(j * LANES, (j + 1) * LANES)
        q2 = _two_head_rows(q_ref[0, :, sl])
        return _dot_nt(q2, k_ref[0, band, sl]), _dot_nt(q2, ck_ref[0, :, sl])

    nxt = scores(0)
    for j in range(n_q_blocks):
        sl = slice(j * LANES, (j + 1) * LANES)
        s, sc = nxt
        if j + 1 < n_q_blocks:
            nxt = scores(j + 1)
        bias = jnp.concatenate(
            [jnp.concatenate([bias_ref[2 * j + hh, shift0 + m - u] for m in range(NA_BAND // 2)], axis=1)
             for hh in range(2) for u in range(NA_QROWS // 2)], axis=0)
        s = jnp.where(row_ok, s + bias, NEG_INF)
        m = jnp.maximum(jnp.max(s, axis=-1, keepdims=True), jnp.max(sc, axis=-1, keepdims=True))
        p = jnp.exp2(s - m)
        pc = jnp.exp2(sc - m)
        l = jnp.sum(p, axis=-1, keepdims=True) + jnp.sum(pc, axis=-1, keepdims=True)
        r = _dot(p.astype(BF16), v_ref[0, band, sl]) + _dot(pc.astype(BF16), cv_ref[0, :, sl])
        r = r / l
        o_ref[0, :, sl] = _merge_two_heads(r).astype(BF16)


def _na_bias_kernel(rpb_ref, out_ref, tz_ref):
    h = pl.program_id(0)
    shape = (GRID_W, LANES)
    q_col = lax.broadcasted_iota(I32, shape, 0)
    lane = lax.broadcasted_iota(I32, shape, 1)
    k_col = lane & (GRID_W - 1)
    col_start = jnp.clip(q_col - WIN_C // 2, 0, GRID_W - WIN_C)
    col_ok = (k_col >= col_start) & (k_col < col_start + WIN_C)
    rel_c = jnp.clip(k_col - q_col + WIN_C - 1, 0, N_REL_C - 1)
    for a in range(N_REL_R):
        acc = jnp.zeros(shape, F32)
        for b in range(N_REL_C):
            acc = jnp.where(rel_c == b, rpb_ref[(h * N_REL_R + a) * N_REL_C + b], acc)
        tz_ref[a] = jnp.where(col_ok, acc * LOG2_E, NEG_INF)
    left = lane < GRID_W
    for dd in range(NA_TILES):
        for qr in range(2):
            rel = [min(max(2 * dd - NA_MAX_SHIFT + kr - qr + WIN_R - 1, 0), N_REL_R - 1) for kr in range(2)]
            out_ref[0, dd, qr * GRID_W:(qr + 1) * GRID_W, :] = jnp.where(left, tz_ref[rel[0]], tz_ref[rel[1]])


def _na_bias_table(rpb):
    return pl.pallas_call(
        _na_bias_kernel,
        grid=(N_HEADS,),
        in_specs=[pl.BlockSpec(memory_space=pltpu.SMEM)],
        out_specs=pl.BlockSpec((1, NA_TILES, LANES, LANES), lambda h: (h, 0, 0, 0)),
        out_shape=jax.ShapeDtypeStruct((N_HEADS, NA_TILES, LANES, LANES), F32),
        scratch_shapes=[pltpu.VMEM((N_REL_R, GRID_W, LANES), F32)],
        compiler_params=_params("arbitrary"),
        name="na_bias",
    )(rpb.reshape(-1))


def _na_attention(q, k, v, ctx_k, ctx_v, bias_tbl):
    n_batch, t, _ = q.shape
    sc = ctx_k.shape[1]
    rows = t // GRID_W
    full = lambda n: pl.BlockSpec((1, n, D_MODEL), lambda bi, qi: (bi, 0, 0))
    return pl.pallas_call(
        functools.partial(_na_kernel, rows=rows),
        grid=(n_batch, t // NA_QBLK),
        in_specs=[
            pl.BlockSpec((1, NA_QBLK, D_MODEL), lambda bi, qi: (bi, qi, 0)),
            full(t), full(t), full(sc), full(sc),
            pl.BlockSpec(bias_tbl.shape, lambda bi, qi: (0, 0, 0, 0)),
        ],
        out_specs=pl.BlockSpec((1, NA_QBLK, D_MODEL), lambda bi, qi: (bi, qi, 0)),
        out_shape=jax.ShapeDtypeStruct((n_batch, t, D_MODEL), BF16),
        compiler_params=_params("arbitrary", "arbitrary"),
        name="na_attn",
    )(q, k, v, ctx_k, ctx_v, bias_tbl)


def _post_kernel(oc_ref, ol_ref, xc_ref, xl_ref, wo_ref, gate_ref, g_ref, shift_ref, scale_ref,
                 wrh_ref, wrl_ref, y_ref, h_ref, lg_ref):
    is_ctx = pl.program_id(0) < CTX_TILES
    o = jnp.where(is_ctx, oc_ref[...], ol_ref[...])
    x = jnp.where(is_ctx, xc_ref[...], xl_ref[...])
    half = TM // 2
    proj = [_dot(o[a * half:(a + 1) * half], wo_ref[...]) for a in range(2)]
    gain = g_ref[...] * (1.0 + scale_ref[0])
    for a in range(2):
        rows = slice(a * half, (a + 1) * half)
        y = x[rows] + gate_ref[0] * proj[a]
        y_ref[rows, :] = y
        ms = jnp.mean(y * y, axis=-1, keepdims=True)
        h = y * lax.rsqrt(ms + RMS_EPS) * gain + shift_ref[0]
        for k in range(ROW_TILE):
            h_ref[pl.ds(a * half * ROW_TILE + k, half, stride=ROW_TILE), :] = h[:, k * LANES:(k + 1) * LANES]
        hh, hl = _split_bf16(h)
        lg_ref[:, rows] = _dot_nt(wrh_ref[...], hh) + _dot_nt(wrh_ref[...], hl) + _dot_nt(wrl_ref[...], hh)


def _post_attention(o_ctx, o_lat, x_ctx, x_lat, wo_bf16, norm_g, mod3, layer, lat_seq, wr_hi, wr_lo):
    n_tok = N_STREAMS * N_TOK
    tile = lambda w: pl.BlockSpec((TM, w), lambda i: (i, 0))
    const = lambda shape: pl.BlockSpec(shape, lambda i: (0,) * len(shape))
    return pl.pallas_call(
        _post_kernel,
        grid=(n_tok // TM,),
        in_specs=_x_specs(0, 0) + _x_specs(x_ctx[1], x_lat[1]) + [
            const((D_MODEL, D_MODEL)),
            _mod_spec(layer, 2, lat_seq),
            const((1, D_MODEL)),
            _mod_spec(layer, 3, lat_seq),
            _mod_spec(layer, 4, lat_seq),
            const((N_EXPERTS, D_MODEL)), const((N_EXPERTS, D_MODEL)),
        ],
        out_specs=[tile(D_MODEL), pl.BlockSpec((TM * ROW_TILE, LANES), lambda i: (i, 0)),
                   pl.BlockSpec((N_EXPERTS, TM), lambda i: (0, i))],
        out_shape=[jax.ShapeDtypeStruct((n_tok, D_MODEL), F32),
                   jax.ShapeDtypeStruct((n_tok * ROW_TILE, LANES), F32),
                   jax.ShapeDtypeStruct((N_EXPERTS, n_tok), F32)],
        compiler_params=_params("arbitrary"),
        name="post_attn",
    )(o_ctx, o_lat, x_ctx[0], x_lat[0], wo_bf16, mod3, norm_g.reshape(1, D_MODEL), mod3, mod3, wr_hi, wr_lo)


def _group_prefix(mask, tri):
    local, offs = [], []
    off = jnp.zeros((mask.shape[0], 1), F32)
    for g in range(mask.shape[1] // GROUP):
        xg = mask[:, g * GROUP:(g + 1) * GROUP]
        offs.append(off)
        local.append(_dot(xg.astype(BF16), tri))
        off = off + jnp.sum(xg, axis=1, keepdims=True)
    offs.append(off)
    return local, offs


def _split3_bf16(x):
    hi = x.astype(BF16)
    r1 = x - hi.astype(F32)
    mid = r1.astype(BF16)
    lo = (r1 - mid.astype(F32)).astype(BF16)
    return hi, mid, lo


def _plan_kernel(lg_ref, tri_ref, idx_ref, gc_ref, cnt_ref, affg_ref):
    lg = lg_ref[...]
    ex = jnp.exp(lg - jnp.max(lg, axis=0, keepdims=True))
    aff = ex / jnp.sum(ex, axis=0, keepdims=True)

    def count_ge(v):
        return jnp.sum(jnp.where(aff >= v, 1.0, 0.0), axis=1, keepdims=True)

    def search(i, thr):
        cand = thr | jnp.left_shift(jnp.int32(1), F32_VALUE_BITS - 1 - i)
        ok = (count_ge(lax.bitcast_convert_type(cand, F32)) >= CAP) & (cand >= F32_MIN_NORMAL_BITS)
        return jnp.where(ok, cand, thr)

    thr = lax.fori_loop(0, F32_VALUE_BITS, search, jnp.zeros((N_EXPERTS, 1), I32))
    lo = lax.bitcast_convert_type(thr, F32)
    hi = lax.bitcast_convert_type(jnp.maximum(thr + 1, F32_MIN_NORMAL_BITS), F32)

    midpoint = lambda lo, hi: lo + (hi - lo) * 0.5

    def splittable(state):
        i, lo, hi = state
        mid = midpoint(lo, hi)
        inside = jnp.max(jnp.where((mid > lo) & (mid < hi), 1.0, 0.0))
        return (i < REFINE_STEPS) & (inside > 0.0)

    def refine(state):
        i, lo, hi = state
        mid = midpoint(lo, hi)
        ok = count_ge(mid) >= CAP
        return i + 1, jnp.where(ok, mid, lo), jnp.where(ok, hi, mid)

    _, lo, hi = lax.while_loop(splittable, refine, (jnp.int32(0), lo, hi))
    tri = tri_ref[...]
    gt = aff >= hi
    eq = jnp.where((aff >= lo) & (aff < hi), 1.0, 0.0)
    need = CAP - jnp.sum(jnp.where(gt, 1.0, 0.0), axis=1, keepdims=True)
    eq_local, eq_offs = _group_prefix(eq, tri)
    eq_rank = jnp.concatenate([eq_local[g] + eq_offs[g] for g in range(N_GROUPS)], axis=1)
    sel = jnp.where(gt | ((eq > 0.0) & (eq_rank < need)), 1.0, 0.0)
    sel_local, offs = _group_prefix(sel, tri)
    for g in range(N_GROUPS):
        sl = slice(g * GROUP, (g + 1) * GROUP)
        cnt_ref[g] = sel_local[g] + sel[:, sl]
        affg_ref[g] = aff[:, sl]

    lane = lax.broadcasted_iota(I32, (N_EXPERTS, LANES), 1)
    never = jnp.full((N_EXPERTS, LANES), 2.0 * N_TOK, F32)
    grp_lo, grp_hi = never, never
    for g in range(N_GROUPS):
        grp_lo = jnp.where(lane == g, offs[g], grp_lo)
        grp_hi = jnp.where(lane == g, offs[g + 1], grp_hi)
    row = lax.broadcasted_iota(I32, (CAP, LANES), 0).astype(F32)
    in_group_lane = lax.broadcasted_iota(I32, (CAP, GROUP), 1).astype(F32)
    zpad = jnp.zeros((LANES - N_GROUPS, GROUP), BF16)
    for e in range(N_EXPERTS):
        lo_row, hi_row = grp_lo[e:e + 1, :], grp_hi[e:e + 1, :]
        in_grp = (lo_row <= row) & (row < hi_row)
        onehot = jnp.where(in_grp, 1.0, 0.0).astype(BF16)
        counts = _dot(onehot, jnp.concatenate([cnt_ref[:, e, :].astype(BF16), zpad], axis=0))
        rank = row[:, 0:1] - jnp.sum(jnp.where(in_grp, lo_row, 0.0), axis=1, keepdims=True)
        local = jnp.sum(jnp.where(counts <= rank, 1.0, 0.0), axis=1, keepdims=True)
        grp = jnp.sum(jnp.where(hi_row <= row, 1.0, 0.0), axis=1, keepdims=True)
        tok = grp * GROUP + local
        aff_rows = sum(_dot(onehot, jnp.concatenate([part, zpad], axis=0))
                       for part in _split3_bf16(affg_ref[:, e, :]))
        gate = jnp.sum(jnp.where(in_group_lane == local, aff_rows, 0.0), axis=1, keepdims=True)
        gc_ref[0, e] = jnp.broadcast_to(gate, (CAP, LANES))
        tok_b = jnp.broadcast_to(tok, (CAP, LANES))
        tok_row = jnp.concatenate([tok_b[t * LANES:(t + 1) * LANES, :].T[0:1, :]
                                   for t in range(CAP // LANES)], axis=1)
        idx_ref[0, e:e + 1, :] = tok_row.astype(I32)


def _plan(logits_t, tri):
    ns = N_STREAMS
    return pl.pallas_call(
        _plan_kernel,
        grid=(ns,),
        in_specs=[pl.BlockSpec((N_EXPERTS, N_TOK), lambda s: (0, s)),
                  pl.BlockSpec((MXU_DIM, MXU_DIM), lambda s: (0, 0))],
        out_specs=[pl.BlockSpec((1, N_EXPERTS, CAP), lambda s: (s, 0, 0)),
                   pl.BlockSpec((1, N_EXPERTS, CAP, LANES), lambda s: (s, 0, 0, 0))],
        out_shape=[jax.ShapeDtypeStruct((ns, N_EXPERTS, CAP), I32),
                   jax.ShapeDtypeStruct((ns, N_EXPERTS, CAP, LANES), F32)],
        scratch_shapes=[pltpu.VMEM((N_GROUPS, N_EXPERTS, GROUP), F32),
                        pltpu.VMEM((N_GROUPS, N_EXPERTS, GROUP), F32)],
        compiler_params=_params("arbitrary"),
        name="plan",
    )(logits_t, tri)


def _tile_rows(r):
    return pl.ds(pl.multiple_of(r * ROW_TILE, ROW_TILE), ROW_TILE)


def _gather_kernel(idx_ref, h_ref, xe_ref, stage_ref):
    sub = SUBLANES
    groups = GATHER_UNROLL // sub

    def move(g, carry):
        for u in range(GATHER_UNROLL):
            tile = h_ref[0, _tile_rows(idx_ref[0, 0, 0, g * GATHER_UNROLL + u]), :]
            stage_ref[g * groups + u // sub, pl.ds(u % sub, ROW_TILE, stride=sub), :] = tile
        return carry

    lax.fori_loop(0, CAP // GATHER_UNROLL, move, 0)
    for k in range(ROW_TILE):
        rows = stage_ref[:, k * sub:(k + 1) * sub, :].reshape(CAP, LANES)
        xe_ref[0, 0, :, k * LANES:(k + 1) * LANES] = rows.astype(BF16)


def _gather(idx, h2_tiles):
    ns = h2_tiles.shape[0]
    return pl.pallas_call(
        _gather_kernel,
        grid=(ns, N_EXPERTS),
        in_specs=[pl.BlockSpec((1, 1, 1, CAP), lambda s, e: (s, e, 0, 0), memory_space=pltpu.SMEM),
                  pl.BlockSpec((1, N_TOK * ROW_TILE, LANES), lambda s, e: (s, 0, 0),
                               pipeline_mode=pl.Buffered(1))],
        out_specs=pl.BlockSpec((1, 1, CAP, D_MODEL), lambda s, e: (s, e, 0, 0)),
        scratch_shapes=[pltpu.VMEM((CAP // SUBLANES, ROW_TILE * SUBLANES, LANES), F32)],
        out_shape=jax.ShapeDtypeStruct((ns, N_EXPERTS, CAP, D_MODEL), BF16),
        compiler_params=_params("arbitrary", "arbitrary"),
        name="moe_gather",
    )(idx.reshape(ns, N_EXPERTS, 1, CAP), h2_tiles)


def _ffn_kernel(x_ref, wg_ref, wu_ref, wd_ref, gc_ref, ye_ref, acc_ref, *, row_tile):
    f = pl.program_id(1)
    last = pl.num_programs(1) - 1
    ns = x_ref.shape[0]
    tiles = [(s, r0) for s in range(ns) for r0 in range(0, CAP, row_tile)]

    def for_each_tile(emit):
        wg = wg_ref[0, 0].astype(BF16)
        wu = wu_ref[0, 0].astype(BF16)
        wd = wd_ref[0, 0].astype(BF16)

        def gate_up(s, r0):
            x = x_ref[s, 0, r0:r0 + row_tile, :]
            return _dot(x, wg), _dot(x, wu)

        nxt = gate_up(*tiles[0])
        for i, (s, r0) in enumerate(tiles):
            gate, up = nxt
            if i + 1 < len(tiles):
                nxt = gate_up(*tiles[i + 1])
            emit(s, r0, _dot((_silu(gate) * up).astype(BF16), wd))

    def first(s, r0, part):
        acc_ref[s, r0:r0 + row_tile, :] = part

    def middle(s, r0, part):
        acc_ref[s, r0:r0 + row_tile, :] += part

    def final(s, r0, part):
        ye = (acc_ref[s, r0:r0 + row_tile, :] + part) * gc_ref[s, 0, r0:r0 + row_tile, 0:1]
        for k in range(ROW_TILE):
            ye_ref[s, 0, pl.ds(r0 * ROW_TILE + k, row_tile, stride=ROW_TILE), :] = (
                ye[:, k * LANES:(k + 1) * LANES])

    pl.when(f == 0)(lambda: for_each_tile(first))
    pl.when((f > 0) & (f < last))(lambda: for_each_tile(middle))
    pl.when(f == last)(lambda: for_each_tile(final))


def _ffn(xe, gc, w_gate, w_up, w_down, layer):
    ns = xe.shape[0]
    return pl.pallas_call(
        functools.partial(_ffn_kernel, row_tile=512),
        grid=(N_EXPERTS, EXPERT_FF // FF_TILE),
        in_specs=[
            pl.BlockSpec((ns, 1, CAP, D_MODEL), lambda e, f: (0, e, 0, 0)),
            pl.BlockSpec((1, 1, D_MODEL, FF_TILE), lambda e, f: (layer, e, 0, f)),
            pl.BlockSpec((1, 1, D_MODEL, FF_TILE), lambda e, f: (layer, e, 0, f)),
            pl.BlockSpec((1, 1, FF_TILE, D_MODEL), lambda e, f: (layer, e, f, 0)),
            pl.BlockSpec((ns, 1, CAP, LANES), lambda e, f: (0, e, 0, 0)),
        ],
        out_specs=pl.BlockSpec((ns, 1, CAP * ROW_TILE, LANES), lambda e, f: (0, e, 0, 0)),
        out_shape=jax.ShapeDtypeStruct((ns, N_EXPERTS, CAP * ROW_TILE, LANES), F32),
        scratch_shapes=[pltpu.VMEM((ns, CAP, D_MODEL), F32)],
        compiler_params=_params("arbitrary", "arbitrary"),
        name="moe_ffn",
    )(xe, w_gate, w_up, w_down, gc)


def _scatter_kernel(idx_ref, ye_ref, y_ref, gate_ref, *refs, split):
    out_refs, acc_ref = refs[:-1], refs[-1]
    s, j = pl.program_id(0), pl.program_id(1)

    @pl.when(j == 0)
    def _():
        acc_ref[...] = jnp.zeros_like(acc_ref)

    @pl.when(j < N_EXPERTS)
    def _():
        def add_rows(g, carry):
            r0 = g * SCATTER_UNROLL
            group = ye_ref.at[0, 0, pl.ds(pl.multiple_of(r0 * ROW_TILE, SCATTER_UNROLL * ROW_TILE),
                                          SCATTER_UNROLL * ROW_TILE), :]
            dst = [_tile_rows(idx_ref[0, 0, 0, r0 + u]) for u in range(SCATTER_UNROLL)]
            val = [acc_ref[dst[u], :] + group[u * ROW_TILE:(u + 1) * ROW_TILE, :] for u in range(SCATTER_UNROLL)]
            for u in range(SCATTER_UNROLL):
                acc_ref[dst[u], :] = val[u]
            return carry

        lax.fori_loop(0, CAP // SCATTER_UNROLL, add_rows, 0)

    @pl.when(j >= N_EXPERTS)
    def _():
        c = j - N_EXPERTS
        rows = FIN_TOK * ROW_TILE
        part = acc_ref.at[pl.ds(pl.multiple_of(c * rows, rows), rows), :]
        moe = jnp.concatenate([part[pl.ds(k, FIN_TOK, stride=ROW_TILE), :] for k in range(ROW_TILE)], axis=1)
        res = y_ref[...] + gate_ref[0] * moe
        if split:
            for si, out_ref in enumerate(out_refs):
                @pl.when(s == si)
                def _(out_ref=out_ref):
                    out_ref[...] = res
        else:
            out_refs[0][...] = res


def _scatter(idx, ye_tiles, y, mod3, layer, lat_seq, split):
    ns = ye_tiles.shape[0]
    fin = lambda j: jnp.maximum(j - N_EXPERTS, 0)

    def gate_index(s, j):
        row = _mod_row(s * N_FIN + fin(j), FIN_TOK, lat_seq)
        return ((layer * N_MOD_ROWS + row) * 6 + 5, 0, 0)

    chunk = (FIN_TOK, D_MODEL)
    if split:
        out_specs = [
            pl.BlockSpec(chunk, lambda s, j, si=si: (
                jnp.where(s == si, fin(j), jnp.where(s < si, 0, N_FIN - 1)), 0))
            for si in range(ns)]
        out_shape = [jax.ShapeDtypeStruct((N_TOK, D_MODEL), F32)] * ns
    else:
        out_specs = pl.BlockSpec(chunk, lambda s, j: (s * N_FIN + fin(j), 0))
        out_shape = jax.ShapeDtypeStruct(y.shape, F32)
    expert = lambda j: jnp.minimum(j, N_EXPERTS - 1)
    return pl.pallas_call(
        functools.partial(_scatter_kernel, split=split),
        grid=(ns, N_EXPERTS + N_FIN),
        in_specs=[
            pl.BlockSpec((1, 1, 1, CAP), lambda s, j: (s, expert(j), 0, 0), memory_space=pltpu.SMEM),
            pl.BlockSpec((1, 1, CAP * ROW_TILE, LANES), lambda s, j: (s, expert(j), 0, 0)),
            pl.BlockSpec(chunk, lambda s, j: (s * N_FIN + fin(j), 0)),
            pl.BlockSpec((1, 1, D_MODEL), gate_index),
        ],
        out_specs=out_specs,
        scratch_shapes=[pltpu.VMEM((N_TOK * ROW_TILE, LANES), F32)],
        out_shape=out_shape,
        compiler_params=_params("arbitrary", "arbitrary"),
        name="moe_scatter",
    )(idx.reshape(ns, N_EXPERTS, 1, CAP), ye_tiles, y, mod3)


def _rope_tables(n_tokens):
    t = np.arange(n_tokens)
    row = (t // GRID_W).astype(np.float32)
    col = (t % GRID_W).astype(np.float32)
    pairs = HEAD_DIM // 4
    inv_freq = ROPE_THETA ** (-jnp.arange(pairs, dtype=F32) / pairs)
    ang = jnp.concatenate([row[:, None] * inv_freq, col[:, None] * inv_freq], axis=-1)
    cos, sin = jnp.cos(ang), jnp.sin(ang)
    reps = MXU_DIM // HEAD_DIM
    return (jnp.tile(jnp.concatenate([cos, cos], axis=-1), (1, reps)),
            jnp.tile(jnp.concatenate([-sin, sin], axis=-1), (1, reps)))


def _dup_cache(cache):
    b, s, hk, hd = cache.shape
    return jnp.broadcast_to(cache[:, :, :, None, :], (b, s, hk, 2, hd)).reshape(b, s, 2 * hk * hd).astype(BF16)


def kernel(x_prompt, x_sample, cache_attn_k, cache_attn_v, cache_na_k, cache_na_v, c, c_ctx,
           norm1_g, norm2_g, w_ada, b_ada, attn_w_qkv, attn_q_gain, attn_k_gain, attn_w_o,
           na_w_qkv, na_q_gain, na_k_gain, na_rpb, na_w_o,
           moe_w_router, moe_w_gate, moe_w_up, moe_w_down):
    bc, tc, _ = x_prompt.shape
    bl, tl, _ = x_sample.shape
    depth = w_ada.shape[0]
    assert bc * tc == N_TOK and bl * tl == N_TOK and 1 + bl <= N_MOD_ROWS
    assert tl % TM == 0 and tl % FIN_TOK == 0

    cond = jnp.zeros((N_MOD_ROWS, D_MODEL), F32).at[0].set(c_ctx).at[1:1 + bl].set(c)
    mod3 = _ada(cond, w_ada, b_ada).reshape(depth * N_MOD_ROWS * 6, 1, D_MODEL)

    seg = jnp.asarray(np.kron(np.eye(MXU_DIM // HEAD_DIM), np.ones((HEAD_DIM, HEAD_DIM))) / HEAD_DIM, BF16)
    tri = jnp.asarray(np.triu(np.ones((MXU_DIM, MXU_DIM)), k=1), BF16)
    rope_tables = _rope_tables(tl)

    x_ctx = (x_prompt.reshape(N_TOK, D_MODEL), 0)
    x_lat = (x_sample.reshape(N_TOK, D_MODEL), 0)
    new_k, new_v = [], []
    for i in range(depth):
        j = i // 2
        gqa = i % 2 == 0
        if gqa:
            w_qkv, q_gain, k_gain, w_o = attn_w_qkv[j], attn_q_gain[j], attn_k_gain[j], attn_w_o[j]
            kv_heads = N_KV_HEADS
        else:
            w_qkv, q_gain, k_gain, w_o = na_w_qkv[j], na_q_gain[j], na_k_gain[j], na_w_o[j]
            kv_heads = N_HEADS
        wr_hi, wr_lo = _split_bf16(moe_w_router[i].T)

        w_qkv_b = w_qkv.astype(BF16)
        qkv_args = (norm1_g[i], mod3, i, tl, w_qkv_b, q_gain, k_gain, seg, kv_heads * HEAD_DIM)
        q_c, kb_c, vb_c, kf, vf = _qkv(x_ctx, True, *qkv_args, None)
        q_l, kb_l, vb_l = _qkv(x_lat, False, *qkv_args, rope_tables if gqa else None)
        o_ctx = _attention(*(a.reshape(bc, tc, a.shape[1]) for a in (q_c, kb_c, vb_c)), tq=tc, nb=4)
        q_l, kb_l, vb_l = (a.reshape(bl, tl, a.shape[1]) for a in (q_l, kb_l, vb_l))
        if gqa:
            o_lat = _attention(q_l, kb_l, vb_l,
                               _dup_cache(cache_attn_k[:, j]), _dup_cache(cache_attn_v[:, j]), tq=512)
        else:
            past = cache_na_k.shape[2]
            o_lat = _na_attention(q_l, kb_l, vb_l,
                                  cache_na_k[:, j].reshape(bl, past, D_MODEL).astype(BF16),
                                  cache_na_v[:, j].reshape(bl, past, D_MODEL).astype(BF16),
                                  _na_bias_table(na_rpb[j]))
        new_k.append(kf.reshape(bc, 1, tc, kv_heads, HEAD_DIM))
        new_v.append(vf.reshape(bc, 1, tc, kv_heads, HEAD_DIM))

        y, h2, logits_t = _post_attention(o_ctx.reshape(N_TOK, D_MODEL), o_lat.reshape(N_TOK, D_MODEL),
                                          x_ctx, x_lat, w_o.astype(BF16), norm2_g[i], mod3, i, tl,
                                          wr_hi, wr_lo)
        idx, gc = _plan(logits_t, tri)
        xe = _gather(idx, h2.reshape(N_STREAMS, N_TOK * ROW_TILE, LANES))
        ye = _ffn(xe, gc, moe_w_gate, moe_w_up, moe_w_down, i)
        y = _scatter(idx, ye, y, mod3, i, tl, split=(i == depth - 1))
        x_ctx, x_lat = (y, 0), (y, CTX_TILES)

    y_ctx, y_lat = y
    return (y_ctx.reshape(bc, tc, D_MODEL), y_lat.reshape(bl, tl, D_MODEL),
            jnp.concatenate(new_k[0::2], axis=1), jnp.concatenate(new_v[0::2], axis=1),
            jnp.concatenate(new_k[1::2], axis=1), jnp.concatenate(new_v[1::2], axis=1))
```
